```python
import math
import jax, jax.numpy as jnp
from jax import lax
import numpy as np

D_MODEL = 1024
BATCH = 4
SEQ = 4096
DEPTH = 4

EPS = 1e-6
N_BRANCH = 4
Q_BLOCK = 128

A_PATTERNS = ((128, 1), (512, 4), (2048, 16))
A_GROUPS = len(A_PATTERNS)
A_SLOTS = 6
A_HEAD_DIM = 64
A_HEADS = A_GROUPS * A_SLOTS
A_QKV = A_HEADS * A_HEAD_DIM
A_WIDTH = A_SLOTS * A_HEAD_DIM

B_WIDTH = 384
B_BLOCKS = 6
B_BLOCK_DIM = B_WIDTH // B_BLOCKS
B_CONV = 4
B_CONV_LEFT = 1
B_C = 8.0

C_HEADS = 4
C_HALF = 64
C_VDIM = 2 * C_HALF
C_QK = C_HEADS * 2 * C_HALF
C_WIDTH = C_HEADS * C_VDIM

D_HEADS = 6
D_NOPE = 64
D_ROPE = 32
D_VDIM = 64
D_QLR = 256
D_KVLR = 128
D_WIDTH = D_HEADS * D_VDIM
ROPE_BASE = 10000.0

IN_SPLITS = (
    ("a_q", A_QKV), ("a_k", A_QKV), ("a_v", A_QKV), ("a_g", A_WIDTH),
    ("b_x", B_WIDTH), ("b_g", B_WIDTH),
    ("c_q", C_QK), ("c_k", C_QK), ("c_v", C_WIDTH), ("c_g", C_WIDTH),
    ("d_cq", D_QLR), ("d_ckv", D_KVLR), ("d_kr", D_ROPE), ("d_g", D_WIDTH),
    ("gate", N_BRANCH * D_MODEL),
)
IN_WIDTH = sum(size for _, size in IN_SPLITS)

kernel_name = "hybrid_gated_parallel_mixer_encoder"


def rmsnorm(x, g):
    xf = x.astype(jnp.float32)
    y = xf * lax.rsqrt(jnp.mean(xf * xf, axis=-1, keepdims=True) + EPS)
    return (y * g.astype(jnp.float32)).astype(x.dtype)


def alibi_slopes(n):
    return jnp.asarray([2.0 ** (-8.0 * (i + 1) / n) for i in range(n)], dtype=jnp.float32)


def rope_tables(s):
    inv = ROPE_BASE ** (-jnp.arange(0, D_ROPE, 2, dtype=jnp.float32) / D_ROPE)
    ang = jnp.arange(s, dtype=jnp.float32)[:, None] * inv[None, :]
    return jnp.cos(ang), jnp.sin(ang)


def apply_rope(x, cos, sin):
    x1, x2 = jnp.split(x, 2, axis=-1)
    return jnp.concatenate([x1 * cos - x2 * sin, x1 * sin + x2 * cos], axis=-1)


def split_in(p):
    out, off = {}, 0
    for name, size in IN_SPLITS:
        out[name] = p[..., off:off + size]
        off += size
    return out


def dilated_window_attention(q, k, v, window, dilation, slopes):
    b, s, h, dh = q.shape
    n = window // (2 * dilation)
    L = s // dilation
    nb = -(-L // n)
    Lp = nb * n

    def to_sub(t):
        t = t.reshape(b, L, dilation, h, dh).transpose(0, 2, 3, 1, 4)
        return jnp.pad(t, ((0, 0), (0, 0), (0, 0), (0, Lp - L), (0, 0)))

    def neighbourhood(t):
        t = jnp.pad(to_sub(t), ((0, 0), (0, 0), (0, 0), (n, n), (0, 0)))
        t = t.reshape(b, dilation, h, nb + 2, n, dh)
        return jnp.concatenate([t[:, :, :, :-2], t[:, :, :, 1:-1], t[:, :, :, 2:]], axis=4)

    qs = to_sub(q).reshape(b, dilation, h, nb, n, dh)
    ks, vs = neighbourhood(k), neighbourhood(v)
    qi = jnp.arange(nb)[:, None] * n + jnp.arange(n)[None, :]
    ki = jnp.arange(nb)[:, None] * n - n + jnp.arange(3 * n)[None, :]
    rel = jnp.abs(ki[:, None, :] - qi[:, :, None])
    valid = (rel <= n) & (ki[:, None, :] >= 0) & (ki[:, None, :] < L)
    dist = (rel * dilation).astype(jnp.float32)
    sc = jnp.einsum("brhnqd,brhnkd->brhnqk", qs, ks) * (dh ** -0.5)
    sc = sc - slopes[:, None, None, None] * dist
    sc = jnp.where(valid, sc, -jnp.inf)
    lse = jax.nn.logsumexp(sc, axis=-1)
    o = jnp.einsum("brhnqk,brhnkd->brhnqd", jnp.exp(sc - lse[..., None]), vs)
    o = o.reshape(b, dilation, h, Lp, dh)[:, :, :, :L].transpose(0, 3, 1, 2, 4).reshape(b, s, h, dh)
    lse = lse.reshape(b, dilation, h, Lp)[..., :L].transpose(0, 3, 1, 2).reshape(b, s, h)
    return o, lse


def mixer_a(q, k, v):
    b, s, _ = q.shape
    shp = (b, s, A_GROUPS, A_SLOTS, A_HEAD_DIM)
    q, k, v = q.reshape(shp), k.reshape(shp), v.reshape(shp)
    slopes = alibi_slopes(A_SLOTS)
    outs, lses = [], []
    for g, (window, dilation) in enumerate(A_PATTERNS):
        o, l = dilated_window_attention(q[:, :, g], k[:, :, g], v[:, :, g], window, dilation, slopes)
        outs.append(o)
        lses.append(l)
    wts = jax.nn.softmax(jnp.stack(lses, axis=2), axis=2)
    o = jnp.einsum("bsgh,bsghd->bshd", wts, jnp.stack(outs, axis=2))
    return o.reshape(b, s, A_WIDTH)


def rglru_scan(xc, w_r, b_r, w_i, b_i, lam, reverse):
    b, s, w = xc.shape
    xb = xc.reshape(b, s, B_BLOCKS, B_BLOCK_DIM)
    r = jax.nn.sigmoid(jnp.einsum("bsnc,ncd->bsnd", xb, w_r).reshape(b, s, w) + b_r)
    i = jax.nn.sigmoid(jnp.einsum("bsnc,ncd->bsnd", xb, w_i).reshape(b, s, w) + b_i)
    log_a = -B_C * r * jax.nn.softplus(-lam.astype(jnp.float32))
    a = jnp.exp(log_a)
    u = jnp.sqrt(-jnp.expm1(2.0 * log_a)) * (i * xc)

    def combine(e1, e2):
        a1, u1 = e1
        a2, u2 = e2
        return a1 * a2, a2 * u1 + u2

    _, hseq = lax.associative_scan(combine, (a, u), reverse=reverse, axis=1)
    return hseq


def mixer_b(xb, conv_w, conv_b, w_r, b_r, w_i, b_i, lam):
    s = xb.shape[1]
    xp = jnp.pad(xb, ((0, 0), (B_CONV_LEFT, B_CONV - 1 - B_CONV_LEFT), (0, 0)))
    xc = conv_b + sum(xp[:, j:j + s] * conv_w[j] for j in range(B_CONV))
    h_fwd = rglru_scan(xc, w_r[0], b_r[0], w_i[0], b_i[0], lam[0], reverse=False)
    h_bwd = rglru_scan(xc, w_r[1], b_r[1], w_i[1], b_i[1], lam[1], reverse=True)
    return h_fwd + h_bwd


def mixer_c(q, k, v, lam_q1, lam_k1, lam_q2, lam_k2, subln, layer):
    b, s, _ = q.shape
    nq = s // Q_BLOCK
    q = q.reshape(b, nq, Q_BLOCK, C_HEADS, 2, C_HALF).transpose(1, 0, 2, 3, 4, 5)
    k = k.reshape(b, s, C_HEADS, 2, C_HALF)
    v = v.reshape(b, s, C_HEADS, C_VDIM)
    lam_init = 0.8 - 0.6 * math.exp(-0.3 * layer)
    lam = jnp.exp(jnp.sum(lam_q1 * lam_k1)) - jnp.exp(jnp.sum(lam_q2 * lam_k2)) + lam_init
    slopes = alibi_slopes(C_HEADS)
    pos = jnp.arange(s, dtype=jnp.float32)
    qpos = pos.reshape(nq, Q_BLOCK)

    def block(args):
        qb, qp = args
        sc = jnp.einsum("bqhcd,bkhcd->bhcqk", qb, k) * (C_HALF ** -0.5)
        sc = sc - slopes[None, :, None, None, None] * jnp.abs(qp[:, None] - pos[None, :])
        p = jax.nn.softmax(sc, axis=-1)
        return jnp.einsum("bhqk,bkhd->bqhd", p[:, :, 0] - lam * p[:, :, 1], v)

    o = lax.map(block, (q, qpos))
    o = o.transpose(1, 0, 2, 3, 4).reshape(b, s, C_HEADS, C_VDIM)
    o = rmsnorm(o, subln) * (1.0 - lam_init)
    return o.reshape(b, s, C_WIDTH)


def mixer_d(c_q, c_kv, k_rope, q_norm, kv_norm, w_uq, w_ukv, cos, sin):
    b, s, _ = c_q.shape
    nq = s // Q_BLOCK
    q = jnp.einsum("bsr,re->bse", rmsnorm(c_q, q_norm), w_uq).reshape(b, s, D_HEADS, D_NOPE + D_ROPE)
    kv = jnp.einsum("bsr,re->bse", rmsnorm(c_kv, kv_norm), w_ukv).reshape(b, s, D_HEADS, D_NOPE + D_VDIM)
    q_rope = apply_rope(q[..., D_NOPE:], cos[None, :, None], sin[None, :, None])
    k_nope, v = kv[..., :D_NOPE], kv[..., D_NOPE:]
    k_rope = apply_rope(k_rope, cos[None], sin[None])
    scale = (D_NOPE + D_ROPE) ** -0.5
    qn = q[..., :D_NOPE].reshape(b, nq, Q_BLOCK, D_HEADS, D_NOPE).transpose(1, 0, 2, 3, 4)
    qr = q_rope.reshape(b, nq, Q_BLOCK, D_HEADS, D_ROPE).transpose(1, 0, 2, 3, 4)

    def block(args):
        qnb, qrb = args
        sc = (jnp.einsum("bqhd,bkhd->bhqk", qnb, k_nope)
              + jnp.einsum("bqhr,bkr->bhqk", qrb, k_rope)) * scale
        p = jax.nn.softmax(sc, axis=-1)
        return jnp.einsum("bhqk,bkhd->bqhd", p, v)

    o = lax.map(block, (qn, qr))
    return o.transpose(1, 0, 2, 3, 4).reshape(b, s, D_WIDTH)


def setup_inputs(seed: int = 0) -> dict:
    key = jax.random.key(seed)
    ks = jax.random.split(key, 28)
    f32 = jnp.float32

    def nrm(k, shape, scale):
        return jax.random.normal(k, shape, f32) * scale

    def gain(k, shape):
        return 1.0 + 0.02 * jax.random.normal(k, shape, f32)

    a0 = jax.random.uniform(ks[11], (DEPTH, 2, B_WIDTH), f32, 0.9, 0.999)
    return {
        "x": nrm(ks[0], (BATCH, SEQ, D_MODEL), 1.0),
        "norm_pre": gain(ks[1], (DEPTH, D_MODEL)),
        "norm_post": gain(ks[2], (DEPTH, D_MODEL)),
        "w_in": nrm(ks[3], (DEPTH, D_MODEL, IN_WIDTH), D_MODEL ** -0.5),
        "conv_w": nrm(ks[4], (DEPTH, B_CONV, B_WIDTH), B_CONV ** -0.5),
        "conv_b": nrm(ks[5], (DEPTH, B_WIDTH), 0.01),
        "lru_wr": nrm(ks[6], (DEPTH, 2, B_BLOCKS, B_BLOCK_DIM, B_BLOCK_DIM), B_BLOCK_DIM ** -0.5),
        "lru_br": nrm(ks[7], (DEPTH, 2, B_WIDTH), 0.01),
        "lru_wi": nrm(ks[8], (DEPTH, 2, B_BLOCKS, B_BLOCK_DIM, B_BLOCK_DIM), B_BLOCK_DIM ** -0.5),
        "lru_bi": nrm(ks[9], (DEPTH, 2, B_WIDTH), 0.01),
        "lru_lambda": jnp.log(a0) - jnp.log1p(-a0),
        "diff_lam_q1": nrm(ks[12], (DEPTH, C_HALF), 0.1),
        "diff_lam_k1": nrm(ks[13], (DEPTH, C_HALF), 0.1),
        "diff_lam_q2": nrm(ks[14], (DEPTH, C_HALF), 0.1),
        "diff_lam_k2": nrm(ks[15], (DEPTH, C_HALF), 0.1),
        "diff_subln": gain(ks[16], (DEPTH, C_VDIM)),
        "mla_q_norm": gain(ks[17], (DEPTH, D_QLR)),
        "mla_kv_norm": gain(ks[18], (DEPTH, D_KVLR)),
        "mla_w_uq": nrm(ks[19], (DEPTH, D_QLR, D_HEADS * (D_NOPE + D_ROPE)), D_QLR ** -0.5),
        "mla_w_ukv": nrm(ks[20], (DEPTH, D_KVLR, D_HEADS * (D_NOPE + D_VDIM)), D_KVLR ** -0.5),
        "w_br_a": nrm(ks[21], (DEPTH, A_WIDTH, D_MODEL), A_WIDTH ** -0.5),
        "w_br_b": nrm(ks[22], (DEPTH, B_WIDTH, D_MODEL), B_WIDTH ** -0.5),
        "w_br_c": nrm(ks[23], (DEPTH, C_WIDTH, D_MODEL), C_WIDTH ** -0.5),
        "w_br_d": nrm(ks[24], (DEPTH, D_WIDTH, D_MODEL), D_WIDTH ** -0.5),
        "b_gate": nrm(ks[25], (DEPTH, N_BRANCH, D_MODEL), 0.01),
        "w_out": nrm(ks[26], (DEPTH, D_MODEL, D_MODEL), D_MODEL ** -0.5),
    }


def reference(x, norm_pre, norm_post, w_in, conv_w, conv_b, lru_wr, lru_br, lru_wi, lru_bi,
              lru_lambda, diff_lam_q1, diff_lam_k1, diff_lam_q2, diff_lam_k2, diff_subln,
              mla_q_norm, mla_kv_norm, mla_w_uq, mla_w_ukv, w_br_a, w_br_b, w_br_c, w_br_d,
              b_gate, w_out):
    b, s, _ = x.shape
    cos, sin = rope_tables(s)
    for l in range(DEPTH):
        h = rmsnorm(x, norm_pre[l])
        p = split_in(jnp.einsum("bsd,de->bse", h, w_in[l], preferred_element_type=jnp.float32))
        y_a = mixer_a(p["a_q"], p["a_k"], p["a_v"]) * jax.nn.silu(p["a_g"])
        y_b = mixer_b(p["b_x"], conv_w[l], conv_b[l], lru_wr[l], lru_br[l], lru_wi[l], lru_bi[l],
                      lru_lambda[l]) * jax.nn.silu(p["b_g"])
        y_c = mixer_c(p["c_q"], p["c_k"], p["c_v"], diff_lam_q1[l], diff_lam_k1[l], diff_lam_q2[l],
                      diff_lam_k2[l], diff_subln[l], l) * jax.nn.silu(p["c_g"])
        y_d = mixer_d(p["d_cq"], p["d_ckv"], p["d_kr"], mla_q_norm[l], mla_kv_norm[l], mla_w_uq[l],
                      mla_w_ukv[l], cos, sin) * jax.nn.silu(p["d_g"])
        g = jax.nn.sigmoid(p["gate"].reshape(b, s, N_BRANCH, D_MODEL) + b_gate[l])
        merged = (g[:, :, 0] * (y_a @ w_br_a[l]) + g[:, :, 1] * (y_b @ w_br_b[l])
                  + g[:, :, 2] * (y_c @ w_br_c[l]) + g[:, :, 3] * (y_d @ w_br_d[l]))
        x = x + rmsnorm(merged @ w_out[l], norm_post[l]).astype(x.dtype)
    return x
```

```python
import functools
import math

import jax
import jax.numpy as jnp
from jax import lax
from jax.experimental import pallas as pl
from jax.experimental.pallas import tpu as pltpu

F32 = jnp.float32
BF16 = jnp.bfloat16

D_MODEL = 1024
EPS = 1e-6
N_BRANCH = 4

A_PATTERNS = ((128, 1), (512, 4), (2048, 16))
A_SLOTS = 6
A_HEAD_DIM = 64
A_QKV = 1152
A_WIDTH = 384
A_RADIUS = 64
A_QBLK = 128
A_KWIN = 256

B_WIDTH = 384
B_BLOCK_DIM = 64
B_C = 8.0

C_HEADS = 4
C_HALF = 64
C_VDIM = 128
C_QK = 512
C_WIDTH = 512

D_HEADS = 6
D_NOPE = 64
D_ROPE = 32
D_VDIM = 64
D_QLR = 256
D_KVLR = 128
D_WIDTH = 384
ROPE_BASE = 10000.0

LANE = 128
SUBLANE = 8
NEG = -1e30
VMEM_LIMIT = 56 * 1024 * 1024

U_GATE, U_DCQ, U_DCKV, U_DKR, U_DG = 0, 32, 34, 35, 36
U_AQ, U_AK, U_AV, U_AG = 39, 48, 57, 66
U_BX, U_BG = 69, 72
U_CQ, U_CK, U_CV, U_CG = 75, 79, 83, 87
U_TOTAL = 92
P_WIDTH = U_TOTAL * LANE

_ORIG = {}
_off = 0
for _name, _size in (("a_q", A_QKV), ("a_k", A_QKV), ("a_v", A_QKV), ("a_g", A_WIDTH),
                     ("b_x", B_WIDTH), ("b_g", B_WIDTH),
                     ("c_q", C_QK), ("c_k", C_QK), ("c_v", C_WIDTH), ("c_g", C_WIDTH),
                     ("d_cq", D_QLR), ("d_ckv", D_KVLR), ("d_kr", D_ROPE), ("d_g", D_WIDTH),
                     ("gate", N_BRANCH * D_MODEL)):
    _ORIG[_name] = (_off, _size)
    _off += _size


def _cparams(*sem):
    return pltpu.CompilerParams(dimension_semantics=sem, vmem_limit_bytes=VMEM_LIMIT)


def _silu(x):
    return x * (1.0 / (1.0 + jnp.exp(-x)))


def _sigmoid(x):
    return 1.0 / (1.0 + jnp.exp(-x))


def _inproj_kernel(x_ref, g_ref, w_ref, o_ref, h_ref):
    @pl.when(pl.program_id(1) == 0)
    def _():
        x = x_ref[...]
        ms = jnp.mean(x * x, axis=-1, keepdims=True)
        h_ref[...] = (x * lax.rsqrt(ms + EPS) * g_ref[...]).astype(BF16)

    o_ref[...] = jnp.dot(h_ref[...], w_ref[...], preferred_element_type=F32).astype(o_ref.dtype)


def _inproj(x2, gain, w, tm, tn):
    n = x2.shape[0]
    return pl.pallas_call(
        _inproj_kernel,
        grid=(n // tm, P_WIDTH // tn),
        in_specs=[pl.BlockSpec((tm, D_MODEL), lambda i, j: (i, 0)),
                  pl.BlockSpec((1, D_MODEL), lambda i, j: (0, 0)),
                  pl.BlockSpec((D_MODEL, tn), lambda i, j: (0, j))],
        out_specs=pl.BlockSpec((tm, tn), lambda i, j: (i, j)),
        out_shape=jax.ShapeDtypeStruct((n, P_WIDTH), BF16),
        scratch_shapes=[pltpu.VMEM((tm, D_MODEL), BF16)],
        compiler_params=_cparams("parallel", "arbitrary"),
        name="inproj",
    )(x2, gain, w)


def _a_bias_tables(slopes):
    ii = jnp.arange(A_QBLK, dtype=jnp.int32)[:, None]
    jj = jnp.arange(A_KWIN, dtype=jnp.int32)[None, :]
    out = []
    for _, dil in A_PATTERNS:
        per_edge = []
        for off in (0, A_RADIUS, A_QBLK):
            rel = jnp.abs(off + ii - jj)
            dist = (rel * dil).astype(F32)
            b = -slopes[:, None, None] * dist[None]
            per_edge.append(jnp.where((rel <= A_RADIUS)[None], b, NEG))
        out.append(jnp.stack(per_edge))
    return jnp.stack(out)


def _a_kernel(q0, k0, v0, q1, k1, v1, q2, k2, v2, gate_ref, bias_ref, o_ref,
              qf, kf, vf, u_ref, z_ref, m_ref):
    s_len = q0.shape[0]
    lane = lax.broadcasted_iota(jnp.int32, (1, LANE), 1)
    first = lane < A_HEAD_DIM
    ones_first = jnp.where(first, 1.0, 0.0).astype(BF16)
    ones_second = jnp.where(first, 0.0, 1.0).astype(BF16)
    scale = A_HEAD_DIM ** -0.5

    for g, ((_, dil), (qr, kr, vr)) in enumerate(zip(A_PATTERNS, ((q0, k0, v0), (q1, k1, v1), (q2, k2, v2)))):
        sub_len = s_len // dil
        nqb = sub_len // A_QBLK
        if dil > 1:
            qf[...] = qr[...].astype(F32)
            kf[...] = kr[...].astype(F32)
            vf[...] = vr[...].astype(F32)

        def block(idx, carry, g=g, dil=dil, sub_len=sub_len, nqb=nqb, qr=qr, kr=kr, vr=vr):
            r = idx // nqb
            qb = idx % nqb
            qs = qb * A_QBLK
            ws = jnp.clip(qs - A_RADIUS, 0, sub_len - A_KWIN)
            edge = jnp.where(qb == 0, 0, jnp.where(qb == nqb - 1, 2, 1))
            if dil == 1:
                qrows = pl.ds(pl.multiple_of(qs, A_QBLK), A_QBLK)
                krows = pl.ds(pl.multiple_of(ws, A_RADIUS), A_KWIN)
                q = qr[qrows, :]
                k = kr[krows, :]
                v = vr[krows, :]
            else:
                qrows = pl.ds(r + qs * dil, A_QBLK, stride=dil)
                krows = pl.ds(r + ws * dil, A_KWIN, stride=dil)
                q = qf[qrows, :].astype(BF16)
                k = kf[krows, :].astype(BF16)
                v = vf[krows, :].astype(BF16)
            zq = jnp.zeros_like(q)
            zv = jnp.zeros_like(v)
            uz = None
            ms = []
            for h, head_lanes in enumerate((first, jnp.logical_not(first))):
                qh = jnp.where(head_lanes, q, zq)
                s = lax.dot_general(qh, k, (((1,), (1,)), ((), ())), preferred_element_type=F32)
                s = s * scale + bias_ref[g, edge, h]
                mh = jnp.max(s, axis=-1, keepdims=True)
                p = jnp.exp(s - mh).astype(BF16)
                ones_h = ones_first if h == 0 else ones_second
                vaug = jnp.concatenate([jnp.where(head_lanes, v, zv),
                                        jnp.broadcast_to(ones_h, v.shape)], axis=1)
                part = jnp.dot(p, vaug, preferred_element_type=F32)
                uz = part if uz is None else uz + part
                ms.append(mh)
            u = uz[:, :LANE]
            z = uz[:, LANE:]
            m = jnp.where(first, ms[0], ms[1])
            if g == 0:
                u_ref[qrows, :] = u
                z_ref[qrows, :] = z
                m_ref[qrows, :] = m
            else:
                m_old = m_ref[qrows, :]
                m_new = jnp.maximum(m_old, m)
                a = jnp.exp(m_old - m_new)
                b = jnp.exp(m - m_new)
                u_ref[qrows, :] = a * u_ref[qrows, :] + b * u
                z_ref[qrows, :] = a * z_ref[qrows, :] + b * z
                m_ref[qrows, :] = m_new
            return carry

        lax.fori_loop(0, dil * nqb, block, 0)

    gate = gate_ref[...].astype(F32)
    o_ref[...] = (u_ref[...] / z_ref[...] * _silu(gate)).astype(o_ref.dtype)


def _mixer_a(p3, bias):
    b, s, _ = p3.shape
    npair = A_SLOTS // 2
    assert s // A_PATTERNS[-1][1] >= A_KWIN

    def col(unit):
        return pl.BlockSpec((None, s, LANE), lambda bi, hp, unit=unit: (bi, 0, unit + hp))

    in_specs = []
    for g in range(len(A_PATTERNS)):
        for base in (U_AQ, U_AK, U_AV):
            in_specs.append(col(base + g * npair))
    in_specs.append(col(U_AG))
    in_specs.append(pl.BlockSpec((len(A_PATTERNS), 3, 2, A_QBLK, A_KWIN), lambda bi, hp: (0, 0, hp, 0, 0)))
    return pl.pallas_call(
        _a_kernel,
        grid=(b, npair),
        in_specs=in_specs,
        out_specs=pl.BlockSpec((None, s, LANE), lambda bi, hp: (bi, 0, hp)),
        out_shape=jax.ShapeDtypeStruct((b, s, A_WIDTH), BF16),
        scratch_shapes=[pltpu.VMEM((s, LANE), F32)] * 6,
        compiler_params=_cparams("parallel", "parallel"),
        name="mixer_a",
    )(*([p3] * 10), bias)


B_CHUNK = 512
B_PAD = 8


def _b_kernel(x_ref, g_ref, cw_ref, cb_ref, w_ref, bias_ref, sp_ref, o_ref, xs_ref, a_ref, u_ref, hf_ref):
    s_len = x_ref.shape[0]
    nchunk = s_len // B_CHUNK
    ntile = B_CHUNK // SUBLANE
    zpad = jnp.zeros((B_PAD, LANE), F32)
    xs_ref[pl.ds(0, B_PAD), :] = zpad
    xs_ref[pl.ds(B_PAD + s_len, B_PAD), :] = zpad
    xs_ref[pl.ds(B_PAD, s_len), :] = x_ref[...].astype(F32)
    row = lax.broadcasted_iota(jnp.int32, (SUBLANE, LANE), 0)

    def gates(c, direction):
        t0 = pl.multiple_of(c * B_CHUNK, B_CHUNK)
        win = xs_ref[pl.ds(t0, B_CHUNK + 2 * B_PAD), :]
        n = B_CHUNK + 2 * B_PAD
        xc = (cb_ref[...]
              + pltpu.roll(win, 1, axis=0) * cw_ref[0:1, :]
              + win * cw_ref[1:2, :]
              + pltpu.roll(win, n - 1, axis=0) * cw_ref[2:3, :]
              + pltpu.roll(win, n - 2, axis=0) * cw_ref[3:4, :])
        xc = xc[B_PAD:B_PAD + B_CHUNK, :]
        ri = jnp.dot(xc.astype(BF16), w_ref[direction], preferred_element_type=F32)
        ri = ri + bias_ref[direction:direction + 1, :]
        r = _sigmoid(ri[:, :LANE])
        i = _sigmoid(ri[:, LANE:])
        log_a = (-B_C) * r * sp_ref[direction:direction + 1, :]
        a = jnp.exp(log_a)
        a_ref[...] = a
        u_ref[...] = jnp.sqrt(-jnp.tanh(log_a) * (a * a + 1.0)) * (i * xc)

    def tile_scan(a, u, h_in, reverse):
        acc_a, acc_h = a, u
        for sh in (1, 2, 4):
            if reverse:
                sa = pltpu.roll(acc_a, SUBLANE - sh, axis=0)
                shh = pltpu.roll(acc_h, SUBLANE - sh, axis=0)
                keep = row < SUBLANE - sh
            else:
                sa = pltpu.roll(acc_a, sh, axis=0)
                shh = pltpu.roll(acc_h, sh, axis=0)
                keep = row >= sh
            acc_h = acc_h + acc_a * jnp.where(keep, shh, 0.0)
            acc_a = acc_a * jnp.where(keep, sa, 1.0)
        return acc_h + acc_a * h_in

    def fwd_chunk(c, h):
        gates(c, 0)
        t0 = c * B_CHUNK

        def tile(t, h):
            rows = pl.ds(pl.multiple_of(t * SUBLANE, SUBLANE), SUBLANE)
            hh = tile_scan(a_ref[rows, :], u_ref[rows, :], h, False)
            hf_ref[pl.ds(pl.multiple_of(t0 + t * SUBLANE, SUBLANE), SUBLANE), :] = hh
            return jnp.broadcast_to(hh[SUBLANE - 1:SUBLANE, :], (SUBLANE, LANE))

        return lax.fori_loop(0, ntile, tile, h)

    lax.fori_loop(0, nchunk, fwd_chunk, jnp.zeros((SUBLANE, LANE), F32))

    def bwd_chunk(ci, h):
        c = nchunk - 1 - ci
        gates(c, 1)
        t0 = c * B_CHUNK

        def tile(ti, h):
            t = ntile - 1 - ti
            rows = pl.ds(pl.multiple_of(t * SUBLANE, SUBLANE), SUBLANE)
            hh = tile_scan(a_ref[rows, :], u_ref[rows, :], h, True)
            orow = pl.ds(pl.multiple_of(t0 + t * SUBLANE, SUBLANE), SUBLANE)
            hf_ref[orow, :] = hf_ref[orow, :] + hh
            return jnp.broadcast_to(hh[0:1, :], (SUBLANE, LANE))

        return lax.fori_loop(0, ntile, tile, h)

    lax.fori_loop(0, nchunk, bwd_chunk, jnp.zeros((SUBLANE, LANE), F32))
    o_ref[...] = (hf_ref[...] * _silu(g_ref[...].astype(F32))).astype(o_ref.dtype)


def _mixer_b(p3, conv_w, conv_b, w_gate, b_gate, softplus_neg_lam):
    b, s, _ = p3.shape
    ngrp = B_WIDTH // LANE
    assert s % B_CHUNK == 0
    return pl.pallas_call(
        _b_kernel,
        grid=(b, ngrp),
        in_specs=[pl.BlockSpec((None, s, LANE), lambda bi, j: (bi, 0, U_BX + j)),
                  pl.BlockSpec((None, s, LANE), lambda bi, j: (bi, 0, U_BG + j)),
                  pl.BlockSpec((4, LANE), lambda bi, j: (0, j)),
                  pl.BlockSpec((1, LANE), lambda bi, j: (0, j)),
                  pl.BlockSpec((None, 2, LANE, 2 * LANE), lambda bi, j: (j, 0, 0, 0)),
                  pl.BlockSpec((None, 2, 2 * LANE), lambda bi, j: (j, 0, 0)),
                  pl.BlockSpec((2, LANE), lambda bi, j: (0, j))],
        out_specs=pl.BlockSpec((None, s, LANE), lambda bi, j: (bi, 0, j)),
        out_shape=jax.ShapeDtypeStruct((b, s, B_WIDTH), BF16),
        scratch_shapes=[pltpu.VMEM((s + 2 * B_PAD, LANE), F32),
                        pltpu.VMEM((B_CHUNK, LANE), F32),
                        pltpu.VMEM((B_CHUNK, LANE), F32),
                        pltpu.VMEM((s, LANE), F32)],
        compiler_params=_cparams("parallel", "parallel"),
        name="mixer_b",
    )(p3, p3, conv_w, conv_b, w_gate, b_gate, softplus_neg_lam)


def _b_gate_weights(w_r, b_r, w_i, b_i):
    per = LANE // B_BLOCK_DIM
    ngrp = B_WIDTH // LANE

    def blockdiag(w):
        w = w.reshape(2, ngrp, per, B_BLOCK_DIM, B_BLOCK_DIM)
        eye = jnp.eye(per, dtype=w.dtype)
        full = jnp.einsum("dgpcx,pq->dgpcqx", w, eye).reshape(2, ngrp, LANE, LANE)
        return full.transpose(1, 0, 2, 3)

    w = jnp.concatenate([blockdiag(w_r), blockdiag(w_i)], axis=-1).astype(BF16)
    bias = jnp.concatenate([b_r.reshape(2, ngrp, LANE), b_i.reshape(2, ngrp, LANE)], axis=-1)
    return w, bias.transpose(1, 0, 2)


def _attend(q, k_ref, v_ref, tk, m_ref, l_ref, acc_ref, bias_fn=None):
    nk = k_ref.shape[0] // tk
    m_ref[...] = jnp.full(m_ref.shape, NEG, F32)
    l_ref[...] = jnp.zeros(l_ref.shape, F32)
    acc_ref[...] = jnp.zeros(acc_ref.shape, F32)

    def body(t, carry):
        rows = pl.ds(pl.multiple_of(t * tk, tk), tk)
        s = lax.dot_general(q, k_ref[rows, :], (((1,), (1,)), ((), ())), preferred_element_type=F32)
        if bias_fn is not None:
            s = s + bias_fn(t)
        m_old = m_ref[...]
        m_new = jnp.maximum(m_old, jnp.max(s, axis=-1, keepdims=True))
        alpha = jnp.exp(m_old - m_new)
        p = jnp.exp(s - m_new)
        l_ref[...] = alpha * l_ref[...] + jnp.sum(p, axis=-1, keepdims=True)
        acc_ref[...] = alpha * acc_ref[...] + jnp.dot(p.astype(BF16), v_ref[rows, :],
                                                      preferred_element_type=F32)
        m_ref[...] = m_new
        return carry

    lax.fori_loop(0, nk, body, 0)
    return acc_ref[...], l_ref[...]


def _c_kernel(q_ref, k_ref, v_ref, g_ref, lam_ref, subln_ref, o_ref, m_ref, l_ref, acc_ref,
              *, tk, slopes, lam_init):
    h = pl.program_id(1)
    tq = q_ref.shape[0]
    q0 = pl.program_id(2) * tq
    lp = lam_ref[...]
    lam = (jnp.exp(jnp.sum(lp[0:1, :] * lp[1:2, :], axis=-1, keepdims=True))
           - jnp.exp(jnp.sum(lp[2:3, :] * lp[3:4, :], axis=-1, keepdims=True)) + lam_init)
    slope = jnp.where(h == 0, slopes[0], jnp.where(h == 1, slopes[1], jnp.where(h == 2, slopes[2], slopes[3])))
    slope = slope.astype(F32)
    qpos = q0 + lax.broadcasted_iota(jnp.int32, (tq, 1), 0)
    kio = lax.broadcasted_iota(jnp.int32, (1, tk), 1)

    def bias_fn(t):
        return -slope * jnp.abs(qpos - (kio + t * tk)).astype(F32)

    lane = lax.broadcasted_iota(jnp.int32, (1, LANE), 1)
    q = q_ref[...]
    qs = (q.astype(F32) * (C_HALF ** -0.5)).astype(BF16)
    zq = jnp.zeros_like(qs)
    outs = []
    for c in range(2):
        sel = (lane < C_HALF) if c == 0 else (lane >= C_HALF)
        acc, l = _attend(jnp.where(sel, qs, zq), k_ref, v_ref, tk, m_ref, l_ref, acc_ref, bias_fn)
        outs.append(acc / l)
    o = outs[0] - lam * outs[1]
    ms = jnp.mean(o * o, axis=-1, keepdims=True)
    o = o * lax.rsqrt(ms + EPS) * subln_ref[...] * (1.0 - lam_init)
    o_ref[...] = (o * _silu(g_ref[...].astype(F32))).astype(o_ref.dtype)


def _mixer_c(p3, lam_params, subln, layer, tq, tk):
    b, s, _ = p3.shape
    lam_init = 0.8 - 0.6 * math.exp(-0.3 * layer)
    slopes = tuple(2.0 ** (-8.0 * (i + 1) / C_HEADS) for i in range(C_HEADS))
    kern = functools.partial(_c_kernel, tk=tk, slopes=slopes, lam_init=lam_init)
    return pl.pallas_call(
        kern,
        grid=(b, C_HEADS, s // tq),
        in_specs=[pl.BlockSpec((None, tq, LANE), lambda bi, h, i: (bi, i, U_CQ + h)),
                  pl.BlockSpec((None, s, LANE), lambda bi, h, i: (bi, 0, U_CK + h)),
                  pl.BlockSpec((None, s, LANE), lambda bi, h, i: (bi, 0, U_CV + h)),
                  pl.BlockSpec((None, tq, LANE), lambda bi, h, i: (bi, i, U_CG + h)),
                  pl.BlockSpec((4, C_HALF), lambda bi, h, i: (0, 0)),
                  pl.BlockSpec((1, C_VDIM), lambda bi, h, i: (0, 0))],
        out_specs=pl.BlockSpec((None, tq, LANE), lambda bi, h, i: (bi, i, h)),
        out_shape=jax.ShapeDtypeStruct((b, s, C_WIDTH), BF16),
        scratch_shapes=[pltpu.VMEM((tq, 1), F32), pltpu.VMEM((tq, 1), F32), pltpu.VMEM((tq, C_VDIM), F32)],
        compiler_params=_cparams("parallel", "parallel", "arbitrary"),
        name="mixer_c",
    )(p3, p3, p3, p3, lam_params, subln)


def _dprep_kernel(cq_ref, ckv_ref, kr_ref, qn_ref, kvn_ref, wq_ref, wqs_ref, wk_ref, wv_ref,
                  e1_ref, e2_ref, cos_ref, sin_ref, vone_ref, q_out, k_out, v_out):
    def norm(x_ref, gain_ref):
        x = x_ref[...].astype(F32)
        ms = jnp.mean(x * x, axis=-1, keepdims=True)
        return (x * lax.rsqrt(ms + EPS) * gain_ref[...]).astype(BF16)

    cqn = norm(cq_ref, qn_ref)
    ckvn = norm(ckv_ref, kvn_ref)
    kr = kr_ref[...]
    cos = jnp.concatenate([cos_ref[...]] * D_HEADS, axis=1)
    sin = jnp.concatenate([sin_ref[...]] * D_HEADS, axis=1)
    scale = (D_NOPE + D_ROPE) ** -0.5
    q = (jnp.dot(cqn, wq_ref[...], preferred_element_type=F32) * cos
         + jnp.dot(cqn, wqs_ref[...], preferred_element_type=F32) * sin)
    q_out[...] = (q * scale).astype(q_out.dtype)
    k = (jnp.dot(ckvn, wk_ref[...], preferred_element_type=F32)
         + jnp.dot(kr, e1_ref[...], preferred_element_type=F32) * cos
         + jnp.dot(kr, e2_ref[...], preferred_element_type=F32) * sin)
    k_out[...] = k.astype(k_out.dtype)
    v = jnp.dot(ckvn, wv_ref[...], preferred_element_type=F32) + vone_ref[...]
    v_out[...] = v.astype(v_out.dtype)


def _d_tables(s):
    half = D_ROPE // 2
    inv = ROPE_BASE ** (-jnp.arange(0, D_ROPE, 2, dtype=F32) / D_ROPE)
    ang = jnp.arange(s, dtype=F32)[:, None] * inv[None, :]
    cos, sin = jnp.cos(ang), jnp.sin(ang)
    pad = LANE - D_NOPE - D_ROPE
    cos_t = jnp.concatenate([jnp.ones((s, D_NOPE), F32), cos, cos, jnp.zeros((s, pad), F32)], axis=1)
    sin_t = jnp.concatenate([jnp.zeros((s, D_NOPE), F32), -sin, sin, jnp.zeros((s, pad), F32)], axis=1)
    del half
    return cos_t, sin_t


def _d_weights(w_uq, w_ukv):
    half = D_ROPE // 2
    pad = LANE - D_NOPE - D_ROPE
    wq = w_uq.reshape(D_QLR, D_HEADS, D_NOPE + D_ROPE)
    zq = jnp.zeros((D_QLR, D_HEADS, pad), w_uq.dtype)
    wq_main = jnp.concatenate([wq, zq], axis=-1).reshape(D_QLR, D_HEADS * LANE)
    wq_swap = jnp.concatenate([jnp.zeros((D_QLR, D_HEADS, D_NOPE), w_uq.dtype),
                               wq[..., D_NOPE + half:], wq[..., D_NOPE:D_NOPE + half], zq],
                              axis=-1).reshape(D_QLR, D_HEADS * LANE)
    wkv = w_ukv.reshape(D_KVLR, D_HEADS, D_NOPE + D_VDIM)
    zk = jnp.zeros((D_KVLR, D_HEADS, LANE - D_NOPE), w_ukv.dtype)
    wk = jnp.concatenate([wkv[..., :D_NOPE], zk], axis=-1).reshape(D_KVLR, D_HEADS * LANE)
    wv = jnp.concatenate([wkv[..., D_NOPE:], jnp.zeros((D_KVLR, D_HEADS, LANE - D_VDIM), w_ukv.dtype)],
                         axis=-1).reshape(D_KVLR, D_HEADS * LANE)
    src = jnp.arange(D_ROPE)
    e1 = jnp.zeros((LANE, LANE), F32).at[src, D_NOPE + src].set(1.0)
    e2 = jnp.zeros((LANE, LANE), F32).at[(src + half) % D_ROPE, D_NOPE + src].set(1.0)
    e1 = jnp.tile(e1, (1, D_HEADS))
    e2 = jnp.tile(e2, (1, D_HEADS))
    vone = jnp.zeros((1, LANE), F32).at[0, D_VDIM].set(1.0)
    vone = jnp.tile(vone, (1, D_HEADS))
    return (wq_main.astype(BF16), wq_swap.astype(BF16), wk.astype(BF16), wv.astype(BF16),
            e1.astype(BF16), e2.astype(BF16), vone)


def _dprep(p3, q_norm, kv_norm, dw, cos_t, sin_t, tm):
    b, s, _ = p3.shape
    wq, wqs, wk, wv, e1, e2, vone = dw
    wide = D_HEADS * LANE
    full = lambda shape: pl.BlockSpec(shape, lambda bi, i: (0,) * len(shape))
    out_spec = pl.BlockSpec((None, tm, wide), lambda bi, i: (bi, i, 0))
    return pl.pallas_call(
        _dprep_kernel,
        grid=(b, s // tm),
        in_specs=[pl.BlockSpec((None, tm, D_QLR), lambda bi, i: (bi, i, U_DCQ // 2)),
                  pl.BlockSpec((None, tm, LANE), lambda bi, i: (bi, i, U_DCKV)),
                  pl.BlockSpec((None, tm, LANE), lambda bi, i: (bi, i, U_DKR)),
                  full((1, D_QLR)), full((1, D_KVLR)),
                  full((D_QLR, wide)), full((D_QLR, wide)), full((D_KVLR, wide)), full((D_KVLR, wide)),
                  full((LANE, wide)), full((LANE, wide)),
                  pl.BlockSpec((tm, LANE), lambda bi, i: (i, 0)),
                  pl.BlockSpec((tm, LANE), lambda bi, i: (i, 0)),
                  full((1, wide))],
        out_specs=[out_spec, out_spec, out_spec],
        out_shape=[jax.ShapeDtypeStruct((b, s, wide), BF16)] * 3,
        compiler_params=_cparams("parallel", "parallel"),
        name="mixer_d_prep",
    )(p3, p3, p3, q_norm, kv_norm, wq, wqs, wk, wv, e1, e2, cos_t, sin_t, vone)


def _d_kernel(q_ref, k_ref, v_ref, g_ref, o_ref, m_ref, l_ref, acc_ref, *, tk):
    outs = []
    for h in range(2):
        cols = slice(h * LANE, (h + 1) * LANE)
        acc, _ = _attend(q_ref[:, cols], k_ref.at[:, cols], v_ref.at[:, cols], tk, m_ref, l_ref, acc_ref)
        o = acc / acc[:, D_VDIM:D_VDIM + 1]
        outs.append(o[:, :D_VDIM])
    o = jnp.concatenate(outs, axis=1)
    o_ref[...] = (o * _silu(g_ref[...].astype(F32))).astype(o_ref.dtype)


def _mixer_d(p3, qd, kd, vd, tq, tk):
    b, s, _ = p3.shape
    npair = D_HEADS // 2
    return pl.pallas_call(
        functools.partial(_d_kernel, tk=tk),
        grid=(b, npair, s // tq),
        in_specs=[pl.BlockSpec((None, tq, 2 * LANE), lambda bi, hp, i: (bi, i, hp)),
                  pl.BlockSpec((None, s, 2 * LANE), lambda bi, hp, i: (bi, 0, hp)),
                  pl.BlockSpec((None, s, 2 * LANE), lambda bi, hp, i: (bi, 0, hp)),
                  pl.BlockSpec((None, tq, LANE), lambda bi, hp, i: (bi, i, U_DG + hp))],
        out_specs=pl.BlockSpec((None, tq, LANE), lambda bi, hp, i: (bi, i, hp)),
        out_shape=jax.ShapeDtypeStruct((b, s, D_WIDTH), BF16),
        scratch_shapes=[pltpu.VMEM((tq, 1), F32), pltpu.VMEM((tq, 1), F32), pltpu.VMEM((tq, LANE), F32)],
        compiler_params=_cparams("parallel", "parallel", "arbitrary"),
        name="mixer_d",
    )(qd, kd, vd, p3)


def _out_kernel(x_ref, ya_ref, yb_ref, yc_ref, yd_ref, g0_ref, g1_ref, g2_ref, g3_ref,
                wa_ref, wb_ref, wc_ref, wd_ref, bg_ref, wo_ref, np_ref, o_ref):
    merged = None
    for i, (y_ref, w_ref, g_ref) in enumerate(((ya_ref, wa_ref, g0_ref), (yb_ref, wb_ref, g1_ref),
                                                (yc_ref, wc_ref, g2_ref), (yd_ref, wd_ref, g3_ref))):
        t = jnp.dot(y_ref[...], w_ref[...], preferred_element_type=F32)
        gate = _sigmoid(g_ref[...].astype(F32) + bg_ref[i:i + 1, :])
        merged = gate * t if merged is None else merged + gate * t
    o = jnp.dot(merged.astype(BF16), wo_ref[...], preferred_element_type=F32)
    ms = jnp.mean(o * o, axis=-1, keepdims=True)
    o_ref[...] = x_ref[...] + o * lax.rsqrt(ms + EPS) * np_ref[...]


def _merge_out(x2, p2, ya, yb, yc, yd, wa, wb, wc, wd, b_gate, w_out, norm_post, tm):
    n = x2.shape[0]
    row = lambda width: pl.BlockSpec((tm, width), lambda i: (i, 0))
    full = lambda shape: pl.BlockSpec(shape, lambda i: (0, 0))
    gate = lambda br: pl.BlockSpec((tm, D_MODEL), lambda i, br=br: (i, U_GATE * LANE // D_MODEL + br))
    return pl.pallas_call(
        _out_kernel,
        grid=(n // tm,),
        in_specs=[row(D_MODEL), row(A_WIDTH), row(B_WIDTH), row(C_WIDTH), row(D_WIDTH),
                  gate(0), gate(1), gate(2), gate(3),
                  full((A_WIDTH, D_MODEL)), full((B_WIDTH, D_MODEL)), full((C_WIDTH, D_MODEL)),
                  full((D_WIDTH, D_MODEL)), full((N_BRANCH, D_MODEL)), full((D_MODEL, D_MODEL)),
                  full((1, D_MODEL))],
        out_specs=row(D_MODEL),
        out_shape=jax.ShapeDtypeStruct((n, D_MODEL), F32),
        compiler_params=_cparams("parallel"),
        name="merge_out",
    )(x2, ya, yb, yc, yd, p2, p2, p2, p2, wa, wb, wc, wd, b_gate, w_out, norm_post)


def _permute_w_in(w_in):
    def cols(name):
        off, size = _ORIG[name]
        return w_in[..., off:off + size]

    def zeros(n):
        return jnp.zeros(w_in.shape[:-1] + (n,), w_in.dtype)

    parts = [cols("gate"), cols("d_cq"), cols("d_ckv"), cols("d_kr"), zeros(LANE - D_ROPE), cols("d_g"),
             cols("a_q"), cols("a_k"), cols("a_v"), cols("a_g"), cols("b_x"), cols("b_g"),
             cols("c_q"), cols("c_k"), cols("c_v"), cols("c_g"), zeros(LANE)]
    w = jnp.concatenate(parts, axis=-1).astype(BF16)
    assert w.shape[-1] == P_WIDTH
    return w


def kernel(x, norm_pre, norm_post, w_in, conv_w, conv_b, lru_wr, lru_br, lru_wi, lru_bi, lru_lambda,
           diff_lam_q1, diff_lam_k1, diff_lam_q2, diff_lam_k2, diff_subln, mla_q_norm, mla_kv_norm,
           mla_w_uq, mla_w_ukv, w_br_a, w_br_b, w_br_c, w_br_d, b_gate, w_out):
    b, s, d = x.shape
    depth = w_in.shape[0]
    n = b * s
    tm_in = min(2048, n)
    tm_out = min(512, n)
    tq = min(512, s)
    tk = min(512, s)

    w_perm = _permute_w_in(w_in)
    a_bias = _a_bias_tables(jnp.asarray([2.0 ** (-8.0 * (i + 1) / A_SLOTS) for i in range(A_SLOTS)], F32))
    cos_t, sin_t = _d_tables(s)
    softplus_neg_lam = jnp.log1p(jnp.exp(-lru_lambda.astype(F32)))

    x2 = x.reshape(n, d)
    for l in range(depth):
        p2 = _inproj(x2, norm_pre[l][None, :], w_perm[l], tm_in, 512)
        p3 = p2.reshape(b, s, P_WIDTH)
        ya = _mixer_a(p3, a_bias)
        bw, bb = _b_gate_weights(lru_wr[l], lru_br[l], lru_wi[l], lru_bi[l])
        yb = _mixer_b(p3, conv_w[l], conv_b[l][None, :], bw, bb, softplus_neg_lam[l])
        lam_params = jnp.stack([diff_lam_q1[l], diff_lam_k1[l], diff_lam_q2[l], diff_lam_k2[l]])
        yc = _mixer_c(p3, lam_params, diff_subln[l][None, :], l, tq, tk)
        dw = _d_weights(mla_w_uq[l], mla_w_ukv[l])
        qd, kd, vd = _dprep(p3, mla_q_norm[l][None, :], mla_kv_norm[l][None, :], dw, cos_t, sin_t, min(1024, s))
        yd = _mixer_d(p3, qd, kd, vd, tq, tk)
        x2 = _merge_out(x2, p2, ya.reshape(n, -1), yb.reshape(n, -1), yc.reshape(n, -1), yd.reshape(n, -1),
                        w_br_a[l].astype(BF16), w_br_b[l].astype(BF16), w_br_c[l].astype(BF16),
                        w_br_d[l].astype(BF16), b_gate[l], w_out[l].astype(BF16), norm_post[l][None, :], tm_out)
    return x2.reshape(b, s, d)
```

```python
import functools
import math

import jax
import jax.numpy as jnp
from jax import lax
from jax.experimental import pallas as pl
from jax.experimental.pallas import tpu as pltpu

F32 = jnp.float32
BF16 = jnp.bfloat16

D_MODEL = 1024
EPS = 1e-6
N_BRANCH = 4

A_PATTERNS = ((128, 1), (512, 4), (2048, 16))
A_SLOTS = 6
A_HEAD_DIM = 64
A_QKV = 1152
A_WIDTH = 384
A_RADIUS = 64
A_QBLK = 128
A_KWIN = 256
A_UNROLL = 4

B_WIDTH = 384
B_BLOCK_DIM = 64
B_C = 8.0

C_HEADS = 4
C_HALF = 64
C_VDIM = 128
C_QK = 512
C_WIDTH = 512

D_HEADS = 6
D_NOPE = 64
D_ROPE = 32
D_VDIM = 64
D_QLR = 256
D_KVLR = 128
D_WIDTH = 384
ROPE_BASE = 10000.0

LANE = 128
SUBLANE = 8
NEG = -1e30
LOG2E = math.log2(math.e)
_NT = (((1,), (1,)), ((), ()))
VMEM_LIMIT = 56 * 1024 * 1024

U_GATE, U_DCQ, U_DCKV, U_DKR, U_DG = 0, 32, 34, 35, 36
U_AQ, U_AK, U_AV, U_AG = 39, 48, 57, 66
U_BX, U_BG = 69, 72
U_CQ, U_CK, U_CV, U_CG = 75, 79, 83, 87
U_TOTAL = 92
P_WIDTH = U_TOTAL * LANE

_ORIG = {}
_off = 0
for _name, _size in (("a_q", A_QKV), ("a_k", A_QKV), ("a_v", A_QKV), ("a_g", A_WIDTH),
                     ("b_x", B_WIDTH), ("b_g", B_WIDTH),
                     ("c_q", C_QK), ("c_k", C_QK), ("c_v", C_WIDTH), ("c_g", C_WIDTH),
                     ("d_cq", D_QLR), ("d_ckv", D_KVLR), ("d_kr", D_ROPE), ("d_g", D_WIDTH),
                     ("gate", N_BRANCH * D_MODEL)):
    _ORIG[_name] = (_off, _size)
    _off += _size


def _cparams(*sem):
    return pltpu.CompilerParams(dimension_semantics=sem, vmem_limit_bytes=VMEM_LIMIT)


def _silu(x):
    return x * (1.0 / (1.0 + jnp.exp(-x)))


def _sigmoid(x):
    return 1.0 / (1.0 + jnp.exp(-x))


def _inproj_kernel(x_ref, g_ref, w_ref, o_ref, h_ref):
    @pl.when(pl.program_id(1) == 0)
    def _():
        x = x_ref[...]
        ms = jnp.mean(x * x, axis=-1, keepdims=True)
        h_ref[...] = (x * lax.rsqrt(ms + EPS) * g_ref[...]).astype(BF16)

    o_ref[...] = jnp.dot(h_ref[...], w_ref[...], preferred_element_type=F32).astype(o_ref.dtype)


def _inproj(x2, gain, w, tm, tn):
    n = x2.shape[0]
    return pl.pallas_call(
        _inproj_kernel,
        grid=(n // tm, P_WIDTH // tn),
        in_specs=[pl.BlockSpec((tm, D_MODEL), lambda i, j: (i, 0)),
                  pl.BlockSpec((1, D_MODEL), lambda i, j: (0, 0)),
                  pl.BlockSpec((D_MODEL, tn), lambda i, j: (0, j))],
        out_specs=pl.BlockSpec((tm, tn), lambda i, j: (i, j)),
        out_shape=jax.ShapeDtypeStruct((n, P_WIDTH), BF16),
        scratch_shapes=[pltpu.VMEM((tm, D_MODEL), BF16)],
        compiler_params=_cparams("parallel", "arbitrary"),
        name="inproj",
    )(x2, gain, w)


def _a_bias_tables(slopes):
    ii = jnp.arange(A_QBLK, dtype=jnp.int32)[:, None]
    jj = jnp.arange(A_KWIN, dtype=jnp.int32)[None, :]
    out = []
    for _, dil in A_PATTERNS:
        per_edge = []
        for off in (0, A_RADIUS, A_QBLK):
            rel = jnp.abs(off + ii - jj)
            dist = (rel * dil).astype(F32)
            b = -slopes[:, None, None] * dist[None]
            per_edge.append(jnp.where((rel <= A_RADIUS)[None], b, NEG))
        out.append(jnp.stack(per_edge))
    return jnp.stack(out)


def _a_kernel(q0, k0, v0, q1, k1, v1, q2, k2, v2, gate_ref, bias_ref, o_ref,
              qf, kf, vf, u_ref, z_ref, m_ref):
    s_len = q0.shape[0]
    lane = lax.broadcasted_iota(jnp.int32, (1, LANE), 1)
    first = lane < A_HEAD_DIM
    ones_first = jnp.where(first, 1.0, 0.0).astype(BF16)
    ones_second = jnp.where(first, 0.0, 1.0).astype(BF16)
    scale = A_HEAD_DIM ** -0.5

    for g, ((_, dil), (qr, kr, vr)) in enumerate(zip(A_PATTERNS, ((q0, k0, v0), (q1, k1, v1), (q2, k2, v2)))):
        sub_len = s_len // dil
        nqb = sub_len // A_QBLK
        if dil > 1:
            qf[...] = qr[...].astype(F32)
            kf[...] = kr[...].astype(F32)
            vf[...] = vr[...].astype(F32)

        def block(idx, carry, g=g, dil=dil, sub_len=sub_len, nqb=nqb, qr=qr, kr=kr, vr=vr):
            r = idx // nqb
            qb = idx % nqb
            qs = qb * A_QBLK
            ws = jnp.clip(qs - A_RADIUS, 0, sub_len - A_KWIN)
            edge = jnp.where(qb == 0, 0, jnp.where(qb == nqb - 1, 2, 1))
            if dil == 1:
                qrows = pl.ds(pl.multiple_of(qs, A_QBLK), A_QBLK)
                krows = pl.ds(pl.multiple_of(ws, A_RADIUS), A_KWIN)
                q = qr[qrows, :]
                k = kr[krows, :]
                v = vr[krows, :]
            else:
                qrows = pl.ds(r + qs * dil, A_QBLK, stride=dil)
                krows = pl.ds(r + ws * dil, A_KWIN, stride=dil)
                q = qf[qrows, :].astype(BF16)
                k = kf[krows, :].astype(BF16)
                v = vf[krows, :].astype(BF16)
            zq = jnp.zeros_like(q)
            zv = jnp.zeros_like(v)
            uz = None
            ms = []
            for h, head_lanes in enumerate((first, jnp.logical_not(first))):
                qh = jnp.where(head_lanes, q, zq)
                s = lax.dot_general(qh, k, (((1,), (1,)), ((), ())), preferred_element_type=F32)
                s = s * scale + bias_ref[g, edge, h]
                mh = jnp.max(s, axis=-1, keepdims=True)
                p = jnp.exp(s - mh).astype(BF16)
                ones_h = ones_first if h == 0 else ones_second
                vaug = jnp.concatenate([jnp.where(head_lanes, v, zv),
                                        jnp.broadcast_to(ones_h, v.shape)], axis=1)
                part = jnp.dot(p, vaug, preferred_element_type=F32)
                uz = part if uz is None else uz + part
                ms.append(mh)
            u = uz[:, :LANE]
            z = uz[:, LANE:]
            m = jnp.where(first, ms[0], ms[1])
            if g == 0:
                u_ref[qrows, :] = u
                z_ref[qrows, :] = z
                m_ref[qrows, :] = m
            else:
                m_old = m_ref[qrows, :]
                m_new = jnp.maximum(m_old, m)
                a = jnp.exp(m_old - m_new)
                b = jnp.exp(m - m_new)
                u_ref[qrows, :] = a * u_ref[qrows, :] + b * u
                z_ref[qrows, :] = a * z_ref[qrows, :] + b * z
                m_ref[qrows, :] = m_new
            return carry

        lax.fori_loop(0, dil * nqb, block, 0, unroll=A_UNROLL)

    gate = gate_ref[...].astype(F32)
    o_ref[...] = (u_ref[...] / z_ref[...] * _silu(gate)).astype(o_ref.dtype)


def _mixer_a(p3, bias):
    b, s, _ = p3.shape
    npair = A_SLOTS // 2
    assert s // A_PATTERNS[-1][1] >= A_KWIN

    def col(unit):
        return pl.BlockSpec((None, s, LANE), lambda bi, hp, unit=unit: (bi, 0, unit + hp))

    in_specs = []
    for g in range(len(A_PATTERNS)):
        for base in (U_AQ, U_AK, U_AV):
            in_specs.append(col(base + g * npair))
    in_specs.append(col(U_AG))
    in_specs.append(pl.BlockSpec((len(A_PATTERNS), 3, 2, A_QBLK, A_KWIN), lambda bi, hp: (0, 0, hp, 0, 0)))
    return pl.pallas_call(
        _a_kernel,
        grid=(b, npair),
        in_specs=in_specs,
        out_specs=pl.BlockSpec((None, s, LANE), lambda bi, hp: (bi, 0, hp)),
        out_shape=jax.ShapeDtypeStruct((b, s, A_WIDTH), BF16),
        scratch_shapes=[pltpu.VMEM((s, LANE), F32)] * 6,
        compiler_params=_cparams("parallel", "parallel"),
        name="mixer_a",
    )(*([p3] * 10), bias)


B_CHUNK = 256
B_PAD = 8


def _b_kernel(x_ref, g_ref, cw_ref, cb_ref, w_ref, bias_ref, sp_ref, o_ref, xs_ref, a_ref, h_ref, c_ref, hf_ref):
    s_len = x_ref.shape[0]
    nchunk = s_len // B_CHUNK
    ntile = s_len // SUBLANE
    zpad = jnp.zeros((B_PAD, LANE), F32)
    xs_ref[pl.ds(0, B_PAD), :] = zpad
    xs_ref[pl.ds(B_PAD + s_len, B_PAD), :] = zpad
    xs_ref[pl.ds(B_PAD, s_len), :] = x_ref[...].astype(F32)
    row_in_tile = jnp.bitwise_and(lax.broadcasted_iota(jnp.int32, (B_CHUNK, 1), 0), SUBLANE - 1)
    tile_idx = lax.broadcasted_iota(jnp.int32, (ntile, 1), 0)

    def gates(c, direction):
        t0 = pl.multiple_of(c * B_CHUNK, B_CHUNK)
        win = xs_ref[pl.ds(t0, B_CHUNK + 2 * B_PAD), :]
        n = B_CHUNK + 2 * B_PAD
        xc = (cb_ref[...]
              + pltpu.roll(win, 1, axis=0) * cw_ref[0:1, :]
              + win * cw_ref[1:2, :]
              + pltpu.roll(win, n - 1, axis=0) * cw_ref[2:3, :]
              + pltpu.roll(win, n - 2, axis=0) * cw_ref[3:4, :])
        xc = xc[B_PAD:B_PAD + B_CHUNK, :]
        ri = jnp.dot(xc.astype(BF16), w_ref[direction], preferred_element_type=F32)
        ri = ri + bias_ref[direction:direction + 1, :]
        r = _sigmoid(ri[:, :LANE])
        i = _sigmoid(ri[:, LANE:])
        log_a = (-B_C) * r * sp_ref[direction:direction + 1, :]
        a = jnp.exp(log_a)
        return a, jnp.sqrt(-jnp.tanh(log_a) * (a * a + 1.0)) * (i * xc)

    def scan_rows(acc_a, acc_h, n, shifts, pos, reverse):
        for sh, seg in shifts:
            if reverse:
                sa = pltpu.roll(acc_a, n - sh, axis=0)
                shh = pltpu.roll(acc_h, n - sh, axis=0)
                keep = pos < seg - sh
            else:
                sa = pltpu.roll(acc_a, sh, axis=0)
                shh = pltpu.roll(acc_h, sh, axis=0)
                keep = pos >= sh
            acc_h = acc_h + acc_a * jnp.where(keep, shh, 0.0)
            acc_a = acc_a * jnp.where(keep, sa, 1.0)
        return acc_a, acc_h

    tile_shifts = tuple((sh, SUBLANE) for sh in (1, 2, 4))
    summary_shifts = tuple((1 << e, ntile) for e in range(ntile.bit_length() - 1))
    assert 1 << len(summary_shifts) == ntile

    for direction in (0, 1):
        reverse = direction == 1

        def local_scan(c, carry, direction=direction, reverse=reverse):
            a, u = gates(c, direction)
            acc_a, acc_h = scan_rows(a, u, B_CHUNK, tile_shifts, row_in_tile, reverse)
            rows = pl.ds(pl.multiple_of(c * B_CHUNK, B_CHUNK), B_CHUNK)
            a_ref[rows, :] = acc_a
            h_ref[rows, :] = acc_h
            return carry

        lax.fori_loop(0, nchunk, local_scan, 0)

        edge = 0 if reverse else SUBLANE - 1
        sum_a = a_ref[pl.ds(edge, ntile, stride=SUBLANE), :]
        sum_h = h_ref[pl.ds(edge, ntile, stride=SUBLANE), :]
        _, inc_h = scan_rows(sum_a, sum_h, ntile, summary_shifts, tile_idx, reverse)
        if reverse:
            state_in = jnp.where(tile_idx < ntile - 1, pltpu.roll(inc_h, ntile - 1, axis=0), 0.0)
        else:
            state_in = jnp.where(tile_idx >= 1, pltpu.roll(inc_h, 1, axis=0), 0.0)
        for r in range(SUBLANE):
            c_ref[pl.ds(r, ntile, stride=SUBLANE), :] = state_in

        def apply_state(c, carry, direction=direction):
            rows = pl.ds(pl.multiple_of(c * B_CHUNK, B_CHUNK), B_CHUNK)
            h = h_ref[rows, :] + a_ref[rows, :] * c_ref[rows, :]
            if direction == 0:
                hf_ref[rows, :] = h
            else:
                gate = g_ref[rows, :].astype(F32)
                o_ref[rows, :] = ((hf_ref[rows, :] + h) * _silu(gate)).astype(o_ref.dtype)
            return carry

        lax.fori_loop(0, nchunk, apply_state, 0)


def _mixer_b(p3, conv_w, conv_b, w_gate, b_gate, softplus_neg_lam):
    b, s, _ = p3.shape
    ngrp = B_WIDTH // LANE
    assert s % B_CHUNK == 0
    return pl.pallas_call(
        _b_kernel,
        grid=(b, ngrp),
        in_specs=[pl.BlockSpec((None, s, LANE), lambda bi, j: (bi, 0, U_BX + j)),
                  pl.BlockSpec((None, s, LANE), lambda bi, j: (bi, 0, U_BG + j)),
                  pl.BlockSpec((4, LANE), lambda bi, j: (0, j)),
                  pl.BlockSpec((1, LANE), lambda bi, j: (0, j)),
                  pl.BlockSpec((None, 2, LANE, 2 * LANE), lambda bi, j: (j, 0, 0, 0)),
                  pl.BlockSpec((None, 2, 2 * LANE), lambda bi, j: (j, 0, 0)),
                  pl.BlockSpec((2, LANE), lambda bi, j: (0, j))],
        out_specs=pl.BlockSpec((None, s, LANE), lambda bi, j: (bi, 0, j)),
        out_shape=jax.ShapeDtypeStruct((b, s, B_WIDTH), BF16),
        scratch_shapes=[pltpu.VMEM((s + 2 * B_PAD, LANE), F32)] + [pltpu.VMEM((s, LANE), F32)] * 4,
        compiler_params=_cparams("parallel", "parallel"),
        name="mixer_b",
    )(p3, p3, conv_w, conv_b, w_gate, b_gate, softplus_neg_lam)


def _b_gate_weights(w_r, b_r, w_i, b_i):
    per = LANE // B_BLOCK_DIM
    ngrp = B_WIDTH // LANE

    def blockdiag(w):
        w = w.reshape(2, ngrp, per, B_BLOCK_DIM, B_BLOCK_DIM)
        eye = jnp.eye(per, dtype=w.dtype)
        full = jnp.einsum("dgpcx,pq->dgpcqx", w, eye).reshape(2, ngrp, LANE, LANE)
        return full.transpose(1, 0, 2, 3)

    w = jnp.concatenate([blockdiag(w_r), blockdiag(w_i)], axis=-1).astype(BF16)
    bias = jnp.concatenate([b_r.reshape(2, ngrp, LANE), b_i.reshape(2, ngrp, LANE)], axis=-1)
    return w, bias.transpose(1, 0, 2)


def _attend_t(streams, tk, nk, s_ref, m_ref, acc_ref, tile_of=None):
    assert nk % 2 == 0 and nk >= 2
    if tile_of is None:
        tile_of = lambda u: u
    for i in range(len(streams)):
        m_ref[i] = jnp.full(m_ref.shape[1:], NEG, F32)
        acc_ref[i] = jnp.zeros(acc_ref.shape[1:], F32)

    def key_rows(u):
        return pl.ds(pl.multiple_of(tile_of(u) * tk, tk), tk)

    def scores(u, slot):
        rows = key_rows(u)
        for i, st in enumerate(streams):
            s_ref[slot, i] = jnp.dot(st["k_ref"][rows, :], st["q"](u), preferred_element_type=F32)

    def accumulate(u, slot, first=False):
        rows = key_rows(u)
        for i, st in enumerate(streams):
            s = s_ref[slot, i]
            if first and st.get("bias0") is not None:
                s = s + st["bias0"][...]
            off = st["off"](u) if st.get("off") is not None else None
            m_old = m_ref[i]
            m_tile = jnp.max(s, axis=0, keepdims=True)
            m_new = jnp.maximum(m_old, m_tile if off is None else m_tile + off)
            alpha = jnp.exp2(m_old - m_new)
            p = jnp.exp2(s - (m_new if off is None else m_new - off)).astype(BF16)
            acc_ref[i] = alpha * acc_ref[i] + jnp.dot(st["vt_ref"][:, rows], p, preferred_element_type=F32)
            m_ref[i] = m_new

    scores(0, 0)
    scores(1, 1)
    accumulate(0, 0, first=True)

    def body(j, carry):
        u = 1 + 2 * j
        scores(u + 1, 0)
        accumulate(u, 1)
        scores(u + 2, 1)
        accumulate(u + 1, 0)
        return carry

    lax.fori_loop(0, (nk - 2) // 2, body, 0)
    accumulate(nk - 1, 1)


C_AUG = 3
C_VROWS = C_VDIM + 16
C_MAPS = ((0, C_HALF), (1, 0))


def _c_kernel(q_ref, k_ref, v_ref, g_ref, lam_ref, subln_ref, o_ref,
              kaug_ref, vt_ref, dbias_ref, s_ref, m_ref, acc_ref, *, tk, slopes, lam_init):
    h = pl.program_id(1)
    qi = pl.program_id(2)
    tq = q_ref.shape[0]
    s_len = k_ref.shape[0]
    nk = s_len // tk
    assert tq == tk
    lp = lam_ref[...]
    lam = (jnp.exp(jnp.sum(lp[0:1, :] * lp[1:2, :], axis=-1, keepdims=True))
           - jnp.exp(jnp.sum(lp[2:3, :] * lp[3:4, :], axis=-1, keepdims=True)) + lam_init)
    slope = jnp.where(h == 0, slopes[0], jnp.where(h == 1, slopes[1], jnp.where(h == 2, slopes[2], slopes[3])))
    c = slope.astype(F32) * LOG2E

    @pl.when(qi == 0)
    def _build_key_side():
        lane = lax.broadcasted_iota(jnp.int32, (1, LANE), 1)
        for ch in range(nk):
            rows = slice(ch * tk, (ch + 1) * tk)
            k = k_ref[rows, :]
            jpos = (ch * tk + lax.broadcasted_iota(jnp.int32, (tk, 1), 0)).astype(F32)
            cj = c * jpos
            hi = cj.astype(BF16).astype(F32)
            mid = (cj - hi).astype(BF16).astype(F32)
            lo = (cj - hi - mid).astype(BF16).astype(F32)
            for cmap, base in C_MAPS:
                aug = jnp.where(lane == base, hi, jnp.where(lane == base + 1, mid,
                                                            jnp.where(lane == base + 2, lo, 0.0)))
                keep = (lane < C_HALF) if cmap == 0 else (lane >= C_HALF)
                kaug_ref[cmap, rows, :] = jnp.where(keep, k, aug.astype(BF16))
            vt_ref[0:C_VDIM, rows] = v_ref[rows, :].astype(F32).T.astype(BF16)
        r16 = lax.broadcasted_iota(jnp.int32, (C_VROWS - C_VDIM, s_len), 0)
        vt_ref[C_VDIM:C_VROWS, :] = jnp.where(r16 == 0, 1.0, 0.0).astype(BF16)
        ii = lax.broadcasted_iota(jnp.int32, (tk, tq), 1)
        jj = lax.broadcasted_iota(jnp.int32, (tk, tq), 0)
        dbias_ref[...] = -c * jnp.abs(ii - jj).astype(F32)

    def tile_of(u):
        if isinstance(u, int) and u == 0:
            return qi
        return u - 1 + (u - 1 >= qi).astype(jnp.int32)

    def sign_of(u):
        return jnp.where(tile_of(u) < qi, 1.0, -1.0).astype(F32)

    q_t = (q_ref[...].astype(F32) * (C_HALF ** -0.5 * LOG2E)).T
    row = lax.broadcasted_iota(jnp.int32, (LANE, 1), 0)
    ci = c * (qi * tq + lax.broadcasted_iota(jnp.int32, (1, tq), 1)).astype(F32)
    streams = []
    for cmap, base in C_MAPS:
        keep = (row < C_HALF) if cmap == 0 else (row >= C_HALF)
        q_base = jnp.where(keep, q_t, 0.0)
        aug_rows = jnp.where((row >= base) & (row < base + C_AUG), 1.0, 0.0)

        def q_of(u, q_base=q_base, aug_rows=aug_rows):
            if isinstance(u, int) and u == 0:
                return q_base.astype(BF16)
            return (q_base + sign_of(u) * aug_rows).astype(BF16)

        def off_of(u):
            if isinstance(u, int) and u == 0:
                return None
            return -sign_of(u) * ci

        streams.append(dict(q=q_of, k_ref=kaug_ref.at[cmap], vt_ref=vt_ref, off=off_of, bias0=dbias_ref))
    _attend_t(streams, tk, nk, s_ref, m_ref, acc_ref, tile_of)

    o_t = []
    for cmap in range(2):
        acc = acc_ref[cmap]
        o_t.append(acc[:C_VDIM, :] / acc[C_VDIM:C_VDIM + 1, :])
    o = (o_t[0] - lam * o_t[1]).T
    ms = jnp.mean(o * o, axis=-1, keepdims=True)
    o = o * lax.rsqrt(ms + EPS) * subln_ref[...] * (1.0 - lam_init)
    o_ref[...] = (o * _silu(g_ref[...].astype(F32))).astype(o_ref.dtype)


def _mixer_c(p3, lam_params, subln, layer, tq, tk):
    b, s, _ = p3.shape
    lam_init = 0.8 - 0.6 * math.exp(-0.3 * layer)
    slopes = tuple(2.0 ** (-8.0 * (i + 1) / C_HEADS) for i in range(C_HEADS))
    kern = functools.partial(_c_kernel, tk=tk, slopes=slopes, lam_init=lam_init)
    return pl.pallas_call(
        kern,
        grid=(b, C_HEADS, s // tq),
        in_specs=[pl.BlockSpec((None, tq, LANE), lambda bi, h, i: (bi, i, U_CQ + h)),
                  pl.BlockSpec((None, s, LANE), lambda bi, h, i: (bi, 0, U_CK + h)),
                  pl.BlockSpec((None, s, LANE), lambda bi, h, i: (bi, 0, U_CV + h)),
                  pl.BlockSpec((None, tq, LANE), lambda bi, h, i: (bi, i, U_CG + h)),
                  pl.BlockSpec((4, C_HALF), lambda bi, h, i: (0, 0)),
                  pl.BlockSpec((1, C_VDIM), lambda bi, h, i: (0, 0))],
        out_specs=pl.BlockSpec((None, tq, LANE), lambda bi, h, i: (bi, i, h)),
        out_shape=jax.ShapeDtypeStruct((b, s, C_WIDTH), BF16),
        scratch_shapes=[pltpu.VMEM((2, s, LANE), BF16), pltpu.VMEM((C_VROWS, s), BF16),
                        pltpu.VMEM((tk, tq), F32), pltpu.VMEM((2, 2, tk, tq), F32),
                        pltpu.VMEM((2, 1, tq), F32), pltpu.VMEM((2, C_VROWS, tq), F32)],
        compiler_params=_cparams("parallel", "parallel", "arbitrary"),
        name="mixer_c",
    )(p3, p3, p3, p3, lam_params, subln)


def _dprep_kernel(cq_ref, ckv_ref, kr_ref, qn_ref, kvn_ref, wqt_ref, wqst_ref, wk_ref, wvt_ref,
                  e1_ref, e2_ref, cos_ref, sin_ref, cost_ref, sint_ref, vonet_ref, qt_out, k_out, vt_out):
    def norm(x_ref, gain_ref):
        x = x_ref[...].astype(F32)
        ms = jnp.mean(x * x, axis=-1, keepdims=True)
        return (x * lax.rsqrt(ms + EPS) * gain_ref[...]).astype(BF16)

    cqn = norm(cq_ref, qn_ref)
    ckvn = norm(ckv_ref, kvn_ref)
    kr = kr_ref[...]
    cos = jnp.concatenate([cos_ref[...]] * D_HEADS, axis=1)
    sin = jnp.concatenate([sin_ref[...]] * D_HEADS, axis=1)
    cos_t = jnp.concatenate([cost_ref[...]] * D_HEADS, axis=0)
    sin_t = jnp.concatenate([sint_ref[...]] * D_HEADS, axis=0)
    q_t = (lax.dot_general(wqt_ref[...], cqn, _NT, preferred_element_type=F32) * cos_t
           + lax.dot_general(wqst_ref[...], cqn, _NT, preferred_element_type=F32) * sin_t)
    qt_out[...] = (q_t * ((D_NOPE + D_ROPE) ** -0.5 * LOG2E)).astype(qt_out.dtype)
    k = (jnp.dot(ckvn, wk_ref[...], preferred_element_type=F32)
         + jnp.dot(kr, e1_ref[...], preferred_element_type=F32) * cos
         + jnp.dot(kr, e2_ref[...], preferred_element_type=F32) * sin)
    k_out[...] = k.astype(k_out.dtype)
    v_t = lax.dot_general(wvt_ref[...], ckvn, _NT, preferred_element_type=F32) + vonet_ref[...]
    vt_out[...] = v_t.astype(vt_out.dtype)


def _d_tables(s):
    inv = ROPE_BASE ** (-jnp.arange(0, D_ROPE, 2, dtype=F32) / D_ROPE)
    ang = jnp.arange(s, dtype=F32)[:, None] * inv[None, :]
    cos, sin = jnp.cos(ang), jnp.sin(ang)
    pad = LANE - D_NOPE - D_ROPE
    cos_t = jnp.concatenate([jnp.ones((s, D_NOPE), F32), cos, cos, jnp.zeros((s, pad), F32)], axis=1)
    sin_t = jnp.concatenate([jnp.zeros((s, D_NOPE), F32), -sin, sin, jnp.zeros((s, pad), F32)], axis=1)
    return cos_t, sin_t


def _d_weights(w_uq, w_ukv):
    half = D_ROPE // 2
    pad = LANE - D_NOPE - D_ROPE
    wq = w_uq.reshape(D_QLR, D_HEADS, D_NOPE + D_ROPE)
    zq = jnp.zeros((D_QLR, D_HEADS, pad), w_uq.dtype)
    wq_main = jnp.concatenate([wq, zq], axis=-1).reshape(D_QLR, D_HEADS * LANE)
    wq_swap = jnp.concatenate([jnp.zeros((D_QLR, D_HEADS, D_NOPE), w_uq.dtype),
                               wq[..., D_NOPE + half:], wq[..., D_NOPE:D_NOPE + half], zq],
                              axis=-1).reshape(D_QLR, D_HEADS * LANE)
    wkv = w_ukv.reshape(D_KVLR, D_HEADS, D_NOPE + D_VDIM)
    zk = jnp.zeros((D_KVLR, D_HEADS, LANE - D_NOPE), w_ukv.dtype)
    wk = jnp.concatenate([wkv[..., :D_NOPE], zk], axis=-1).reshape(D_KVLR, D_HEADS * LANE)
    wv = jnp.concatenate([wkv[..., D_NOPE:], jnp.zeros((D_KVLR, D_HEADS, LANE - D_VDIM), w_ukv.dtype)],
                         axis=-1).reshape(D_KVLR, D_HEADS * LANE)
    src = jnp.arange(D_ROPE)
    e1 = jnp.zeros((LANE, LANE), F32).at[src, D_NOPE + src].set(1.0)
    e2 = jnp.zeros((LANE, LANE), F32).at[(src + half) % D_ROPE, D_NOPE + src].set(1.0)
    e1 = jnp.tile(e1, (1, D_HEADS))
    e2 = jnp.tile(e2, (1, D_HEADS))
    vone = jnp.zeros((LANE, 1), F32).at[D_VDIM, 0].set(1.0)
    vone = jnp.tile(vone, (D_HEADS, 1))
    return (wq_main.T.astype(BF16), wq_swap.T.astype(BF16), wk.astype(BF16), wv.T.astype(BF16),
            e1.astype(BF16), e2.astype(BF16), vone)


def _dprep(p3, q_norm, kv_norm, dw, cos_t, sin_t, tm):
    b, s, _ = p3.shape
    wqt, wqst, wk, wvt, e1, e2, vonet = dw
    wide = D_HEADS * LANE
    full = lambda shape: pl.BlockSpec(shape, lambda bi, i: (0,) * len(shape))
    tok_major = pl.BlockSpec((None, tm, wide), lambda bi, i: (bi, i, 0))
    feat_major = pl.BlockSpec((None, wide, tm), lambda bi, i: (bi, 0, i))
    return pl.pallas_call(
        _dprep_kernel,
        grid=(b, s // tm),
        in_specs=[pl.BlockSpec((None, tm, D_QLR), lambda bi, i: (bi, i, U_DCQ // 2)),
                  pl.BlockSpec((None, tm, LANE), lambda bi, i: (bi, i, U_DCKV)),
                  pl.BlockSpec((None, tm, LANE), lambda bi, i: (bi, i, U_DKR)),
                  full((1, D_QLR)), full((1, D_KVLR)),
                  full((wide, D_QLR)), full((wide, D_QLR)), full((D_KVLR, wide)), full((wide, D_KVLR)),
                  full((LANE, wide)), full((LANE, wide)),
                  pl.BlockSpec((tm, LANE), lambda bi, i: (i, 0)),
                  pl.BlockSpec((tm, LANE), lambda bi, i: (i, 0)),
                  pl.BlockSpec((LANE, tm), lambda bi, i: (0, i)),
                  pl.BlockSpec((LANE, tm), lambda bi, i: (0, i)),
                  full((wide, 1))],
        out_specs=[feat_major, tok_major, feat_major],
        out_shape=[jax.ShapeDtypeStruct((b, wide, s), BF16), jax.ShapeDtypeStruct((b, s, wide), BF16),
                   jax.ShapeDtypeStruct((b, wide, s), BF16)],
        compiler_params=_cparams("parallel", "parallel"),
        name="mixer_d_prep",
    )(p3, p3, p3, q_norm, kv_norm, wqt, wqst, wk, wvt, e1, e2, cos_t, sin_t, cos_t.T, sin_t.T, vonet)


def _d_kernel(qt_ref, k_ref, vt_ref, g_ref, o_ref, s_ref, m_ref, acc_ref, *, tk):
    streams = []
    for h in range(2):
        sl = slice(h * LANE, (h + 1) * LANE)
        streams.append(dict(q=lambda u, q=qt_ref[sl, :]: q, k_ref=k_ref.at[:, sl], vt_ref=vt_ref.at[sl, :]))
    _attend_t(streams, tk, k_ref.shape[0] // tk, s_ref, m_ref, acc_ref)
    parts = []
    for h in range(2):
        acc = acc_ref[h]
        parts.append(acc[:D_VDIM, :] / acc[D_VDIM:D_VDIM + 1, :])
    o = jnp.concatenate(parts, axis=0).T
    o_ref[...] = (o * _silu(g_ref[...].astype(F32))).astype(o_ref.dtype)


def _mixer_d(p3, qt, kd, vt, tq, tk):
    b, s, _ = p3.shape
    npair = D_HEADS // 2
    return pl.pallas_call(
        functools.partial(_d_kernel, tk=tk),
        grid=(b, npair, s // tq),
        in_specs=[pl.BlockSpec((None, 2 * LANE, tq), lambda bi, hp, i: (bi, hp, i)),
                  pl.BlockSpec((None, s, 2 * LANE), lambda bi, hp, i: (bi, 0, hp)),
                  pl.BlockSpec((None, 2 * LANE, s), lambda bi, hp, i: (bi, hp, 0)),
                  pl.BlockSpec((None, tq, LANE), lambda bi, hp, i: (bi, i, U_DG + hp))],
        out_specs=pl.BlockSpec((None, tq, LANE), lambda bi, hp, i: (bi, i, hp)),
        out_shape=jax.ShapeDtypeStruct((b, s, D_WIDTH), BF16),
        scratch_shapes=[pltpu.VMEM((2, 2, tk, tq), F32), pltpu.VMEM((2, 1, tq), F32),
                        pltpu.VMEM((2, LANE, tq), F32)],
        compiler_params=_cparams("parallel", "parallel", "arbitrary"),
        name="mixer_d",
    )(qt, kd, vt, p3)


def _out_kernel(x_ref, ya_ref, yb_ref, yc_ref, yd_ref, g0_ref, g1_ref, g2_ref, g3_ref,
                wa_ref, wb_ref, wc_ref, wd_ref, bg_ref, wo_ref, np_ref, o_ref):
    merged = None
    for i, (y_ref, w_ref, g_ref) in enumerate(((ya_ref, wa_ref, g0_ref), (yb_ref, wb_ref, g1_ref),
                                                (yc_ref, wc_ref, g2_ref), (yd_ref, wd_ref, g3_ref))):
        t = jnp.dot(y_ref[...], w_ref[...], preferred_element_type=F32)
        gate = _sigmoid(g_ref[...].astype(F32) + bg_ref[i:i + 1, :])
        merged = gate * t if merged is None else merged + gate * t
    o = jnp.dot(merged.astype(BF16), wo_ref[...], preferred_element_type=F32)
    ms = jnp.mean(o * o, axis=-1, keepdims=True)
    o_ref[...] = x_ref[...] + o * lax.rsqrt(ms + EPS) * np_ref[...]


def _merge_out(x2, p2, ya, yb, yc, yd, wa, wb, wc, wd, b_gate, w_out, norm_post, tm):
    n = x2.shape[0]
    row = lambda width: pl.BlockSpec((tm, width), lambda i: (i, 0))
    full = lambda shape: pl.BlockSpec(shape, lambda i: (0, 0))
    gate = lambda br: pl.BlockSpec((tm, D_MODEL), lambda i, br=br: (i, U_GATE * LANE // D_MODEL + br))
    return pl.pallas_call(
        _out_kernel,
        grid=(n // tm,),
        in_specs=[row(D_MODEL), row(A_WIDTH), row(B_WIDTH), row(C_WIDTH), row(D_WIDTH),
                  gate(0), gate(1), gate(2), gate(3),
                  full((A_WIDTH, D_MODEL)), full((B_WIDTH, D_MODEL)), full((C_WIDTH, D_MODEL)),
                  full((D_WIDTH, D_MODEL)), full((N_BRANCH, D_MODEL)), full((D_MODEL, D_MODEL)),
                  full((1, D_MODEL))],
        out_specs=row(D_MODEL),
        out_shape=jax.ShapeDtypeStruct((n, D_MODEL), F32),
        compiler_params=_cparams("parallel"),
        name="merge_out",
    )(x2, ya, yb, yc, yd, p2, p2, p2, p2, wa, wb, wc, wd, b_gate, w_out, norm_post)


def _permute_w_in(w_in):
    def cols(name):
        off, size = _ORIG[name]
        return w_in[..., off:off + size]

    def zeros(n):
        return jnp.zeros(w_in.shape[:-1] + (n,), w_in.dtype)

    parts = [cols("gate"), cols("d_cq"), cols("d_ckv"), cols("d_kr"), zeros(LANE - D_ROPE), cols("d_g"),
             cols("a_q"), cols("a_k"), cols("a_v"), cols("a_g"), cols("b_x"), cols("b_g"),
             cols("c_q"), cols("c_k"), cols("c_v"), cols("c_g"), zeros(LANE)]
    w = jnp.concatenate(parts, axis=-1).astype(BF16)
    assert w.shape[-1] == P_WIDTH
    return w


def kernel(x, norm_pre, norm_post, w_in, conv_w, conv_b, lru_wr, lru_br, lru_wi, lru_bi, lru_lambda,
           diff_lam_q1, diff_lam_k1, diff_lam_q2, diff_lam_k2, diff_subln, mla_q_norm, mla_kv_norm,
           mla_w_uq, mla_w_ukv, w_br_a, w_br_b, w_br_c, w_br_d, b_gate, w_out):
    b, s, d = x.shape
    depth = w_in.shape[0]
    n = b * s
    tm_in = min(2048, n)
    tm_out = min(512, n)
    tq = min(512, s)
    tk = min(512, s)

    w_perm = _permute_w_in(w_in)
    a_bias = _a_bias_tables(jnp.asarray([2.0 ** (-8.0 * (i + 1) / A_SLOTS) for i in range(A_SLOTS)], F32))
    cos_t, sin_t = _d_tables(s)
    softplus_neg_lam = jnp.log1p(jnp.exp(-lru_lambda.astype(F32)))

    x2 = x.reshape(n, d)
    for l in range(depth):
        p2 = _inproj(x2, norm_pre[l][None, :], w_perm[l], tm_in, 512)
        p3 = p2.reshape(b, s, P_WIDTH)
        ya = _mixer_a(p3, a_bias)
        bw, bb = _b_gate_weights(lru_wr[l], lru_br[l], lru_wi[l], lru_bi[l])
        yb = _mixer_b(p3, conv_w[l], conv_b[l][None, :], bw, bb, softplus_neg_lam[l])
        lam_params = jnp.stack([diff_lam_q1[l], diff_lam_k1[l], diff_lam_q2[l], diff_lam_k2[l]])
        yc = _mixer_c(p3, lam_params, diff_subln[l][None, :], l, tq, tk)
        dw = _d_weights(mla_w_uq[l], mla_w_ukv[l])
        qd, kd, vd = _dprep(p3, mla_q_norm[l][None, :], mla_kv_norm[l][None, :], dw, cos_t, sin_t, min(1024, s))
        yd = _mixer_d(p3, qd, kd, vd, tq, tk)
        x2 = _merge_out(x2, p2, ya.reshape(n, -1), yb.reshape(n, -1), yc.reshape(n, -1), yd.reshape(n, -1),
                        w_br_a[l].astype(BF16), w_br_b[l].astype(BF16), w_br_c[l].astype(BF16),
                        w_br_d[l].astype(BF16), b_gate[l], w_out[l].astype(BF16), norm_post[l][None, :], tm_out)
    return x2.reshape(b, s, d)
```

```python
import functools
import math

import jax
import jax.numpy as jnp
from jax import lax
from jax.experimental import pallas as pl
from jax.experimental.pallas import tpu as pltpu

F32 = jnp.float32
BF16 = jnp.bfloat16

D_MODEL = 1024
EPS = 1e-6
N_BRANCH = 4

A_PATTERNS = ((128, 1), (512, 4), (2048, 16))
A_SLOTS = 6
A_HEAD_DIM = 64
A_QKV = 1152
A_WIDTH = 384
A_RADIUS = 64
A_QBLK = 128
A_KWIN = 256
A_UNROLL = 4

B_WIDTH = 384
B_BLOCK_DIM = 64
B_C = 8.0

C_HEADS = 4
C_HALF = 64
C_VDIM = 128
C_QK = 512
C_WIDTH = 512

D_HEADS = 6
D_NOPE = 64
D_ROPE = 32
D_VDIM = 64
D_QLR = 256
D_KVLR = 128
D_WIDTH = 384
D_VROWS = D_VDIM + 16
ROPE_BASE = 10000.0

LANE = 128
SUBLANE = 8
NEG = -1e30
LOG2E = math.log2(math.e)
_NT = (((1,), (1,)), ((), ()))
VMEM_LIMIT = 56 * 1024 * 1024

U_GATE, U_DCQ, U_DCKV, U_DKR, U_DG = 0, 32, 34, 35, 36
U_AQ, U_AK, U_AV, U_AG = 39, 48, 57, 66
U_BX, U_BG = 69, 72
U_CQ, U_CK, U_CV, U_CG = 75, 79, 83, 87
U_TOTAL = 92
P_WIDTH = U_TOTAL * LANE

_ORIG = {}
_off = 0
for _name, _size in (("a_q", A_QKV), ("a_k", A_QKV), ("a_v", A_QKV), ("a_g", A_WIDTH),
                     ("b_x", B_WIDTH), ("b_g", B_WIDTH),
                     ("c_q", C_QK), ("c_k", C_QK), ("c_v", C_WIDTH), ("c_g", C_WIDTH),
                     ("d_cq", D_QLR), ("d_ckv", D_KVLR), ("d_kr", D_ROPE), ("d_g", D_WIDTH),
                     ("gate", N_BRANCH * D_MODEL)):
    _ORIG[_name] = (_off, _size)
    _off += _size


def _cparams(*sem):
    return pltpu.CompilerParams(dimension_semantics=sem, vmem_limit_bytes=VMEM_LIMIT)


def _silu(x):
    return x * (1.0 / (1.0 + jnp.exp(-x)))


def _sigmoid(x):
    return 1.0 / (1.0 + jnp.exp(-x))


def _inproj_kernel(x_ref, g_ref, w_ref, o_ref, h_ref):
    @pl.when(pl.program_id(1) == 0)
    def _():
        x = x_ref[...]
        ms = jnp.mean(x * x, axis=-1, keepdims=True)
        h_ref[...] = (x * lax.rsqrt(ms + EPS) * g_ref[...]).astype(BF16)

    o_ref[...] = jnp.dot(h_ref[...], w_ref[...], preferred_element_type=F32).astype(o_ref.dtype)


def _inproj(x2, gain, w_all, layer, tm, tn):
    n = x2.shape[0]
    return pl.pallas_call(
        _inproj_kernel,
        grid=(n // tm, P_WIDTH // tn),
        in_specs=[pl.BlockSpec((tm, D_MODEL), lambda i, j: (i, 0)),
                  pl.BlockSpec((1, D_MODEL), lambda i, j: (0, 0)),
                  pl.BlockSpec((None, D_MODEL, tn), lambda i, j: (layer, 0, j))],
        out_specs=pl.BlockSpec((tm, tn), lambda i, j: (i, j)),
        out_shape=jax.ShapeDtypeStruct((n, P_WIDTH), BF16),
        scratch_shapes=[pltpu.VMEM((tm, D_MODEL), BF16)],
        compiler_params=_cparams("parallel", "arbitrary"),
        name="inproj",
    )(x2, gain, w_all)


def _a_bias_tables(slopes):
    ii = jnp.arange(A_QBLK, dtype=jnp.int32)[:, None]
    jj = jnp.arange(A_KWIN, dtype=jnp.int32)[None, :]
    out = []
    for _, dil in A_PATTERNS:
        per_edge = []
        for off in (0, A_RADIUS, A_QBLK):
            rel = jnp.abs(off + ii - jj)
            dist = (rel * dil).astype(F32)
            b = -slopes[:, None, None] * dist[None] * LOG2E
            per_edge.append(jnp.where((rel <= A_RADIUS)[None], b, NEG))
        out.append(jnp.stack(per_edge))
    return jnp.stack(out)


def _a_kernel(q0, k0, v0, q1, k1, v1, q2, k2, v2, gate_ref, bias_ref, o_ref,
              qf, kf, vf, u_ref, z_ref, m_ref):
    s_len = q0.shape[0]
    lane = lax.broadcasted_iota(jnp.int32, (1, LANE), 1)
    first = lane < A_HEAD_DIM
    ones_first = jnp.where(first, 1.0, 0.0).astype(BF16)
    ones_second = jnp.where(first, 0.0, 1.0).astype(BF16)
    scale = A_HEAD_DIM ** -0.5 * LOG2E

    for g, ((_, dil), (qr, kr, vr)) in enumerate(zip(A_PATTERNS, ((q0, k0, v0), (q1, k1, v1), (q2, k2, v2)))):
        sub_len = s_len // dil
        nqb = sub_len // A_QBLK
        if dil > 1:
            qf[...] = qr[...].astype(F32) * scale
            kf[...] = kr[...].astype(F32)
            vf[...] = vr[...].astype(F32)

        def block(idx, carry, g=g, dil=dil, sub_len=sub_len, nqb=nqb, qr=qr, kr=kr, vr=vr):
            r = idx // nqb
            qb = idx % nqb
            qs = qb * A_QBLK
            ws = jnp.clip(qs - A_RADIUS, 0, sub_len - A_KWIN)
            edge = jnp.where(qb == 0, 0, jnp.where(qb == nqb - 1, 2, 1))
            if dil == 1:
                qrows = pl.ds(pl.multiple_of(qs, A_QBLK), A_QBLK)
                krows = pl.ds(pl.multiple_of(ws, A_RADIUS), A_KWIN)
                q = (qr[qrows, :].astype(F32) * scale).astype(BF16)
                k = kr[krows, :]
                v = vr[krows, :]
            else:
                qrows = pl.ds(r + qs * dil, A_QBLK, stride=dil)
                krows = pl.ds(r + ws * dil, A_KWIN, stride=dil)
                q = qf[qrows, :].astype(BF16)
                k = kf[krows, :].astype(BF16)
                v = vf[krows, :].astype(BF16)
            zq = jnp.zeros_like(q)
            zv = jnp.zeros_like(v)
            uz = None
            ms = []
            for h, head_lanes in enumerate((first, jnp.logical_not(first))):
                qh = jnp.where(head_lanes, q, zq)
                s = lax.dot_general(qh, k, (((1,), (1,)), ((), ())), preferred_element_type=F32)
                s = s + bias_ref[g, edge, h]
                mh = jnp.max(s, axis=-1, keepdims=True)
                p = jnp.exp2(s - mh).astype(BF16)
                ones_h = ones_first if h == 0 else ones_second
                vaug = jnp.concatenate([jnp.where(head_lanes, v, zv),
                                        jnp.broadcast_to(ones_h, v.shape)], axis=1)
                part = jnp.dot(p, vaug, preferred_element_type=F32)
                uz = part if uz is None else uz + part
                ms.append(mh)
            u = uz[:, :LANE]
            z = uz[:, LANE:]
            m = jnp.where(first, ms[0], ms[1])
            if g == 0:
                u_ref[qrows, :] = u
                z_ref[qrows, :] = z
                m_ref[qrows, :] = m
            else:
                m_old = m_ref[qrows, :]
                m_new = jnp.maximum(m_old, m)
                a = jnp.exp2(m_old - m_new)
                b = jnp.exp2(m - m_new)
                u_ref[qrows, :] = a * u_ref[qrows, :] + b * u
                z_ref[qrows, :] = a * z_ref[qrows, :] + b * z
                m_ref[qrows, :] = m_new
            return carry

        lax.fori_loop(0, dil * nqb, block, 0, unroll=A_UNROLL)

    gate = gate_ref[...].astype(F32)
    o_ref[...] = (u_ref[...] / z_ref[...] * _silu(gate)).astype(o_ref.dtype)


def _mixer_a(p3, bias):
    b, s, _ = p3.shape
    npair = A_SLOTS // 2
    assert s // A_PATTERNS[-1][1] >= A_KWIN

    def col(unit):
        return pl.BlockSpec((None, s, LANE), lambda bi, hp, unit=unit: (bi, 0, unit + hp))

    in_specs = []
    for g in range(len(A_PATTERNS)):
        for base in (U_AQ, U_AK, U_AV):
            in_specs.append(col(base + g * npair))
    in_specs.append(col(U_AG))
    in_specs.append(pl.BlockSpec((len(A_PATTERNS), 3, 2, A_QBLK, A_KWIN), lambda bi, hp: (0, 0, hp, 0, 0)))
    return pl.pallas_call(
        _a_kernel,
        grid=(b, npair),
        in_specs=in_specs,
        out_specs=pl.BlockSpec((None, s, LANE), lambda bi, hp: (bi, 0, hp)),
        out_shape=jax.ShapeDtypeStruct((b, s, A_WIDTH), BF16),
        scratch_shapes=[pltpu.VMEM((s, LANE), F32)] * 6,
        compiler_params=_cparams("parallel", "parallel"),
        name="mixer_a",
    )(*([p3] * 10), bias)


B_CHUNK = 256
B_PAD = 8


def _b_kernel(x_ref, g_ref, cw_ref, cb_ref, w_ref, bias_ref, sp_ref, o_ref, xs_ref, a_ref, h_ref, c_ref, hf_ref):
    s_len = x_ref.shape[0]
    nchunk = s_len // B_CHUNK
    ntile = s_len // SUBLANE
    zpad = jnp.zeros((B_PAD, LANE), F32)
    xs_ref[pl.ds(0, B_PAD), :] = zpad
    xs_ref[pl.ds(B_PAD + s_len, B_PAD), :] = zpad
    xs_ref[pl.ds(B_PAD, s_len), :] = x_ref[...].astype(F32)
    row_in_tile = jnp.bitwise_and(lax.broadcasted_iota(jnp.int32, (B_CHUNK, 1), 0), SUBLANE - 1)
    tile_idx = lax.broadcasted_iota(jnp.int32, (ntile, 1), 0)

    def gates(c, direction):
        t0 = pl.multiple_of(c * B_CHUNK, B_CHUNK)
        win = xs_ref[pl.ds(t0, B_CHUNK + 2 * B_PAD), :]
        n = B_CHUNK + 2 * B_PAD
        xc = (cb_ref[...]
              + pltpu.roll(win, 1, axis=0) * cw_ref[0:1, :]
              + win * cw_ref[1:2, :]
              + pltpu.roll(win, n - 1, axis=0) * cw_ref[2:3, :]
              + pltpu.roll(win, n - 2, axis=0) * cw_ref[3:4, :])
        xc = xc[B_PAD:B_PAD + B_CHUNK, :]
        ri = jnp.dot(xc.astype(BF16), w_ref[direction], preferred_element_type=F32)
        ri = ri + bias_ref[direction:direction + 1, :]
        r = _sigmoid(ri[:, :LANE])
        i = _sigmoid(ri[:, LANE:])
        log_a = (-B_C) * r * sp_ref[direction:direction + 1, :]
        a = jnp.exp(log_a)
        return a, jnp.sqrt(-jnp.tanh(log_a) * (a * a + 1.0)) * (i * xc)

    def scan_rows(acc_a, acc_h, n, shifts, pos, reverse):
        for sh, seg in shifts:
            if reverse:
                sa = pltpu.roll(acc_a, n - sh, axis=0)
                shh = pltpu.roll(acc_h, n - sh, axis=0)
                keep = pos < seg - sh
            else:
                sa = pltpu.roll(acc_a, sh, axis=0)
                shh = pltpu.roll(acc_h, sh, axis=0)
                keep = pos >= sh
            acc_h = acc_h + acc_a * jnp.where(keep, shh, 0.0)
            acc_a = acc_a * jnp.where(keep, sa, 1.0)
        return acc_a, acc_h

    tile_shifts = tuple((sh, SUBLANE) for sh in (1, 2, 4))
    summary_shifts = tuple((1 << e, ntile) for e in range(ntile.bit_length() - 1))
    assert 1 << len(summary_shifts) == ntile

    for direction in (0, 1):
        reverse = direction == 1

        def local_scan(c, carry, direction=direction, reverse=reverse):
            a, u = gates(c, direction)
            acc_a, acc_h = scan_rows(a, u, B_CHUNK, tile_shifts, row_in_tile, reverse)
            rows = pl.ds(pl.multiple_of(c * B_CHUNK, B_CHUNK), B_CHUNK)
            a_ref[rows, :] = acc_a
            h_ref[rows, :] = acc_h
            return carry

        lax.fori_loop(0, nchunk, local_scan, 0)

        edge = 0 if reverse else SUBLANE - 1
        sum_a = a_ref[pl.ds(edge, ntile, stride=SUBLANE), :]
        sum_h = h_ref[pl.ds(edge, ntile, stride=SUBLANE), :]
        _, inc_h = scan_rows(sum_a, sum_h, ntile, summary_shifts, tile_idx, reverse)
        if reverse:
            state_in = jnp.where(tile_idx < ntile - 1, pltpu.roll(inc_h, ntile - 1, axis=0), 0.0)
        else:
            state_in = jnp.where(tile_idx >= 1, pltpu.roll(inc_h, 1, axis=0), 0.0)
        for r in range(SUBLANE):
            c_ref[pl.ds(r, ntile, stride=SUBLANE), :] = state_in

        def apply_state(c, carry, direction=direction):
            rows = pl.ds(pl.multiple_of(c * B_CHUNK, B_CHUNK), B_CHUNK)
            h = h_ref[rows, :] + a_ref[rows, :] * c_ref[rows, :]
            if direction == 0:
                hf_ref[rows, :] = h
            else:
                gate = g_ref[rows, :].astype(F32)
                o_ref[rows, :] = ((hf_ref[rows, :] + h) * _silu(gate)).astype(o_ref.dtype)
            return carry

        lax.fori_loop(0, nchunk, apply_state, 0)


def _mixer_b(p3, conv_w, conv_b, w_gate, b_gate, softplus_neg_lam):
    b, s, _ = p3.shape
    ngrp = B_WIDTH // LANE
    assert s % B_CHUNK == 0
    return pl.pallas_call(
        _b_kernel,
        grid=(b, ngrp),
        in_specs=[pl.BlockSpec((None, s, LANE), lambda bi, j: (bi, 0, U_BX + j)),
                  pl.BlockSpec((None, s, LANE), lambda bi, j: (bi, 0, U_BG + j)),
                  pl.BlockSpec((4, LANE), lambda bi, j: (0, j)),
                  pl.BlockSpec((1, LANE), lambda bi, j: (0, j)),
                  pl.BlockSpec((None, 2, LANE, 2 * LANE), lambda bi, j: (j, 0, 0, 0)),
                  pl.BlockSpec((None, 2, 2 * LANE), lambda bi, j: (j, 0, 0)),
                  pl.BlockSpec((2, LANE), lambda bi, j: (0, j))],
        out_specs=pl.BlockSpec((None, s, LANE), lambda bi, j: (bi, 0, j)),
        out_shape=jax.ShapeDtypeStruct((b, s, B_WIDTH), BF16),
        scratch_shapes=[pltpu.VMEM((s + 2 * B_PAD, LANE), F32)] + [pltpu.VMEM((s, LANE), F32)] * 4,
        compiler_params=_cparams("parallel", "parallel"),
        name="mixer_b",
    )(p3, p3, conv_w, conv_b, w_gate, b_gate, softplus_neg_lam)


def _b_gate_weights(w_r, b_r, w_i, b_i):
    per = LANE // B_BLOCK_DIM
    ngrp = B_WIDTH // LANE

    def blockdiag(w):
        w = w.reshape(2, ngrp, per, B_BLOCK_DIM, B_BLOCK_DIM)
        eye = jnp.eye(per, dtype=w.dtype)
        full = jnp.einsum("dgpcx,pq->dgpcqx", w, eye).reshape(2, ngrp, LANE, LANE)
        return full.transpose(1, 0, 2, 3)

    w = jnp.concatenate([blockdiag(w_r), blockdiag(w_i)], axis=-1).astype(BF16)
    bias = jnp.concatenate([b_r.reshape(2, ngrp, LANE), b_i.reshape(2, ngrp, LANE)], axis=-1)
    return w, bias.transpose(1, 0, 2)


ATT_STATIC_UNROLL = False


def _attend_t(streams, tk, nk, s_ref, mt_ref, m_ref, acc_ref, tile_of=None):
    assert nk % 2 == 0 and nk >= 2
    if tile_of is None:
        tile_of = lambda u: u
    for i in range(len(streams)):
        m_ref[i] = jnp.full(m_ref.shape[1:], NEG, F32)
        acc_ref[i] = jnp.zeros(acc_ref.shape[1:], F32)

    def key_rows(u):
        return pl.ds(pl.multiple_of(tile_of(u) * tk, tk), tk)

    def scores(u, slot):
        rows = key_rows(u)
        for i, st in enumerate(streams):
            s = jnp.dot(st["k_ref"][rows, :], st["q"](u), preferred_element_type=F32)
            s_ref[slot, i] = s
            mt_ref[slot, i] = jnp.max(s, axis=0, keepdims=True)

    def accumulate(u, slot, first=False):
        rows = key_rows(u)
        for i, st in enumerate(streams):
            s = s_ref[slot, i]
            if first and st.get("bias0") is not None:
                s = s + st["bias0"][...]
                m_tile = jnp.max(s, axis=0, keepdims=True)
            else:
                m_tile = mt_ref[slot, i]
            off = st["off"](u) if st.get("off") is not None else None
            m_old = m_ref[i]
            m_new = jnp.maximum(m_old, m_tile if off is None else m_tile + off)
            alpha = jnp.exp2(m_old - m_new)
            p = jnp.exp2(s - (m_new if off is None else m_new - off)).astype(BF16)
            acc_ref[i] = alpha * acc_ref[i] + jnp.dot(st["vt_ref"][:, rows], p, preferred_element_type=F32)
            m_ref[i] = m_new

    scores(0, 0)
    scores(1, 1)
    accumulate(0, 0, first=True)

    def body(j, carry):
        u = 1 + 2 * j
        scores(u + 1, 0)
        accumulate(u, 1)
        scores(u + 2, 1)
        accumulate(u + 1, 0)
        return carry

    if ATT_STATIC_UNROLL:
        for j in range((nk - 2) // 2):
            body(j, 0)
    else:
        lax.fori_loop(0, (nk - 2) // 2, body, 0)
    accumulate(nk - 1, 1)


C_AUG = 3
C_VROWS = C_VDIM + 16
C_MAPS = ((0, C_HALF), (1, 0))


def _c_kernel(q_ref, k_ref, v_ref, g_ref, lam_ref, subln_ref, o_ref,
              kaug_ref, vt_ref, dbias_ref, s_ref, mt_ref, m_ref, acc_ref, *, tk, slopes, lam_init):
    h = pl.program_id(1)
    qi = pl.program_id(2)
    tq = q_ref.shape[0]
    s_len = k_ref.shape[0]
    nk = s_len // tk
    assert tq == tk
    lp = lam_ref[...]
    lam = (jnp.exp(jnp.sum(lp[0:1, :] * lp[1:2, :], axis=-1, keepdims=True))
           - jnp.exp(jnp.sum(lp[2:3, :] * lp[3:4, :], axis=-1, keepdims=True)) + lam_init)
    slope = jnp.where(h == 0, slopes[0], jnp.where(h == 1, slopes[1], jnp.where(h == 2, slopes[2], slopes[3])))
    c = slope.astype(F32) * LOG2E

    @pl.when(qi == 0)
    def _build_key_side():
        lane = lax.broadcasted_iota(jnp.int32, (1, LANE), 1)
        for ch in range(nk):
            rows = slice(ch * tk, (ch + 1) * tk)
            k = k_ref[rows, :]
            jpos = (ch * tk + lax.broadcasted_iota(jnp.int32, (tk, 1), 0)).astype(F32)
            cj = c * jpos
            hi = cj.astype(BF16).astype(F32)
            mid = (cj - hi).astype(BF16).astype(F32)
            lo = (cj - hi - mid).astype(BF16).astype(F32)
            for cmap, base in C_MAPS:
                aug = jnp.where(lane == base, hi, jnp.where(lane == base + 1, mid,
                                                            jnp.where(lane == base + 2, lo, 0.0)))
                keep = (lane < C_HALF) if cmap == 0 else (lane >= C_HALF)
                kaug_ref[cmap, rows, :] = jnp.where(keep, k, aug.astype(BF16))
            vt_ref[0:C_VDIM, rows] = v_ref[rows, :].astype(F32).T.astype(BF16)
        r16 = lax.broadcasted_iota(jnp.int32, (C_VROWS - C_VDIM, s_len), 0)
        vt_ref[C_VDIM:C_VROWS, :] = jnp.where(r16 == 0, 1.0, 0.0).astype(BF16)
        ii = lax.broadcasted_iota(jnp.int32, (tk, tq), 1)
        jj = lax.broadcasted_iota(jnp.int32, (tk, tq), 0)
        dbias_ref[...] = -c * jnp.abs(ii - jj).astype(F32)

    def tile_of(u):
        if isinstance(u, int) and u == 0:
            return qi
        return u - 1 + (u - 1 >= qi).astype(jnp.int32)

    def sign_of(u):
        return jnp.where(tile_of(u) < qi, 1.0, -1.0).astype(F32)

    q_t = (q_ref[...].astype(F32) * (C_HALF ** -0.5 * LOG2E)).T
    row = lax.broadcasted_iota(jnp.int32, (LANE, 1), 0)
    ci = c * (qi * tq + lax.broadcasted_iota(jnp.int32, (1, tq), 1)).astype(F32)
    streams = []
    for cmap, base in C_MAPS:
        keep = (row < C_HALF) if cmap == 0 else (row >= C_HALF)
        q_base = jnp.where(keep, q_t, 0.0)
        aug_rows = jnp.where((row >= base) & (row < base + C_AUG), 1.0, 0.0)

        def q_of(u, q_base=q_base, aug_rows=aug_rows):
            if isinstance(u, int) and u == 0:
                return q_base.astype(BF16)
            return (q_base + sign_of(u) * aug_rows).astype(BF16)

        def off_of(u):
            if isinstance(u, int) and u == 0:
                return None
            return -sign_of(u) * ci

        streams.append(dict(q=q_of, k_ref=kaug_ref.at[cmap], vt_ref=vt_ref, off=off_of, bias0=dbias_ref))
    _attend_t(streams, tk, nk, s_ref, mt_ref, m_ref, acc_ref, tile_of)

    o_t = []
    for cmap in range(2):
        acc = acc_ref[cmap]
        o_t.append(acc[:C_VDIM, :] / acc[C_VDIM:C_VDIM + 1, :])
    o = (o_t[0] - lam * o_t[1]).T
    ms = jnp.mean(o * o, axis=-1, keepdims=True)
    o = o * lax.rsqrt(ms + EPS) * subln_ref[...] * (1.0 - lam_init)
    o_ref[...] = (o * _silu(g_ref[...].astype(F32))).astype(o_ref.dtype)


def _mixer_c(p3, lam_params, subln, layer, tq, tk):
    b, s, _ = p3.shape
    lam_init = 0.8 - 0.6 * math.exp(-0.3 * layer)
    slopes = tuple(2.0 ** (-8.0 * (i + 1) / C_HEADS) for i in range(C_HEADS))
    kern = functools.partial(_c_kernel, tk=tk, slopes=slopes, lam_init=lam_init)
    return pl.pallas_call(
        kern,
        grid=(b, C_HEADS, s // tq),
        in_specs=[pl.BlockSpec((None, tq, LANE), lambda bi, h, i: (bi, i, U_CQ + h)),
                  pl.BlockSpec((None, s, LANE), lambda bi, h, i: (bi, 0, U_CK + h)),
                  pl.BlockSpec((None, s, LANE), lambda bi, h, i: (bi, 0, U_CV + h)),
                  pl.BlockSpec((None, tq, LANE), lambda bi, h, i: (bi, i, U_CG + h)),
                  pl.BlockSpec((4, C_HALF), lambda bi, h, i: (0, 0)),
                  pl.BlockSpec((1, C_VDIM), lambda bi, h, i: (0, 0))],
        out_specs=pl.BlockSpec((None, tq, LANE), lambda bi, h, i: (bi, i, h)),
        out_shape=jax.ShapeDtypeStruct((b, s, C_WIDTH), BF16),
        scratch_shapes=[pltpu.VMEM((2, s, LANE), BF16), pltpu.VMEM((C_VROWS, s), BF16),
                        pltpu.VMEM((tk, tq), F32), pltpu.VMEM((2, 2, tk, tq), F32),
                        pltpu.VMEM((2, 2, 1, tq), F32), pltpu.VMEM((2, 1, tq), F32),
                        pltpu.VMEM((2, C_VROWS, tq), F32)],
        compiler_params=_cparams("parallel", "parallel", "arbitrary"),
        name="mixer_c",
    )(p3, p3, p3, p3, lam_params, subln)


def _dprep_kernel(cq_ref, ckv_ref, kr_ref, qn_ref, kvn_ref, wqt_ref, wqst_ref, wk_ref, wvt_ref,
                  e1_ref, e2_ref, cos_ref, sin_ref, cost_ref, sint_ref, vonet_ref, qt_out, k_out, vt_out):
    def norm(x_ref, gain_ref):
        x = x_ref[...].astype(F32)
        ms = jnp.mean(x * x, axis=-1, keepdims=True)
        return (x * lax.rsqrt(ms + EPS) * gain_ref[...]).astype(BF16)

    cqn = norm(cq_ref, qn_ref)
    ckvn = norm(ckv_ref, kvn_ref)
    kr = kr_ref[...]
    cos = jnp.concatenate([cos_ref[...]] * D_HEADS, axis=1)
    sin = jnp.concatenate([sin_ref[...]] * D_HEADS, axis=1)
    cos_t = jnp.concatenate([cost_ref[...]] * D_HEADS, axis=0)
    sin_t = jnp.concatenate([sint_ref[...]] * D_HEADS, axis=0)
    q_t = (lax.dot_general(wqt_ref[...], cqn, _NT, preferred_element_type=F32) * cos_t
           + lax.dot_general(wqst_ref[...], cqn, _NT, preferred_element_type=F32) * sin_t)
    qt_out[...] = (q_t * ((D_NOPE + D_ROPE) ** -0.5 * LOG2E)).astype(qt_out.dtype)
    k = (jnp.dot(ckvn, wk_ref[...], preferred_element_type=F32)
         + jnp.dot(kr, e1_ref[...], preferred_element_type=F32) * cos
         + jnp.dot(kr, e2_ref[...], preferred_element_type=F32) * sin)
    k_out[...] = k.astype(k_out.dtype)
    v_t = lax.dot_general(wvt_ref[...], ckvn, _NT, preferred_element_type=F32) + vonet_ref[...]
    vt_out[...] = v_t.astype(vt_out.dtype)


def _d_tables(s):
    inv = ROPE_BASE ** (-jnp.arange(0, D_ROPE, 2, dtype=F32) / D_ROPE)
    ang = jnp.arange(s, dtype=F32)[:, None] * inv[None, :]
    cos, sin = jnp.cos(ang), jnp.sin(ang)
    pad = LANE - D_NOPE - D_ROPE
    cos_t = jnp.concatenate([jnp.ones((s, D_NOPE), F32), cos, cos, jnp.zeros((s, pad), F32)], axis=1)
    sin_t = jnp.concatenate([jnp.zeros((s, D_NOPE), F32), -sin, sin, jnp.zeros((s, pad), F32)], axis=1)
    return cos_t, sin_t


def _d_weights(w_uq, w_ukv):
    half = D_ROPE // 2
    pad = LANE - D_NOPE - D_ROPE
    wq = w_uq.reshape(D_QLR, D_HEADS, D_NOPE + D_ROPE)
    zq = jnp.zeros((D_QLR, D_HEADS, pad), w_uq.dtype)
    wq_main = jnp.concatenate([wq, zq], axis=-1).reshape(D_QLR, D_HEADS * LANE)
    wq_swap = jnp.concatenate([jnp.zeros((D_QLR, D_HEADS, D_NOPE), w_uq.dtype),
                               wq[..., D_NOPE + half:], wq[..., D_NOPE:D_NOPE + half], zq],
                              axis=-1).reshape(D_QLR, D_HEADS * LANE)
    wkv = w_ukv.reshape(D_KVLR, D_HEADS, D_NOPE + D_VDIM)
    zk = jnp.zeros((D_KVLR, D_HEADS, LANE - D_NOPE), w_ukv.dtype)
    wk = jnp.concatenate([wkv[..., :D_NOPE], zk], axis=-1).reshape(D_KVLR, D_HEADS * LANE)
    wv = jnp.concatenate([wkv[..., D_NOPE:], jnp.zeros((D_KVLR, D_HEADS, D_VROWS - D_VDIM), w_ukv.dtype)],
                         axis=-1).reshape(D_KVLR, D_HEADS * D_VROWS)
    src = jnp.arange(D_ROPE)
    e1 = jnp.zeros((LANE, LANE), F32).at[src, D_NOPE + src].set(1.0)
    e2 = jnp.zeros((LANE, LANE), F32).at[(src + half) % D_ROPE, D_NOPE + src].set(1.0)
    e1 = jnp.tile(e1, (1, D_HEADS))
    e2 = jnp.tile(e2, (1, D_HEADS))
    vone = jnp.zeros((D_VROWS, 1), F32).at[D_VDIM, 0].set(1.0)
    vone = jnp.tile(vone, (D_HEADS, 1))
    return (wq_main.T.astype(BF16), wq_swap.T.astype(BF16), wk.astype(BF16), wv.T.astype(BF16),
            e1.astype(BF16), e2.astype(BF16), vone)


def _dprep(p3, q_norm, kv_norm, dw, cos_t, sin_t, tm):
    b, s, _ = p3.shape
    wqt, wqst, wk, wvt, e1, e2, vonet = dw
    wide = D_HEADS * LANE
    vwide = D_HEADS * D_VROWS
    full = lambda shape: pl.BlockSpec(shape, lambda bi, i: (0,) * len(shape))
    tok_major = pl.BlockSpec((None, tm, wide), lambda bi, i: (bi, i, 0))
    feat_major = lambda rows: pl.BlockSpec((None, rows, tm), lambda bi, i: (bi, 0, i))
    return pl.pallas_call(
        _dprep_kernel,
        grid=(b, s // tm),
        in_specs=[pl.BlockSpec((None, tm, D_QLR), lambda bi, i: (bi, i, U_DCQ // 2)),
                  pl.BlockSpec((None, tm, LANE), lambda bi, i: (bi, i, U_DCKV)),
                  pl.BlockSpec((None, tm, LANE), lambda bi, i: (bi, i, U_DKR)),
                  full((1, D_QLR)), full((1, D_KVLR)),
                  full((wide, D_QLR)), full((wide, D_QLR)), full((D_KVLR, wide)), full((vwide, D_KVLR)),
                  full((LANE, wide)), full((LANE, wide)),
                  pl.BlockSpec((tm, LANE), lambda bi, i: (i, 0)),
                  pl.BlockSpec((tm, LANE), lambda bi, i: (i, 0)),
                  pl.BlockSpec((LANE, tm), lambda bi, i: (0, i)),
                  pl.BlockSpec((LANE, tm), lambda bi, i: (0, i)),
                  full((vwide, 1))],
        out_specs=[feat_major(wide), tok_major, feat_major(vwide)],
        out_shape=[jax.ShapeDtypeStruct((b, wide, s), BF16), jax.ShapeDtypeStruct((b, s, wide), BF16),
                   jax.ShapeDtypeStruct((b, vwide, s), BF16)],
        compiler_params=_cparams("parallel", "parallel"),
        name="mixer_d_prep",
    )(p3, p3, p3, q_norm, kv_norm, wqt, wqst, wk, wvt, e1, e2, cos_t, sin_t, cos_t.T, sin_t.T, vonet)


def _d_kernel(qt_ref, k_ref, vt_ref, g_ref, o_ref, s_ref, mt_ref, m_ref, acc_ref, *, tk):
    streams = []
    for h in range(2):
        sl = slice(h * LANE, (h + 1) * LANE)
        vrows = slice(h * D_VROWS, (h + 1) * D_VROWS)
        streams.append(dict(q=lambda u, q=qt_ref[sl, :]: q, k_ref=k_ref.at[:, sl], vt_ref=vt_ref.at[vrows, :]))
    _attend_t(streams, tk, k_ref.shape[0] // tk, s_ref, mt_ref, m_ref, acc_ref)
    parts = []
    for h in range(2):
        acc = acc_ref[h]
        parts.append(acc[:D_VDIM, :] / acc[D_VDIM:D_VDIM + 1, :])
    o = jnp.concatenate(parts, axis=0).T
    o_ref[...] = (o * _silu(g_ref[...].astype(F32))).astype(o_ref.dtype)


def _mixer_d(p3, qt, kd, vt, tq, tk):
    b, s, _ = p3.shape
    npair = D_HEADS // 2
    return pl.pallas_call(
        functools.partial(_d_kernel, tk=tk),
        grid=(b, npair, s // tq),
        in_specs=[pl.BlockSpec((None, 2 * LANE, tq), lambda bi, hp, i: (bi, hp, i)),
                  pl.BlockSpec((None, s, 2 * LANE), lambda bi, hp, i: (bi, 0, hp)),
                  pl.BlockSpec((None, 2 * D_VROWS, s), lambda bi, hp, i: (bi, hp, 0)),
                  pl.BlockSpec((None, tq, LANE), lambda bi, hp, i: (bi, i, U_DG + hp))],
        out_specs=pl.BlockSpec((None, tq, LANE), lambda bi, hp, i: (bi, i, hp)),
        out_shape=jax.ShapeDtypeStruct((b, s, D_WIDTH), BF16),
        scratch_shapes=[pltpu.VMEM((2, 2, tk, tq), F32), pltpu.VMEM((2, 2, 1, tq), F32),
                        pltpu.VMEM((2, 1, tq), F32), pltpu.VMEM((2, D_VROWS, tq), F32)],
        compiler_params=_cparams("parallel", "parallel", "arbitrary"),
        name="mixer_d",
    )(qt, kd, vt, p3)


def _out_kernel(x_ref, ya_ref, yb_ref, yc_ref, yd_ref, g0_ref, g1_ref, g2_ref, g3_ref,
                wa_ref, wb_ref, wc_ref, wd_ref, bg_ref, wo_ref, np_ref, o_ref):
    merged = None
    for i, (y_ref, w_ref, g_ref) in enumerate(((ya_ref, wa_ref, g0_ref), (yb_ref, wb_ref, g1_ref),
                                                (yc_ref, wc_ref, g2_ref), (yd_ref, wd_ref, g3_ref))):
        t = jnp.dot(y_ref[...], w_ref[...], preferred_element_type=F32)
        gate = _sigmoid(g_ref[...].astype(F32) + bg_ref[i:i + 1, :])
        merged = gate * t if merged is None else merged + gate * t
    o = jnp.dot(merged.astype(BF16), wo_ref[...], preferred_element_type=F32)
    ms = jnp.mean(o * o, axis=-1, keepdims=True)
    o_ref[...] = x_ref[...] + o * lax.rsqrt(ms + EPS) * np_ref[...]


def _merge_out(x2, p2, ya, yb, yc, yd, wa, wb, wc, wd, b_gate, w_out, norm_post, tm):
    n = x2.shape[0]
    row = lambda width: pl.BlockSpec((tm, width), lambda i: (i, 0))
    full = lambda shape: pl.BlockSpec(shape, lambda i: (0, 0))
    gate = lambda br: pl.BlockSpec((tm, D_MODEL), lambda i, br=br: (i, U_GATE * LANE // D_MODEL + br))
    return pl.pallas_call(
        _out_kernel,
        grid=(n // tm,),
        in_specs=[row(D_MODEL), row(A_WIDTH), row(B_WIDTH), row(C_WIDTH), row(D_WIDTH),
                  gate(0), gate(1), gate(2), gate(3),
                  full((A_WIDTH, D_MODEL)), full((B_WIDTH, D_MODEL)), full((C_WIDTH, D_MODEL)),
                  full((D_WIDTH, D_MODEL)), full((N_BRANCH, D_MODEL)), full((D_MODEL, D_MODEL)),
                  full((1, D_MODEL))],
        out_specs=row(D_MODEL),
        out_shape=jax.ShapeDtypeStruct((n, D_MODEL), F32),
        compiler_params=_cparams("parallel"),
        name="merge_out",
    )(x2, ya, yb, yc, yd, p2, p2, p2, p2, wa, wb, wc, wd, b_gate, w_out, norm_post)


def _permute_w_in(w_in):
    def cols(name):
        off, size = _ORIG[name]
        return w_in[..., off:off + size]

    def zeros(n):
        return jnp.zeros(w_in.shape[:-1] + (n,), w_in.dtype)

    parts = [cols("gate"), cols("d_cq"), cols("d_ckv"), cols("d_kr"), zeros(LANE - D_ROPE), cols("d_g"),
             cols("a_q"), cols("a_k"), cols("a_v"), cols("a_g"), cols("b_x"), cols("b_g"),
             cols("c_q"), cols("c_k"), cols("c_v"), cols("c_g"), zeros(LANE)]
    w = jnp.concatenate(parts, axis=-1).astype(BF16)
    assert w.shape[-1] == P_WIDTH
    return w


def kernel(x, norm_pre, norm_post, w_in, conv_w, conv_b, lru_wr, lru_br, lru_wi, lru_bi, lru_lambda,
           diff_lam_q1, diff_lam_k1, diff_lam_q2, diff_lam_k2, diff_subln, mla_q_norm, mla_kv_norm,
           mla_w_uq, mla_w_ukv, w_br_a, w_br_b, w_br_c, w_br_d, b_gate, w_out):
    b, s, d = x.shape
    depth = w_in.shape[0]
    n = b * s
    tm_in = min(2048, n)
    tm_out = min(512, n)
    tq = min(512, s)
    tk = min(512, s)

    w_perm = _permute_w_in(w_in)
    a_bias = _a_bias_tables(jnp.asarray([2.0 ** (-8.0 * (i + 1) / A_SLOTS) for i in range(A_SLOTS)], F32))
    cos_t, sin_t = _d_tables(s)
    softplus_neg_lam = jnp.log1p(jnp.exp(-lru_lambda.astype(F32)))

    x2 = x.reshape(n, d)
    for l in range(depth):
        p2 = _inproj(x2, norm_pre[l][None, :], w_perm, l, tm_in, 512)
        p3 = p2.reshape(b, s, P_WIDTH)
        ya = _mixer_a(p3, a_bias)
        bw, bb = _b_gate_weights(lru_wr[l], lru_br[l], lru_wi[l], lru_bi[l])
        yb = _mixer_b(p3, conv_w[l], conv_b[l][None, :], bw, bb, softplus_neg_lam[l])
        lam_params = jnp.stack([diff_lam_q1[l], diff_lam_k1[l], diff_lam_q2[l], diff_lam_k2[l]])
        yc = _mixer_c(p3, lam_params, diff_subln[l][None, :], l, tq, tk)
        dw = _d_weights(mla_w_uq[l], mla_w_ukv[l])
        qd, kd, vd = _dprep(p3, mla_q_norm[l][None, :], mla_kv_norm[l][None, :], dw, cos_t, sin_t, min(1024, s))
        yd = _mixer_d(p3, qd, kd, vd, min(2 * tq, s), tk)
        x2 = _merge_out(x2, p2, ya.reshape(n, -1), yb.reshape(n, -1), yc.reshape(n, -1), yd.reshape(n, -1),
                        w_br_a[l].astype(BF16), w_br_b[l].astype(BF16), w_br_c[l].astype(BF16),
                        w_br_d[l].astype(BF16), b_gate[l], w_out[l].astype(BF16), norm_post[l][None, :], tm_out)
    return x2.reshape(b, s, d)
```

```python
import functools
import math

import jax
import jax.numpy as jnp
from jax import lax
from jax.experimental import pallas as pl
from jax.experimental.pallas import tpu as pltpu

F32 = jnp.float32
BF16 = jnp.bfloat16

D_MODEL = 1024
EPS = 1e-6
N_BRANCH = 4

A_PATTERNS = ((128, 1), (512, 4), (2048, 16))
A_SLOTS = 6
A_HEAD_DIM = 64
A_QKV = 1152
A_WIDTH = 384
A_RADIUS = 64
A_QBLK = 128
A_KWIN = 256
A_UNROLL = 8

B_WIDTH = 384
B_BLOCK_DIM = 64
B_C = 8.0

C_HEADS = 4
C_HALF = 64
C_VDIM = 128
C_QK = 512
C_WIDTH = 512

D_HEADS = 6
D_NOPE = 64
D_ROPE = 32
D_VDIM = 64
D_QLR = 256
D_KVLR = 128
D_WIDTH = 384
D_VROWS = D_VDIM + 16
ROPE_BASE = 10000.0

LANE = 128
SUBLANE = 8
NEG = -1e30
LOG2E = math.log2(math.e)
_NT = (((1,), (1,)), ((), ()))
VMEM_LIMIT = 56 * 1024 * 1024

U_GATE, U_DG, U_DCQ, U_DCKV, U_DKR = 0, 32, 36, 38, 39
U_AQ, U_AK, U_AV, U_AG = 40, 49, 58, 67
U_BX, U_BG = 70, 73
U_CQ, U_CK, U_CV, U_CG = 76, 80, 84, 88
U_TOTAL = 92
P_WIDTH = U_TOTAL * LANE

W_GROUP = 4
W_NGROUP = U_TOTAL // W_GROUP
W_GATE_GROUPS = N_BRANCH * D_MODEL // LANE // W_GROUP
W_DG_GROUP = W_GATE_GROUPS
W_DSMALL_GROUP = W_DG_GROUP + 1
W_MAIN_UNITS = (3 * A_QKV + A_WIDTH + 2 * B_WIDTH + 2 * C_QK + 2 * C_WIDTH) // LANE
W_DKR_UNIT = W_MAIN_UNITS + (D_QLR + D_KVLR) // LANE
W_GATE_UNIT = W_DKR_UNIT + D_WIDTH // LANE
assert (U_DCQ, U_AQ) == ((W_DSMALL_GROUP) * W_GROUP, (W_DSMALL_GROUP + 1) * W_GROUP)


def _cparams(*sem):
    return pltpu.CompilerParams(dimension_semantics=sem, vmem_limit_bytes=VMEM_LIMIT)


def _silu(x):
    return x * (1.0 / (1.0 + jnp.exp(-x)))


def _sigmoid(x):
    return 1.0 / (1.0 + jnp.exp(-x))


def _inproj_kernel(x_ref, g_ref, w_ref, o_ref, h_ref):
    @pl.when(pl.program_id(1) == 0)
    def _():
        x = x_ref[...]
        ms = jnp.mean(x * x, axis=-1, keepdims=True)
        h_ref[...] = (x * lax.rsqrt(ms + EPS) * g_ref[...]).astype(BF16)

    o_ref[...] = jnp.dot(h_ref[...], w_ref[...], preferred_element_type=F32).astype(o_ref.dtype)


def _inproj(x2, gain, w_all, layer, tm, tn):
    n = x2.shape[0]
    return pl.pallas_call(
        _inproj_kernel,
        grid=(n // tm, P_WIDTH // tn),
        in_specs=[pl.BlockSpec((tm, D_MODEL), lambda i, j: (i, 0)),
                  pl.BlockSpec((1, D_MODEL), lambda i, j: (0, 0)),
                  pl.BlockSpec((None, D_MODEL, tn), lambda i, j: (layer, 0, j))],
        out_specs=pl.BlockSpec((tm, tn), lambda i, j: (i, j)),
        out_shape=jax.ShapeDtypeStruct((n, P_WIDTH), BF16),
        scratch_shapes=[pltpu.VMEM((tm, D_MODEL), BF16)],
        compiler_params=_cparams("parallel", "arbitrary"),
        name="inproj",
    )(x2, gain, w_all)


def _a_bias_tables(slopes):
    ii = jnp.arange(A_QBLK, dtype=jnp.int32)[:, None]
    jj = jnp.arange(A_KWIN, dtype=jnp.int32)[None, :]
    out = []
    for _, dil in A_PATTERNS:
        per_edge = []
        for off in (0, A_RADIUS, A_QBLK):
            rel = jnp.abs(off + ii - jj)
            dist = (rel * dil).astype(F32)
            b = -slopes[:, None, None] * dist[None] * LOG2E
            per_edge.append(jnp.where((rel <= A_RADIUS)[None], b, NEG))
        out.append(jnp.stack(per_edge))
    return jnp.stack(out)


def _a_pitch(dil):
    return dil + SUBLANE if dil % (2 * SUBLANE) == 0 else dil


def _a_kernel(q0, k0, v0, q1, k1, v1, q2, k2, v2, gate_ref, bias_ref, o_ref,
              qf, kf, vf, u_ref, z_ref, m_ref, up_ref, zp_ref, mp_ref):
    s_len = q0.shape[0]
    lane = lax.broadcasted_iota(jnp.int32, (1, LANE), 1)
    first = lane < A_HEAD_DIM
    ones_first = jnp.where(first, 1.0, 0.0).astype(BF16)
    ones_second = jnp.where(first, 0.0, 1.0).astype(BF16)
    scale = A_HEAD_DIM ** -0.5 * LOG2E
    padded_groups = [g for g, (_, dil) in enumerate(A_PATTERNS) if _a_pitch(dil) != dil]
    assert padded_groups == [len(A_PATTERNS) - 1]

    for g, ((_, dil), (qr, kr, vr)) in enumerate(zip(A_PATTERNS, ((q0, k0, v0), (q1, k1, v1), (q2, k2, v2)))):
        sub_len = s_len // dil
        nqb = sub_len // A_QBLK
        pitch = _a_pitch(dil)
        if pitch != dil:
            def stage(l2, carry, dil=dil, pitch=pitch, qr=qr, kr=kr, vr=vr):
                src = pl.ds(pl.multiple_of(l2 * 2 * dil, 2 * dil), 2 * dil)
                dst = pl.multiple_of(l2 * 2 * pitch, SUBLANE)
                for ref, buf, mul in ((qr, qf, scale), (kr, kf, None), (vr, vf, None)):
                    x = ref[src, :].astype(F32)
                    x = x if mul is None else x * mul
                    buf[pl.ds(dst, dil), :] = x[:dil]
                    buf[pl.ds(dst + pitch, dil), :] = x[dil:]
                return carry

            lax.fori_loop(0, sub_len // 2, stage, 0, unroll=4)
        elif dil > 1:
            rows = pl.ds(0, s_len)
            qf[rows, :] = qr[...].astype(F32) * scale
            kf[rows, :] = kr[...].astype(F32)
            vf[rows, :] = vr[...].astype(F32)

        def block(idx, carry, g=g, dil=dil, pitch=pitch, sub_len=sub_len, nqb=nqb, qr=qr, kr=kr, vr=vr):
            r = idx // nqb
            qb = idx % nqb
            qs = qb * A_QBLK
            ws = jnp.clip(qs - A_RADIUS, 0, sub_len - A_KWIN)
            edge = jnp.where(qb == 0, 0, jnp.where(qb == nqb - 1, 2, 1))
            if dil == 1:
                qrows = pl.ds(pl.multiple_of(qs, A_QBLK), A_QBLK)
                krows = pl.ds(pl.multiple_of(ws, A_RADIUS), A_KWIN)
                q = (qr[qrows, :].astype(F32) * scale).astype(BF16)
                k = kr[krows, :]
                v = vr[krows, :]
            else:
                qrows = pl.ds(r + qs * pitch, A_QBLK, stride=pitch)
                krows = pl.ds(r + ws * pitch, A_KWIN, stride=pitch)
                q = qf[qrows, :].astype(BF16)
                k = kf[krows, :].astype(BF16)
                v = vf[krows, :].astype(BF16)
            zq = jnp.zeros_like(q)
            zv = jnp.zeros_like(v)
            uz = None
            ms = []
            for h, head_lanes in enumerate((first, jnp.logical_not(first))):
                qh = jnp.where(head_lanes, q, zq)
                s = lax.dot_general(qh, k, (((1,), (1,)), ((), ())), preferred_element_type=F32)
                s = s + bias_ref[g, edge, h]
                mh = jnp.max(s, axis=-1, keepdims=True)
                p = jnp.exp2(s - mh).astype(BF16)
                ones_h = ones_first if h == 0 else ones_second
                vaug = jnp.concatenate([jnp.where(head_lanes, v, zv),
                                        jnp.broadcast_to(ones_h, v.shape)], axis=1)
                part = jnp.dot(p, vaug, preferred_element_type=F32)
                uz = part if uz is None else uz + part
                ms.append(mh)
            u = uz[:, :LANE]
            z = uz[:, LANE:]
            m = jnp.where(first, ms[0], ms[1])
            if g == 0:
                u_ref[qrows, :] = u
                z_ref[qrows, :] = z
                m_ref[qrows, :] = m
            elif pitch != dil:
                up_ref[qrows, :] = u
                zp_ref[qrows, :] = z
                mp_ref[qrows, :] = m
            else:
                m_old = m_ref[qrows, :]
                m_new = jnp.maximum(m_old, m)
                a = jnp.exp2(m_old - m_new)
                b = jnp.exp2(m - m_new)
                u_ref[qrows, :] = a * u_ref[qrows, :] + b * u
                z_ref[qrows, :] = a * z_ref[qrows, :] + b * z
                m_ref[qrows, :] = m_new
            return carry

        lax.fori_loop(0, dil * nqb, block, 0, unroll=A_UNROLL)

    dil = A_PATTERNS[padded_groups[0]][1]
    pitch = _a_pitch(dil)

    def finish(l2, carry):
        rows = pl.ds(pl.multiple_of(l2 * 2 * dil, 2 * dil), 2 * dil)
        src = pl.multiple_of(l2 * 2 * pitch, SUBLANE)

        def padded(ref):
            return jnp.concatenate([ref[pl.ds(src, dil), :], ref[pl.ds(src + pitch, dil), :]], axis=0)

        m_a, m_b = m_ref[rows, :], padded(mp_ref)
        m_new = jnp.maximum(m_a, m_b)
        a = jnp.exp2(m_a - m_new)
        b = jnp.exp2(m_b - m_new)
        u = a * u_ref[rows, :] + b * padded(up_ref)
        z = a * z_ref[rows, :] + b * padded(zp_ref)
        o_ref[rows, :] = (u / z * _silu(gate_ref[rows, :].astype(F32))).astype(o_ref.dtype)
        return carry

    lax.fori_loop(0, s_len // (2 * dil), finish, 0, unroll=4)


def _mixer_a(p3, bias):
    b, s, _ = p3.shape
    npair = A_SLOTS // 2
    assert s // A_PATTERNS[-1][1] >= A_KWIN
    padded_rows = max(s // dil * _a_pitch(dil) for _, dil in A_PATTERNS)

    def col(unit):
        return pl.BlockSpec((None, s, LANE), lambda bi, hp, unit=unit: (bi, 0, unit + hp))

    in_specs = []
    for g in range(len(A_PATTERNS)):
        for base in (U_AQ, U_AK, U_AV):
            in_specs.append(col(base + g * npair))
    in_specs.append(col(U_AG))
    in_specs.append(pl.BlockSpec((len(A_PATTERNS), 3, 2, A_QBLK, A_KWIN), lambda bi, hp: (0, 0, hp, 0, 0)))
    return pl.pallas_call(
        _a_kernel,
        grid=(b, npair),
        in_specs=in_specs,
        out_specs=pl.BlockSpec((None, s, LANE), lambda bi, hp: (bi, 0, hp)),
        out_shape=jax.ShapeDtypeStruct((b, s, A_WIDTH), BF16),
        scratch_shapes=([pltpu.VMEM((padded_rows, LANE), F32)] * 3 + [pltpu.VMEM((s, LANE), F32)] * 3
                        + [pltpu.VMEM((padded_rows, LANE), F32)] * 3),
        compiler_params=_cparams("parallel", "parallel"),
        name="mixer_a",
    )(*([p3] * 10), bias)


B_CHUNK = 256
B_PAD = 8


def _b_kernel(x_ref, g_ref, cw_ref, cb_ref, w_ref, bias_ref, sp_ref, o_ref, xs_ref, a_ref, h_ref, c_ref):
    s_len = x_ref.shape[0]
    nchunk = s_len // B_CHUNK
    ntile = s_len // SUBLANE
    zpad = jnp.zeros((B_PAD, LANE), F32)
    xs_ref[pl.ds(0, B_PAD), :] = zpad
    xs_ref[pl.ds(B_PAD + s_len, B_PAD), :] = zpad
    xs_ref[pl.ds(B_PAD, s_len), :] = x_ref[...].astype(F32)
    row_in_tile = lax.broadcasted_iota(jnp.int32, (1, SUBLANE, 1), 1)
    tile_idx = lax.broadcasted_iota(jnp.int32, (ntile, 1), 0)

    def scan_rows(acc_a, acc_h, axis, shifts, pos, reverse):
        n = acc_a.shape[axis]
        for sh in shifts:
            if reverse:
                sa = pltpu.roll(acc_a, n - sh, axis=axis)
                shh = pltpu.roll(acc_h, n - sh, axis=axis)
                keep = pos < n - sh
            else:
                sa = pltpu.roll(acc_a, sh, axis=axis)
                shh = pltpu.roll(acc_h, sh, axis=axis)
                keep = pos >= sh
            acc_h = acc_h + acc_a * jnp.where(keep, shh, 0.0)
            acc_a = acc_a * jnp.where(keep, sa, 1.0)
        return acc_a, acc_h

    tile_shifts = (1, 2, 4)
    summary_shifts = tuple(1 << e for e in range(ntile.bit_length() - 1))
    assert 1 << len(summary_shifts) == ntile

    def local_scan(c, carry):
        t0 = pl.multiple_of(c * B_CHUNK, B_CHUNK)
        win = xs_ref[pl.ds(t0, B_CHUNK + 2 * B_PAD), :]
        n = B_CHUNK + 2 * B_PAD
        xc = (cb_ref[...]
              + pltpu.roll(win, 1, axis=0) * cw_ref[0:1, :]
              + win * cw_ref[1:2, :]
              + pltpu.roll(win, n - 1, axis=0) * cw_ref[2:3, :]
              + pltpu.roll(win, n - 2, axis=0) * cw_ref[3:4, :])
        xc = xc[B_PAD:B_PAD + B_CHUNK, :]
        ri = jnp.dot(xc.astype(BF16), w_ref[...], preferred_element_type=F32) + bias_ref[...]
        rows = pl.ds(t0, B_CHUNK)
        for direction in (0, 1):
            r = _sigmoid(ri[:, (2 * direction) * LANE:(2 * direction + 1) * LANE])
            i = _sigmoid(ri[:, (2 * direction + 1) * LANE:(2 * direction + 2) * LANE])
            log_a = (-B_C) * r * sp_ref[direction:direction + 1, :]
            a = jnp.exp(log_a)
            u = jnp.sqrt(-jnp.tanh(log_a) * (a * a + 1.0)) * (i * xc)
            tiles = (B_CHUNK // SUBLANE, SUBLANE, LANE)
            acc_a, acc_h = scan_rows(a.reshape(tiles), u.reshape(tiles), 1, tile_shifts, row_in_tile,
                                     direction == 1)
            a_ref[direction, rows, :] = acc_a.reshape(B_CHUNK, LANE)
            h_ref[direction, rows, :] = acc_h.reshape(B_CHUNK, LANE)
        return carry

    lax.fori_loop(0, nchunk, local_scan, 0)

    for direction in (0, 1):
        reverse = direction == 1
        edge = 0 if reverse else SUBLANE - 1
        sum_a = a_ref[direction, pl.ds(edge, ntile, stride=SUBLANE), :]
        sum_h = h_ref[direction, pl.ds(edge, ntile, stride=SUBLANE), :]
        _, inc_h = scan_rows(sum_a, sum_h, 0, summary_shifts, tile_idx, reverse)
        if reverse:
            state_in = jnp.where(tile_idx < ntile - 1, pltpu.roll(inc_h, ntile - 1, axis=0), 0.0)
        else:
            state_in = jnp.where(tile_idx >= 1, pltpu.roll(inc_h, 1, axis=0), 0.0)
        for r in range(SUBLANE):
            c_ref[direction, pl.ds(r, ntile, stride=SUBLANE), :] = state_in

    def apply_state(c, carry):
        rows = pl.ds(pl.multiple_of(c * B_CHUNK, B_CHUNK), B_CHUNK)
        h = (h_ref[0, rows, :] + a_ref[0, rows, :] * c_ref[0, rows, :]
             + h_ref[1, rows, :] + a_ref[1, rows, :] * c_ref[1, rows, :])
        o_ref[rows, :] = (h * _silu(g_ref[rows, :].astype(F32))).astype(o_ref.dtype)
        return carry

    lax.fori_loop(0, nchunk, apply_state, 0)


def _mixer_b(p3, conv_w, conv_b, w_gate, b_gate, softplus_neg_lam):
    b, s, _ = p3.shape
    ngrp = B_WIDTH // LANE
    assert s % B_CHUNK == 0
    return pl.pallas_call(
        _b_kernel,
        grid=(b, ngrp),
        in_specs=[pl.BlockSpec((None, s, LANE), lambda bi, j: (bi, 0, U_BX + j)),
                  pl.BlockSpec((None, s, LANE), lambda bi, j: (bi, 0, U_BG + j)),
                  pl.BlockSpec((4, LANE), lambda bi, j: (0, j)),
                  pl.BlockSpec((1, LANE), lambda bi, j: (0, j)),
                  pl.BlockSpec((None, LANE, 4 * LANE), lambda bi, j: (j, 0, 0)),
                  pl.BlockSpec((None, 1, 4 * LANE), lambda bi, j: (j, 0, 0)),
                  pl.BlockSpec((2, LANE), lambda bi, j: (0, j))],
        out_specs=pl.BlockSpec((None, s, LANE), lambda bi, j: (bi, 0, j)),
        out_shape=jax.ShapeDtypeStruct((b, s, B_WIDTH), BF16),
        scratch_shapes=[pltpu.VMEM((s + 2 * B_PAD, LANE), F32)] + [pltpu.VMEM((2, s, LANE), F32)] * 3,
        compiler_params=_cparams("parallel", "parallel"),
        name="mixer_b",
    )(p3, p3, conv_w, conv_b, w_gate, b_gate, softplus_neg_lam)


def _b_gate_weights(w_r, b_r, w_i, b_i):
    per = LANE // B_BLOCK_DIM
    ngrp = B_WIDTH // LANE

    def blockdiag(w):
        w = w.reshape(2, ngrp, per, B_BLOCK_DIM, B_BLOCK_DIM)
        eye = jnp.eye(per, dtype=w.dtype)
        full = jnp.einsum("dgpcx,pq->dgpcqx", w, eye).reshape(2, ngrp, LANE, LANE)
        return full.transpose(1, 0, 2, 3)

    w = jnp.concatenate([blockdiag(w_r), blockdiag(w_i)], axis=-1)
    w = w.transpose(0, 2, 1, 3).reshape(ngrp, LANE, 4 * LANE).astype(BF16)
    bias = jnp.concatenate([b_r.reshape(2, ngrp, LANE), b_i.reshape(2, ngrp, LANE)], axis=-1)
    return w, bias.transpose(1, 0, 2).reshape(ngrp, 1, 4 * LANE)


ATT_STATIC_UNROLL = False


def _attend_t(streams, tk, nk, s_ref, mt_ref, m_ref, acc_ref, tile_of=None):
    assert nk % 2 == 0 and nk >= 2
    if tile_of is None:
        tile_of = lambda u: u
    for i in range(len(streams)):
        m_ref[i] = jnp.full(m_ref.shape[1:], NEG, F32)
        acc_ref[i] = jnp.zeros(acc_ref.shape[1:], F32)

    def key_rows(u):
        return pl.ds(pl.multiple_of(tile_of(u) * tk, tk), tk)

    def scores(u, slot):
        rows = key_rows(u)
        for i, st in enumerate(streams):
            s = jnp.dot(st["k_ref"][rows, :], st["q"](u), preferred_element_type=F32)
            s_ref[slot, i] = s
            mt_ref[slot, i] = jnp.max(s, axis=0, keepdims=True)

    def accumulate(u, slot, first=False):
        rows = key_rows(u)
        for i, st in enumerate(streams):
            s = s_ref[slot, i]
            if first and st.get("bias0") is not None:
                s = s + st["bias0"][...]
                m_tile = jnp.max(s, axis=0, keepdims=True)
            else:
                m_tile = mt_ref[slot, i]
            off = st["off"](u) if st.get("off") is not None else None
            m_old = m_ref[i]
            m_new = jnp.maximum(m_old, m_tile if off is None else m_tile + off)
            alpha = jnp.exp2(m_old - m_new)
            p = jnp.exp2(s - (m_new if off is None else m_new - off)).astype(BF16)
            acc_ref[i] = alpha * acc_ref[i] + jnp.dot(st["vt_ref"][:, rows], p, preferred_element_type=F32)
            m_ref[i] = m_new

    scores(0, 0)
    scores(1, 1)
    accumulate(0, 0, first=True)

    def body(j, carry):
        u = 1 + 2 * j
        scores(u + 1, 0)
        accumulate(u, 1)
        scores(u + 2, 1)
        accumulate(u + 1, 0)
        return carry

    if ATT_STATIC_UNROLL:
        for j in range((nk - 2) // 2):
            body(j, 0)
    else:
        lax.fori_loop(0, (nk - 2) // 2, body, 0)
    accumulate(nk - 1, 1)


C_AUG = 3
C_VROWS = C_VDIM + 16
C_MAPS = ((0, C_HALF), (1, 0))


def _c_kernel(q_ref, k_ref, v_ref, g_ref, lam_ref, subln_ref, o_ref,
              kaug_ref, vt_ref, dbias_ref, s_ref, mt_ref, m_ref, acc_ref, *, tk, slopes, lam_init):
    h = pl.program_id(1)
    qi = pl.program_id(2)
    tq = q_ref.shape[0]
    s_len = k_ref.shape[0]
    nk = s_len // tk
    assert tq == tk
    lp = lam_ref[...]
    lam = (jnp.exp(jnp.sum(lp[0:1, :] * lp[1:2, :], axis=-1, keepdims=True))
           - jnp.exp(jnp.sum(lp[2:3, :] * lp[3:4, :], axis=-1, keepdims=True)) + lam_init)
    slope = jnp.where(h == 0, slopes[0], jnp.where(h == 1, slopes[1], jnp.where(h == 2, slopes[2], slopes[3])))
    c = slope.astype(F32) * LOG2E

    @pl.when(qi == 0)
    def _build_key_side():
        lane = lax.broadcasted_iota(jnp.int32, (1, LANE), 1)
        for ch in range(nk):
            rows = slice(ch * tk, (ch + 1) * tk)
            k = k_ref[rows, :]
            jpos = (ch * tk + lax.broadcasted_iota(jnp.int32, (tk, 1), 0)).astype(F32)
            cj = c * jpos
            hi = cj.astype(BF16).astype(F32)
            mid = (cj - hi).astype(BF16).astype(F32)
            lo = (cj - hi - mid).astype(BF16).astype(F32)
            for cmap, base in C_MAPS:
                aug = jnp.where(lane == base, hi, jnp.where(lane == base + 1, mid,
                                                            jnp.where(lane == base + 2, lo, 0.0)))
                keep = (lane < C_HALF) if cmap == 0 else (lane >= C_HALF)
                kaug_ref[cmap, rows, :] = jnp.where(keep, k, aug.astype(BF16))
            vt_ref[0:C_VDIM, rows] = v_ref[rows, :].astype(F32).T.astype(BF16)
        r16 = lax.broadcasted_iota(jnp.int32, (C_VROWS - C_VDIM, s_len), 0)
        vt_ref[C_VDIM:C_VROWS, :] = jnp.where(r16 == 0, 1.0, 0.0).astype(BF16)
        ii = lax.broadcasted_iota(jnp.int32, (tk, tq), 1)
        jj = lax.broadcasted_iota(jnp.int32, (tk, tq), 0)
        dbias_ref[...] = -c * jnp.abs(ii - jj).astype(F32)

    def tile_of(u):
        if isinstance(u, int) and u == 0:
            return qi
        return u - 1 + (u - 1 >= qi).astype(jnp.int32)

    def sign_of(u):
        return jnp.where(tile_of(u) < qi, 1.0, -1.0).astype(F32)

    q_t = (q_ref[...].astype(F32) * (C_HALF ** -0.5 * LOG2E)).T
    row = lax.broadcasted_iota(jnp.int32, (LANE, 1), 0)
    ci = c * (qi * tq + lax.broadcasted_iota(jnp.int32, (1, tq), 1)).astype(F32)
    streams = []
    for cmap, base in C_MAPS:
        keep = (row < C_HALF) if cmap == 0 else (row >= C_HALF)
        q_base = jnp.where(keep, q_t, 0.0)
        aug_rows = jnp.where((row >= base) & (row < base + C_AUG), 1.0, 0.0)
        q_diag = q_base.astype(BF16)
        q_before = (q_base + aug_rows).astype(BF16)
        q_after = (q_base - aug_rows).astype(BF16)

        def q_of(u, q_diag=q_diag, q_before=q_before, q_after=q_after):
            if isinstance(u, int) and u == 0:
                return q_diag
            return jnp.where(tile_of(u) < qi, q_before, q_after)

        def off_of(u):
            if isinstance(u, int) and u == 0:
                return None
            return -sign_of(u) * ci

        streams.append(dict(q=q_of, k_ref=kaug_ref.at[cmap], vt_ref=vt_ref, off=off_of, bias0=dbias_ref))
    _attend_t(streams, tk, nk, s_ref, mt_ref, m_ref, acc_ref, tile_of)

    o_t = []
    for cmap in range(2):
        acc = acc_ref[cmap]
        o_t.append(acc[:C_VDIM, :] / acc[C_VDIM:C_VDIM + 1, :])
    o = (o_t[0] - lam * o_t[1]).T
    ms = jnp.mean(o * o, axis=-1, keepdims=True)
    o = o * lax.rsqrt(ms + EPS) * subln_ref[...] * (1.0 - lam_init)
    o_ref[...] = (o * _silu(g_ref[...].astype(F32))).astype(o_ref.dtype)


def _mixer_c(p3, lam_params, subln, layer, tq, tk):
    b, s, _ = p3.shape
    lam_init = 0.8 - 0.6 * math.exp(-0.3 * layer)
    slopes = tuple(2.0 ** (-8.0 * (i + 1) / C_HEADS) for i in range(C_HEADS))
    kern = functools.partial(_c_kernel, tk=tk, slopes=slopes, lam_init=lam_init)
    return pl.pallas_call(
        kern,
        grid=(b, C_HEADS, s // tq),
        in_specs=[pl.BlockSpec((None, tq, LANE), lambda bi, h, i: (bi, i, U_CQ + h)),
                  pl.BlockSpec((None, s, LANE), lambda bi, h, i: (bi, 0, U_CK + h)),
                  pl.BlockSpec((None, s, LANE), lambda bi, h, i: (bi, 0, U_CV + h)),
                  pl.BlockSpec((None, tq, LANE), lambda bi, h, i: (bi, i, U_CG + h)),
                  pl.BlockSpec((4, C_HALF), lambda bi, h, i: (0, 0)),
                  pl.BlockSpec((1, C_VDIM), lambda bi, h, i: (0, 0))],
        out_specs=pl.BlockSpec((None, tq, LANE), lambda bi, h, i: (bi, i, h)),
        out_shape=jax.ShapeDtypeStruct((b, s, C_WIDTH), BF16),
        scratch_shapes=[pltpu.VMEM((2, s, LANE), BF16), pltpu.VMEM((C_VROWS, s), BF16),
                        pltpu.VMEM((tk, tq), F32), pltpu.VMEM((2, 2, tk, tq), F32),
                        pltpu.VMEM((2, 2, 1, tq), F32), pltpu.VMEM((2, 1, tq), F32),
                        pltpu.VMEM((2, C_VROWS, tq), F32)],
        compiler_params=_cparams("parallel", "parallel", "arbitrary"),
        name="mixer_c",
    )(p3, p3, p3, p3, lam_params, subln)


def _dprep_kernel(cq_ref, ckv_ref, kr_ref, qn_ref, kvn_ref, wqt_ref, wqst_ref, wk_ref, wvt_ref,
                  e1_ref, e2_ref, cos_ref, sin_ref, cost_ref, sint_ref, vonet_ref, qt_out, k_out, vt_out):
    def norm(x_ref, gain_ref):
        x = x_ref[...].astype(F32)
        ms = jnp.mean(x * x, axis=-1, keepdims=True)
        return (x * lax.rsqrt(ms + EPS) * gain_ref[...]).astype(BF16)

    cqn = norm(cq_ref, qn_ref)
    ckvn = norm(ckv_ref, kvn_ref)
    kr = kr_ref[...]
    cos = jnp.concatenate([cos_ref[...]] * D_HEADS, axis=1)
    sin = jnp.concatenate([sin_ref[...]] * D_HEADS, axis=1)
    cos_t = jnp.concatenate([cost_ref[...]] * D_HEADS, axis=0)
    sin_t = jnp.concatenate([sint_ref[...]] * D_HEADS, axis=0)
    q_t = (lax.dot_general(wqt_ref[...], cqn, _NT, preferred_element_type=F32) * cos_t
           + lax.dot_general(wqst_ref[...], cqn, _NT, preferred_element_type=F32) * sin_t)
    qt_out[...] = (q_t * ((D_NOPE + D_ROPE) ** -0.5 * LOG2E)).astype(qt_out.dtype)
    k = (jnp.dot(ckvn, wk_ref[...], preferred_element_type=F32)
         + jnp.dot(kr, e1_ref[...], preferred_element_type=F32) * cos
         + jnp.dot(kr, e2_ref[...], preferred_element_type=F32) * sin)
    k_out[...] = k.astype(k_out.dtype)
    v_t = lax.dot_general(wvt_ref[...], ckvn, _NT, preferred_element_type=F32) + vonet_ref[...]
    vt_out[...] = v_t.astype(vt_out.dtype)


def _d_tables(s):
    inv = ROPE_BASE ** (-jnp.arange(0, D_ROPE, 2, dtype=F32) / D_ROPE)
    ang = jnp.arange(s, dtype=F32)[:, None] * inv[None, :]
    cos, sin = jnp.cos(ang), jnp.sin(ang)
    pad = LANE - D_NOPE - D_ROPE
    cos_t = jnp.concatenate([jnp.ones((s, D_NOPE), F32), cos, cos, jnp.zeros((s, pad), F32)], axis=1)
    sin_t = jnp.concatenate([jnp.zeros((s, D_NOPE), F32), -sin, sin, jnp.zeros((s, pad), F32)], axis=1)
    return cos_t, sin_t


def _d_weights(w_uq, w_ukv):
    half = D_ROPE // 2
    pad = LANE - D_NOPE - D_ROPE
    wq = w_uq.reshape(D_QLR, D_HEADS, D_NOPE + D_ROPE)
    zq = jnp.zeros((D_QLR, D_HEADS, pad), w_uq.dtype)
    wq_main = jnp.concatenate([wq, zq], axis=-1).reshape(D_QLR, D_HEADS * LANE)
    wq_swap = jnp.concatenate([jnp.zeros((D_QLR, D_HEADS, D_NOPE), w_uq.dtype),
                               wq[..., D_NOPE + half:], wq[..., D_NOPE:D_NOPE + half], zq],
                              axis=-1).reshape(D_QLR, D_HEADS * LANE)
    wkv = w_ukv.reshape(D_KVLR, D_HEADS, D_NOPE + D_VDIM)
    zk = jnp.zeros((D_KVLR, D_HEADS, LANE - D_NOPE), w_ukv.dtype)
    wk = jnp.concatenate([wkv[..., :D_NOPE], zk], axis=-1).reshape(D_KVLR, D_HEADS * LANE)
    wv = jnp.concatenate([wkv[..., D_NOPE:], jnp.zeros((D_KVLR, D_HEADS, D_VROWS - D_VDIM), w_ukv.dtype)],
                         axis=-1).reshape(D_KVLR, D_HEADS * D_VROWS)
    src = jnp.arange(D_ROPE)
    e1 = jnp.zeros((LANE, LANE), F32).at[src, D_NOPE + src].set(1.0)
    e2 = jnp.zeros((LANE, LANE), F32).at[(src + half) % D_ROPE, D_NOPE + src].set(1.0)
    e1 = jnp.tile(e1, (1, D_HEADS))
    e2 = jnp.tile(e2, (1, D_HEADS))
    vone = jnp.zeros((D_VROWS, 1), F32).at[D_VDIM, 0].set(1.0)
    vone = jnp.tile(vone, (D_HEADS, 1))
    return (wq_main.T.astype(BF16), wq_swap.T.astype(BF16), wk.astype(BF16), wv.T.astype(BF16),
            e1.astype(BF16), e2.astype(BF16), vone)


def _dprep(p3, q_norm, kv_norm, dw, cos_t, sin_t, tm):
    b, s, _ = p3.shape
    wqt, wqst, wk, wvt, e1, e2, vonet = dw
    wide = D_HEADS * LANE
    vwide = D_HEADS * D_VROWS
    full = lambda shape: pl.BlockSpec(shape, lambda bi, i: (0,) * len(shape))
    tok_major = pl.BlockSpec((None, tm, wide), lambda bi, i: (bi, i, 0))
    feat_major = lambda rows: pl.BlockSpec((None, rows, tm), lambda bi, i: (bi, 0, i))
    return pl.pallas_call(
        _dprep_kernel,
        grid=(b, s // tm),
        in_specs=[pl.BlockSpec((None, tm, D_QLR), lambda bi, i: (bi, i, U_DCQ // 2)),
                  pl.BlockSpec((None, tm, LANE), lambda bi, i: (bi, i, U_DCKV)),
                  pl.BlockSpec((None, tm, LANE), lambda bi, i: (bi, i, U_DKR)),
                  full((1, D_QLR)), full((1, D_KVLR)),
                  full((wide, D_QLR)), full((wide, D_QLR)), full((D_KVLR, wide)), full((vwide, D_KVLR)),
                  full((LANE, wide)), full((LANE, wide)),
                  pl.BlockSpec((tm, LANE), lambda bi, i: (i, 0)),
                  pl.BlockSpec((tm, LANE), lambda bi, i: (i, 0)),
                  pl.BlockSpec((LANE, tm), lambda bi, i: (0, i)),
                  pl.BlockSpec((LANE, tm), lambda bi, i: (0, i)),
                  full((vwide, 1))],
        out_specs=[feat_major(wide), tok_major, feat_major(vwide)],
        out_shape=[jax.ShapeDtypeStruct((b, wide, s), BF16), jax.ShapeDtypeStruct((b, s, wide), BF16),
                   jax.ShapeDtypeStruct((b, vwide, s), BF16)],
        compiler_params=_cparams("parallel", "parallel"),
        name="mixer_d_prep",
    )(p3, p3, p3, q_norm, kv_norm, wqt, wqst, wk, wvt, e1, e2, cos_t, sin_t, cos_t.T, sin_t.T, vonet)


def _d_kernel(qt_ref, k_ref, vt_ref, g_ref, o_ref, s_ref, mt_ref, m_ref, acc_ref, *, tk):
    streams = []
    for h in range(2):
        sl = slice(h * LANE, (h + 1) * LANE)
        vrows = slice(h * D_VROWS, (h + 1) * D_VROWS)
        streams.append(dict(q=lambda u, q=qt_ref[sl, :]: q, k_ref=k_ref.at[:, sl], vt_ref=vt_ref.at[vrows, :]))
    _attend_t(streams, tk, k_ref.shape[0] // tk, s_ref, mt_ref, m_ref, acc_ref)
    parts = []
    for h in range(2):
        acc = acc_ref[h]
        parts.append(acc[:D_VDIM, :] / acc[D_VDIM:D_VDIM + 1, :])
    o = jnp.concatenate(parts, axis=0).T
    o_ref[...] = (o * _silu(g_ref[...].astype(F32))).astype(o_ref.dtype)


def _mixer_d(p3, qt, kd, vt, tq, tk):
    b, s, _ = p3.shape
    npair = D_HEADS // 2
    return pl.pallas_call(
        functools.partial(_d_kernel, tk=tk),
        grid=(b, npair, s // tq),
        in_specs=[pl.BlockSpec((None, 2 * LANE, tq), lambda bi, hp, i: (bi, hp, i)),
                  pl.BlockSpec((None, s, 2 * LANE), lambda bi, hp, i: (bi, 0, hp)),
                  pl.BlockSpec((None, 2 * D_VROWS, s), lambda bi, hp, i: (bi, hp, 0)),
                  pl.BlockSpec((None, tq, LANE), lambda bi, hp, i: (bi, i, U_DG + hp))],
        out_specs=pl.BlockSpec((None, tq, LANE), lambda bi, hp, i: (bi, i, hp)),
        out_shape=jax.ShapeDtypeStruct((b, s, D_WIDTH), BF16),
        scratch_shapes=[pltpu.VMEM((2, 2, tk, tq), F32), pltpu.VMEM((2, 2, 1, tq), F32),
                        pltpu.VMEM((2, 1, tq), F32), pltpu.VMEM((2, D_VROWS, tq), F32)],
        compiler_params=_cparams("parallel", "parallel", "arbitrary"),
        name="mixer_d",
    )(qt, kd, vt, p3)


def _out_kernel(x_ref, ya_ref, yb_ref, yc_ref, yd_ref, g0_ref, g1_ref, g2_ref, g3_ref,
                wa_ref, wb_ref, wc_ref, wd_ref, bg_ref, wo_ref, np_ref, o_ref):
    merged = None
    for i, (y_ref, w_ref, g_ref) in enumerate(((ya_ref, wa_ref, g0_ref), (yb_ref, wb_ref, g1_ref),
                                                (yc_ref, wc_ref, g2_ref), (yd_ref, wd_ref, g3_ref))):
        t = jnp.dot(y_ref[...], w_ref[...], preferred_element_type=F32)
        gate = _sigmoid(g_ref[...].astype(F32) + bg_ref[i:i + 1, :])
        merged = gate * t if merged is None else merged + gate * t
    o = jnp.dot(merged.astype(BF16), wo_ref[...], preferred_element_type=F32)
    ms = jnp.mean(o * o, axis=-1, keepdims=True)
    o_ref[...] = x_ref[...] + o * lax.rsqrt(ms + EPS) * np_ref[...]


def _merge_out(x2, p2, ya, yb, yc, yd, wa, wb, wc, wd, b_gate, w_out, norm_post, layer, tm):
    n = x2.shape[0]
    row = lambda width: pl.BlockSpec((tm, width), lambda i: (i, 0))
    full = lambda shape: pl.BlockSpec((None,) + shape, lambda i: (layer, 0, 0))
    gate = lambda br: pl.BlockSpec((tm, D_MODEL), lambda i, br=br: (i, U_GATE * LANE // D_MODEL + br))
    return pl.pallas_call(
        _out_kernel,
        grid=(n // tm,),
        in_specs=[row(D_MODEL), row(A_WIDTH), row(B_WIDTH), row(C_WIDTH), row(D_WIDTH),
                  gate(0), gate(1), gate(2), gate(3),
                  full((A_WIDTH, D_MODEL)), full((B_WIDTH, D_MODEL)), full((C_WIDTH, D_MODEL)),
                  full((D_WIDTH, D_MODEL)), full((N_BRANCH, D_MODEL)), full((D_MODEL, D_MODEL)),
                  full((1, D_MODEL))],
        out_specs=row(D_MODEL),
        out_shape=jax.ShapeDtypeStruct((n, D_MODEL), F32),
        compiler_params=_cparams("parallel"),
        name="merge_out",
    )(x2, ya, yb, yc, yd, p2, p2, p2, p2, wa, wb, wc, wd, b_gate, w_out, norm_post)


def _w_src_unit(g):
    return jnp.where(g < W_GATE_GROUPS, W_GATE_UNIT + W_GROUP * g,
                     jnp.where(g == W_DG_GROUP, W_DKR_UNIT,
                               jnp.where(g == W_DSMALL_GROUP, W_MAIN_UNITS, W_GROUP * (g - W_DSMALL_GROUP - 1))))


def _relayout_kernel(*refs):
    views, o_ref = refs[:W_GROUP + 1], refs[W_GROUP + 1]
    g = pl.program_id(1)
    lane = lax.broadcasted_iota(jnp.int32, (1, LANE), 1)

    def shifted(k):
        lo = pltpu.roll(views[k][...], LANE - D_ROPE, axis=1)
        hi = pltpu.roll(views[k + 1][...], LANE - D_ROPE, axis=1)
        return jnp.where(lane < LANE - D_ROPE, lo, hi)

    def put(k, val):
        o_ref[:, k * LANE:(k + 1) * LANE] = val.astype(o_ref.dtype)

    @pl.when(g < W_GATE_GROUPS)
    def _gate():
        for k in range(W_GROUP):
            put(k, shifted(k))

    @pl.when(g == W_DG_GROUP)
    def _dg():
        for k in range(D_WIDTH // LANE):
            put(k, shifted(k))
        put(W_GROUP - 1, jnp.zeros(views[0].shape, F32))

    @pl.when(g == W_DSMALL_GROUP)
    def _dsmall():
        for k in range(W_GROUP - 1):
            put(k, views[k][...])
        put(W_GROUP - 1, jnp.where(lane < D_ROPE, views[W_GROUP - 1][...], 0.0))

    @pl.when(g > W_DSMALL_GROUP)
    def _main():
        for k in range(W_GROUP):
            put(k, views[k][...])


def _relayout_w_in(w_in):
    depth, d, width = w_in.shape
    last_unit = (width - 1) // LANE
    assert (W_GATE_UNIT + N_BRANCH * D_MODEL // LANE) == last_unit and width % LANE == D_ROPE

    def view(k):
        return pl.BlockSpec((None, d, LANE), lambda l, g, k=k: (l, 0, jnp.minimum(_w_src_unit(g) + k, last_unit)))

    return pl.pallas_call(
        _relayout_kernel,
        grid=(depth, W_NGROUP),
        in_specs=[view(k) for k in range(W_GROUP + 1)],
        out_specs=pl.BlockSpec((None, d, W_GROUP * LANE), lambda l, g: (l, 0, g)),
        out_shape=jax.ShapeDtypeStruct((depth, d, P_WIDTH), BF16),
        compiler_params=_cparams("parallel", "parallel"),
        name="w_relayout",
    )(*([w_in] * (W_GROUP + 1)))


def kernel(x, norm_pre, norm_post, w_in, conv_w, conv_b, lru_wr, lru_br, lru_wi, lru_bi, lru_lambda,
           diff_lam_q1, diff_lam_k1, diff_lam_q2, diff_lam_k2, diff_subln, mla_q_norm, mla_kv_norm,
           mla_w_uq, mla_w_ukv, w_br_a, w_br_b, w_br_c, w_br_d, b_gate, w_out):
    b, s, d = x.shape
    depth = w_in.shape[0]
    n = b * s
    tm_in = min(2048, n)
    tm_out = min(512, n)
    tq = min(512, s)
    tk = min(512, s)

    w_perm = _relayout_w_in(w_in)
    out_weights = [w.astype(BF16) for w in (w_br_a, w_br_b, w_br_c, w_br_d)]
    w_out_bf = w_out.astype(BF16)
    a_bias = _a_bias_tables(jnp.asarray([2.0 ** (-8.0 * (i + 1) / A_SLOTS) for i in range(A_SLOTS)], F32))
    cos_t, sin_t = _d_tables(s)
    softplus_neg_lam = jnp.log1p(jnp.exp(-lru_lambda.astype(F32)))

    x2 = x.reshape(n, d)
    for l in range(depth):
        p2 = _inproj(x2, norm_pre[l][None, :], w_perm, l, tm_in, 512)
        p3 = p2.reshape(b, s, P_WIDTH)
        ya = _mixer_a(p3, a_bias)
        bw, bb = _b_gate_weights(lru_wr[l], lru_br[l], lru_wi[l], lru_bi[l])
        yb = _mixer_b(p3, conv_w[l], conv_b[l][None, :], bw, bb, softplus_neg_lam[l])
        lam_params = jnp.stack([diff_lam_q1[l], diff_lam_k1[l], diff_lam_q2[l], diff_lam_k2[l]])
        yc = _mixer_c(p3, lam_params, diff_subln[l][None, :], l, tq, tk)
        dw = _d_weights(mla_w_uq[l], mla_w_ukv[l])
        qd, kd, vd = _dprep(p3, mla_q_norm[l][None, :], mla_kv_norm[l][None, :], dw, cos_t, sin_t, min(1024, s))
        yd = _mixer_d(p3, qd, kd, vd, min(2 * tq, s), tk)
        x2 = _merge_out(x2, p2, ya.reshape(n, -1), yb.reshape(n, -1), yc.reshape(n, -1), yd.reshape(n, -1),
                        *out_weights, b_gate, w_out_bf, norm_post[:, None, :], l, tm_out)
    return x2.reshape(b, s, d)
```

```python
import functools
import math

import jax
import jax.numpy as jnp
from jax import lax
from jax.experimental import pallas as pl
from jax.experimental.pallas import tpu as pltpu

F32 = jnp.float32
BF16 = jnp.bfloat16

D_MODEL = 1024
EPS = 1e-6
N_BRANCH = 4

A_PATTERNS = ((128, 1), (512, 4), (2048, 16))
A_SLOTS = 6
A_HEAD_DIM = 64
A_QKV = 1152
A_WIDTH = 384
A_RADIUS = 64
A_QBLK = 128
A_KWIN = 256
A_UNROLL = 8

B_WIDTH = 384
B_BLOCK_DIM = 64
B_C = 8.0

C_HEADS = 4
C_HALF = 64
C_VDIM = 128
C_QK = 512
C_WIDTH = 512

D_HEADS = 6
D_NOPE = 64
D_ROPE = 32
D_VDIM = 64
D_QLR = 256
D_KVLR = 128
D_WIDTH = 384
D_VROWS = D_VDIM + 16
ROPE_BASE = 10000.0

LANE = 128
SUBLANE = 8
NEG = -1e30
LOG2E = math.log2(math.e)
_NT = (((1,), (1,)), ((), ()))
VMEM_LIMIT = 56 * 1024 * 1024

IN_TN = 512
IN_MAIN_COLS = 3 * A_QKV + A_WIDTH + 2 * B_WIDTH + 2 * C_QK + 2 * C_WIDTH + D_QLR + D_KVLR
IN_DG_COL = IN_MAIN_COLS + D_ROPE
IN_GATE_COL = IN_DG_COL + D_WIDTH
IN_TAIL_TILES = (N_BRANCH * D_MODEL + D_WIDTH + LANE) // IN_TN
IN_MAIN_TILES = -(-(IN_MAIN_COLS + D_ROPE) // IN_TN)
U_GATE, U_DG = 0, N_BRANCH * D_MODEL // LANE
U_MAIN = IN_TAIL_TILES * IN_TN // LANE
U_AQ, U_AK, U_AV, U_AG = U_MAIN, U_MAIN + 9, U_MAIN + 18, U_MAIN + 27
U_BX, U_BG = U_MAIN + 30, U_MAIN + 33
U_CQ, U_CK, U_CV, U_CG = U_MAIN + 36, U_MAIN + 40, U_MAIN + 44, U_MAIN + 48
U_DCQ, U_DCKV, U_DKR = U_MAIN + 52, U_MAIN + 54, U_MAIN + 55
U_TOTAL = U_MAIN + IN_MAIN_TILES * IN_TN // LANE
P_WIDTH = U_TOTAL * LANE
assert U_TOTAL == 92 and U_DCQ % 2 == 0 and IN_GATE_COL + N_BRANCH * D_MODEL == 11552


def _cparams(*sem):
    return pltpu.CompilerParams(dimension_semantics=sem, vmem_limit_bytes=VMEM_LIMIT)


def _silu(x):
    return x * (1.0 / (1.0 + jnp.exp(-x)))


def _sigmoid(x):
    return 1.0 / (1.0 + jnp.exp(-x))


def _inproj_kernel(x_ref, g_ref, wtail_ref, wmain_ref, o_ref, h_ref):
    j = pl.program_id(1)

    @pl.when(j == 0)
    def _():
        x = x_ref[...]
        ms = jnp.mean(x * x, axis=-1, keepdims=True)
        h_ref[...] = (x * lax.rsqrt(ms + EPS) * g_ref[...]).astype(BF16)

    @pl.when(j < IN_TAIL_TILES)
    def _():
        o_ref[...] = lax.dot_general(h_ref[...], wtail_ref[...], _NT,
                                     preferred_element_type=F32).astype(o_ref.dtype)

    @pl.when(j >= IN_TAIL_TILES)
    def _():
        o_ref[...] = lax.dot_general(h_ref[...], wmain_ref[...].astype(BF16), _NT,
                                     preferred_element_type=F32).astype(o_ref.dtype)


def _in_tail_weights(w_in_t):
    gate = w_in_t[:, IN_GATE_COL:IN_GATE_COL + N_BRANCH * D_MODEL]
    dg = w_in_t[:, IN_DG_COL:IN_GATE_COL]
    pad = jnp.zeros((w_in_t.shape[0], IN_TAIL_TILES * IN_TN - gate.shape[1] - dg.shape[1], D_MODEL), w_in_t.dtype)
    return jnp.concatenate([gate, dg, pad], axis=1).astype(BF16)


def _inproj(x2, gain, w_tail, w_in_t, layer, tm):
    n = x2.shape[0]
    return pl.pallas_call(
        _inproj_kernel,
        grid=(n // tm, P_WIDTH // IN_TN),
        in_specs=[pl.BlockSpec((tm, D_MODEL), lambda i, j: (i, 0)),
                  pl.BlockSpec((1, D_MODEL), lambda i, j: (0, 0)),
                  pl.BlockSpec((None, IN_TN, D_MODEL),
                               lambda i, j: (layer, jnp.minimum(j, IN_TAIL_TILES - 1), 0)),
                  pl.BlockSpec((None, IN_TN, D_MODEL),
                               lambda i, j: (layer, jnp.maximum(j - IN_TAIL_TILES, 0), 0))],
        out_specs=pl.BlockSpec((tm, IN_TN), lambda i, j: (i, j)),
        out_shape=jax.ShapeDtypeStruct((n, P_WIDTH), BF16),
        scratch_shapes=[pltpu.VMEM((tm, D_MODEL), BF16)],
        compiler_params=_cparams("parallel", "arbitrary"),
        name="inproj",
    )(x2, gain, w_tail, w_in_t)


def _a_bias_tables(slopes):
    ii = jnp.arange(A_QBLK, dtype=jnp.int32)[:, None]
    jj = jnp.arange(A_KWIN, dtype=jnp.int32)[None, :]
    out = []
    for _, dil in A_PATTERNS:
        per_edge = []
        for off in (0, A_RADIUS, A_QBLK):
            rel = jnp.abs(off + ii - jj)
            dist = (rel * dil).astype(F32)
            b = -slopes[:, None, None] * dist[None] * LOG2E
            per_edge.append(jnp.where((rel <= A_RADIUS)[None], b, NEG))
        out.append(jnp.stack(per_edge))
    return jnp.stack(out)


def _a_pitch(dil):
    return dil + SUBLANE if dil % (2 * SUBLANE) == 0 else dil


def _a_kernel(q0, k0, v0, q1, k1, v1, q2, k2, v2, gate_ref, bias_ref, o_ref,
              qf, kf, vf, u_ref, z_ref, m_ref, up_ref, zp_ref, mp_ref):
    s_len = q0.shape[0]
    lane = lax.broadcasted_iota(jnp.int32, (1, LANE), 1)
    first = lane < A_HEAD_DIM
    ones_first = jnp.where(first, 1.0, 0.0).astype(BF16)
    ones_second = jnp.where(first, 0.0, 1.0).astype(BF16)
    scale = A_HEAD_DIM ** -0.5 * LOG2E
    padded_groups = [g for g, (_, dil) in enumerate(A_PATTERNS) if _a_pitch(dil) != dil]
    assert padded_groups == [len(A_PATTERNS) - 1]

    for g, ((_, dil), (qr, kr, vr)) in enumerate(zip(A_PATTERNS, ((q0, k0, v0), (q1, k1, v1), (q2, k2, v2)))):
        sub_len = s_len // dil
        nqb = sub_len // A_QBLK
        pitch = _a_pitch(dil)
        if pitch != dil:
            def stage(l2, carry, dil=dil, pitch=pitch, qr=qr, kr=kr, vr=vr):
                src = pl.ds(pl.multiple_of(l2 * 2 * dil, 2 * dil), 2 * dil)
                dst = pl.multiple_of(l2 * 2 * pitch, SUBLANE)
                for ref, buf, mul in ((qr, qf, scale), (kr, kf, None), (vr, vf, None)):
                    x = ref[src, :].astype(F32)
                    x = x if mul is None else x * mul
                    buf[pl.ds(dst, dil), :] = x[:dil]
                    buf[pl.ds(dst + pitch, dil), :] = x[dil:]
                return carry

            lax.fori_loop(0, sub_len // 2, stage, 0, unroll=4)
        elif dil > 1:
            rows = pl.ds(0, s_len)
            qf[rows, :] = qr[...].astype(F32) * scale
            kf[rows, :] = kr[...].astype(F32)
            vf[rows, :] = vr[...].astype(F32)

        def block(idx, carry, g=g, dil=dil, pitch=pitch, sub_len=sub_len, nqb=nqb, qr=qr, kr=kr, vr=vr):
            r = idx // nqb
            qb = idx % nqb
            qs = qb * A_QBLK
            ws = jnp.clip(qs - A_RADIUS, 0, sub_len - A_KWIN)
            edge = jnp.where(qb == 0, 0, jnp.where(qb == nqb - 1, 2, 1))
            if dil == 1:
                qrows = pl.ds(pl.multiple_of(qs, A_QBLK), A_QBLK)
                krows = pl.ds(pl.multiple_of(ws, A_RADIUS), A_KWIN)
                q = (qr[qrows, :].astype(F32) * scale).astype(BF16)
                k = kr[krows, :]
                v = vr[krows, :]
            else:
                qrows = pl.ds(r + qs * pitch, A_QBLK, stride=pitch)
                krows = pl.ds(r + ws * pitch, A_KWIN, stride=pitch)
                q = qf[qrows, :].astype(BF16)
                k = kf[krows, :].astype(BF16)
                v = vf[krows, :].astype(BF16)
            zq = jnp.zeros_like(q)
            zv = jnp.zeros_like(v)
            uz = None
            ms = []
            for h, head_lanes in enumerate((first, jnp.logical_not(first))):
                qh = jnp.where(head_lanes, q, zq)
                s = lax.dot_general(qh, k, (((1,), (1,)), ((), ())), preferred_element_type=F32)
                s = s + bias_ref[g, edge, h]
                mh = jnp.max(s, axis=-1, keepdims=True)
                p = jnp.exp2(s - mh).astype(BF16)
                ones_h = ones_first if h == 0 else ones_second
                vaug = jnp.concatenate([jnp.where(head_lanes, v, zv),
                                        jnp.broadcast_to(ones_h, v.shape)], axis=1)
                part = jnp.dot(p, vaug, preferred_element_type=F32)
                uz = part if uz is None else uz + part
                ms.append(mh)
            u = uz[:, :LANE]
            z = uz[:, LANE:]
            m = jnp.where(first, ms[0], ms[1])
            if g == 0:
                u_ref[qrows, :] = u
                z_ref[qrows, :] = z
                m_ref[qrows, :] = m
            elif pitch != dil:
                up_ref[qrows, :] = u
                zp_ref[qrows, :] = z
                mp_ref[qrows, :] = m
            else:
                m_old = m_ref[qrows, :]
                m_new = jnp.maximum(m_old, m)
                a = jnp.exp2(m_old - m_new)
                b = jnp.exp2(m - m_new)
                u_ref[qrows, :] = a * u_ref[qrows, :] + b * u
                z_ref[qrows, :] = a * z_ref[qrows, :] + b * z
                m_ref[qrows, :] = m_new
            return carry

        lax.fori_loop(0, dil * nqb, block, 0, unroll=A_UNROLL)

    dil = A_PATTERNS[padded_groups[0]][1]
    pitch = _a_pitch(dil)

    def finish(l2, carry):
        rows = pl.ds(pl.multiple_of(l2 * 2 * dil, 2 * dil), 2 * dil)
        src = pl.multiple_of(l2 * 2 * pitch, SUBLANE)

        def padded(ref):
            return jnp.concatenate([ref[pl.ds(src, dil), :], ref[pl.ds(src + pitch, dil), :]], axis=0)

        m_a, m_b = m_ref[rows, :], padded(mp_ref)
        m_new = jnp.maximum(m_a, m_b)
        a = jnp.exp2(m_a - m_new)
        b = jnp.exp2(m_b - m_new)
        u = a * u_ref[rows, :] + b * padded(up_ref)
        z = a * z_ref[rows, :] + b * padded(zp_ref)
        o_ref[rows, :] = (u / z * _silu(gate_ref[rows, :].astype(F32))).astype(o_ref.dtype)
        return carry

    lax.fori_loop(0, s_len // (2 * dil), finish, 0, unroll=4)


def _mixer_a(p3, bias):
    b, s, _ = p3.shape
    npair = A_SLOTS // 2
    assert s // A_PATTERNS[-1][1] >= A_KWIN
    padded_rows = max(s // dil * _a_pitch(dil) for _, dil in A_PATTERNS)

    def col(unit):
        return pl.BlockSpec((None, s, LANE), lambda bi, hp, unit=unit: (bi, 0, unit + hp))

    in_specs = []
    for g in range(len(A_PATTERNS)):
        for base in (U_AQ, U_AK, U_AV):
            in_specs.append(col(base + g * npair))
    in_specs.append(col(U_AG))
    in_specs.append(pl.BlockSpec((len(A_PATTERNS), 3, 2, A_QBLK, A_KWIN), lambda bi, hp: (0, 0, hp, 0, 0)))
    return pl.pallas_call(
        _a_kernel,
        grid=(b, npair),
        in_specs=in_specs,
        out_specs=pl.BlockSpec((None, s, LANE), lambda bi, hp: (bi, 0, hp)),
        out_shape=jax.ShapeDtypeStruct((b, s, A_WIDTH), BF16),
        scratch_shapes=([pltpu.VMEM((padded_rows, LANE), F32)] * 3 + [pltpu.VMEM((s, LANE), F32)] * 3
                        + [pltpu.VMEM((padded_rows, LANE), F32)] * 3),
        compiler_params=_cparams("parallel", "parallel"),
        name="mixer_a",
    )(*([p3] * 10), bias)


B_CHUNK = 256
B_PAD = 8


def _b_kernel(x_ref, g_ref, cw_ref, cb_ref, w_ref, bias_ref, sp_ref, o_ref, xs_ref, a_ref, h_ref, c_ref):
    s_len = x_ref.shape[0]
    nchunk = s_len // B_CHUNK
    ntile = s_len // SUBLANE
    zpad = jnp.zeros((B_PAD, LANE), F32)
    xs_ref[pl.ds(0, B_PAD), :] = zpad
    xs_ref[pl.ds(B_PAD + s_len, B_PAD), :] = zpad
    xs_ref[pl.ds(B_PAD, s_len), :] = x_ref[...].astype(F32)
    row_in_tile = lax.broadcasted_iota(jnp.int32, (1, SUBLANE, 1), 1)
    tile_idx = lax.broadcasted_iota(jnp.int32, (ntile, 1), 0)

    def scan_rows(acc_a, acc_h, axis, shifts, pos, reverse):
        n = acc_a.shape[axis]
        for sh in shifts:
            if reverse:
                sa = pltpu.roll(acc_a, n - sh, axis=axis)
                shh = pltpu.roll(acc_h, n - sh, axis=axis)
                keep = pos < n - sh
            else:
                sa = pltpu.roll(acc_a, sh, axis=axis)
                shh = pltpu.roll(acc_h, sh, axis=axis)
                keep = pos >= sh
            acc_h = acc_h + acc_a * jnp.where(keep, shh, 0.0)
            acc_a = acc_a * jnp.where(keep, sa, 1.0)
        return acc_a, acc_h

    tile_shifts = (1, 2, 4)
    summary_shifts = tuple(1 << e for e in range(ntile.bit_length() - 1))
    assert 1 << len(summary_shifts) == ntile

    def local_scan(c, carry):
        t0 = pl.multiple_of(c * B_CHUNK, B_CHUNK)
        win = xs_ref[pl.ds(t0, B_CHUNK + 2 * B_PAD), :]
        n = B_CHUNK + 2 * B_PAD
        xc = (cb_ref[...]
              + pltpu.roll(win, 1, axis=0) * cw_ref[0:1, :]
              + win * cw_ref[1:2, :]
              + pltpu.roll(win, n - 1, axis=0) * cw_ref[2:3, :]
              + pltpu.roll(win, n - 2, axis=0) * cw_ref[3:4, :])
        xc = xc[B_PAD:B_PAD + B_CHUNK, :]
        ri = jnp.dot(xc.astype(BF16), w_ref[...], preferred_element_type=F32) + bias_ref[...]
        rows = pl.ds(t0, B_CHUNK)
        for direction in (0, 1):
            r = _sigmoid(ri[:, (2 * direction) * LANE:(2 * direction + 1) * LANE])
            i = _sigmoid(ri[:, (2 * direction + 1) * LANE:(2 * direction + 2) * LANE])
            log_a = (-B_C) * r * sp_ref[direction:direction + 1, :]
            a = jnp.exp(log_a)
            u = jnp.sqrt(-jnp.tanh(log_a) * (a * a + 1.0)) * (i * xc)
            tiles = (B_CHUNK // SUBLANE, SUBLANE, LANE)
            acc_a, acc_h = scan_rows(a.reshape(tiles), u.reshape(tiles), 1, tile_shifts, row_in_tile,
                                     direction == 1)
            a_ref[direction, rows, :] = acc_a.reshape(B_CHUNK, LANE)
            h_ref[direction, rows, :] = acc_h.reshape(B_CHUNK, LANE)
        return carry

    lax.fori_loop(0, nchunk, local_scan, 0)

    for direction in (0, 1):
        reverse = direction == 1
        edge = 0 if reverse else SUBLANE - 1
        sum_a = a_ref[direction, pl.ds(edge, ntile, stride=SUBLANE), :]
        sum_h = h_ref[direction, pl.ds(edge, ntile, stride=SUBLANE), :]
        _, inc_h = scan_rows(sum_a, sum_h, 0, summary_shifts, tile_idx, reverse)
        if reverse:
            state_in = jnp.where(tile_idx < ntile - 1, pltpu.roll(inc_h, ntile - 1, axis=0), 0.0)
        else:
            state_in = jnp.where(tile_idx >= 1, pltpu.roll(inc_h, 1, axis=0), 0.0)
        for r in range(SUBLANE):
            c_ref[direction, pl.ds(r, ntile, stride=SUBLANE), :] = state_in

    def apply_state(c, carry):
        rows = pl.ds(pl.multiple_of(c * B_CHUNK, B_CHUNK), B_CHUNK)
        h = (h_ref[0, rows, :] + a_ref[0, rows, :] * c_ref[0, rows, :]
             + h_ref[1, rows, :] + a_ref[1, rows, :] * c_ref[1, rows, :])
        o_ref[rows, :] = (h * _silu(g_ref[rows, :].astype(F32))).astype(o_ref.dtype)
        return carry

    lax.fori_loop(0, nchunk, apply_state, 0)


def _mixer_b(p3, conv_w, conv_b, w_gate, b_gate, softplus_neg_lam):
    b, s, _ = p3.shape
    ngrp = B_WIDTH // LANE
    assert s % B_CHUNK == 0
    return pl.pallas_call(
        _b_kernel,
        grid=(b, ngrp),
        in_specs=[pl.BlockSpec((None, s, LANE), lambda bi, j: (bi, 0, U_BX + j)),
                  pl.BlockSpec((None, s, LANE), lambda bi, j: (bi, 0, U_BG + j)),
                  pl.BlockSpec((4, LANE), lambda bi, j: (0, j)),
                  pl.BlockSpec((1, LANE), lambda bi, j: (0, j)),
                  pl.BlockSpec((None, LANE, 4 * LANE), lambda bi, j: (j, 0, 0)),
                  pl.BlockSpec((None, 1, 4 * LANE), lambda bi, j: (j, 0, 0)),
                  pl.BlockSpec((2, LANE), lambda bi, j: (0, j))],
        out_specs=pl.BlockSpec((None, s, LANE), lambda bi, j: (bi, 0, j)),
        out_shape=jax.ShapeDtypeStruct((b, s, B_WIDTH), BF16),
        scratch_shapes=[pltpu.VMEM((s + 2 * B_PAD, LANE), F32)] + [pltpu.VMEM((2, s, LANE), F32)] * 3,
        compiler_params=_cparams("parallel", "parallel"),
        name="mixer_b",
    )(p3, p3, conv_w, conv_b, w_gate, b_gate, softplus_neg_lam)


def _b_gate_weights(w_r, b_r, w_i, b_i):
    per = LANE // B_BLOCK_DIM
    ngrp = B_WIDTH // LANE

    def blockdiag(w):
        w = w.reshape(2, ngrp, per, B_BLOCK_DIM, B_BLOCK_DIM)
        eye = jnp.eye(per, dtype=w.dtype)
        full = jnp.einsum("dgpcx,pq->dgpcqx", w, eye).reshape(2, ngrp, LANE, LANE)
        return full.transpose(1, 0, 2, 3)

    w = jnp.concatenate([blockdiag(w_r), blockdiag(w_i)], axis=-1)
    w = w.transpose(0, 2, 1, 3).reshape(ngrp, LANE, 4 * LANE).astype(BF16)
    bias = jnp.concatenate([b_r.reshape(2, ngrp, LANE), b_i.reshape(2, ngrp, LANE)], axis=-1)
    return w, bias.transpose(1, 0, 2).reshape(ngrp, 1, 4 * LANE)


ATT_STATIC_UNROLL = False


def _attend_t(streams, tk, nk, s_ref, mt_ref, m_ref, acc_ref, tile_of=None):
    assert nk % 2 == 0 and nk >= 2
    if tile_of is None:
        tile_of = lambda u: u
    for i in range(len(streams)):
        m_ref[i] = jnp.full(m_ref.shape[1:], NEG, F32)
        acc_ref[i] = jnp.zeros(acc_ref.shape[1:], F32)

    def key_rows(u):
        return pl.ds(pl.multiple_of(tile_of(u) * tk, tk), tk)

    def scores(u, slot):
        rows = key_rows(u)
        for i, st in enumerate(streams):
            s = jnp.dot(st["k_ref"][rows, :], st["q"](u), preferred_element_type=F32)
            s_ref[slot, i] = s
            mt_ref[slot, i] = jnp.max(s, axis=0, keepdims=True)

    def accumulate(u, slot, first=False):
        rows = key_rows(u)
        for i, st in enumerate(streams):
            s = s_ref[slot, i]
            if first and st.get("bias0") is not None:
                s = s + st["bias0"][...]
                m_tile = jnp.max(s, axis=0, keepdims=True)
            else:
                m_tile = mt_ref[slot, i]
            off = st["off"](u) if st.get("off") is not None else None
            m_old = m_ref[i]
            m_new = jnp.maximum(m_old, m_tile if off is None else m_tile + off)
            alpha = jnp.exp2(m_old - m_new)
            p = jnp.exp2(s - (m_new if off is None else m_new - off)).astype(BF16)
            acc_ref[i] = alpha * acc_ref[i] + jnp.dot(st["vt_ref"][:, rows], p, preferred_element_type=F32)
            m_ref[i] = m_new

    scores(0, 0)
    scores(1, 1)
    accumulate(0, 0, first=True)

    def body(j, carry):
        u = 1 + 2 * j
        scores(u + 1, 0)
        accumulate(u, 1)
        scores(u + 2, 1)
        accumulate(u + 1, 0)
        return carry

    if ATT_STATIC_UNROLL:
        for j in range((nk - 2) // 2):
            body(j, 0)
    else:
        lax.fori_loop(0, (nk - 2) // 2, body, 0)
    accumulate(nk - 1, 1)


C_AUG = 3
C_VROWS = C_VDIM + 16
C_MAPS = ((0, C_HALF), (1, 0))


def _c_kernel(q_ref, k_ref, v_ref, g_ref, lam_ref, subln_ref, o_ref,
              kaug_ref, vt_ref, dbias_ref, s_ref, mt_ref, m_ref, acc_ref, *, tk, slopes, lam_init):
    h = pl.program_id(1)
    qi = pl.program_id(2)
    tq = q_ref.shape[0]
    s_len = k_ref.shape[0]
    nk = s_len // tk
    assert tq == tk
    lp = lam_ref[...]
    lam = (jnp.exp(jnp.sum(lp[0:1, :] * lp[1:2, :], axis=-1, keepdims=True))
           - jnp.exp(jnp.sum(lp[2:3, :] * lp[3:4, :], axis=-1, keepdims=True)) + lam_init)
    slope = jnp.where(h == 0, slopes[0], jnp.where(h == 1, slopes[1], jnp.where(h == 2, slopes[2], slopes[3])))
    c = slope.astype(F32) * LOG2E

    @pl.when(qi == 0)
    def _build_key_side():
        lane = lax.broadcasted_iota(jnp.int32, (1, LANE), 1)
        for ch in range(nk):
            rows = slice(ch * tk, (ch + 1) * tk)
            k = k_ref[rows, :]
            jpos = (ch * tk + lax.broadcasted_iota(jnp.int32, (tk, 1), 0)).astype(F32)
            cj = c * jpos
            hi = cj.astype(BF16).astype(F32)
            mid = (cj - hi).astype(BF16).astype(F32)
            lo = (cj - hi - mid).astype(BF16).astype(F32)
            for cmap, base in C_MAPS:
                aug = jnp.where(lane == base, hi, jnp.where(lane == base + 1, mid,
                                                            jnp.where(lane == base + 2, lo, 0.0)))
                keep = (lane < C_HALF) if cmap == 0 else (lane >= C_HALF)
                kaug_ref[cmap, rows, :] = jnp.where(keep, k, aug.astype(BF16))
            vt_ref[0:C_VDIM, rows] = v_ref[rows, :].astype(F32).T.astype(BF16)
        r16 = lax.broadcasted_iota(jnp.int32, (C_VROWS - C_VDIM, s_len), 0)
        vt_ref[C_VDIM:C_VROWS, :] = jnp.where(r16 == 0, 1.0, 0.0).astype(BF16)
        ii = lax.broadcasted_iota(jnp.int32, (tk, tq), 1)
        jj = lax.broadcasted_iota(jnp.int32, (tk, tq), 0)
        dbias_ref[...] = -c * jnp.abs(ii - jj).astype(F32)

    def tile_of(u):
        if isinstance(u, int) and u == 0:
            return qi
        return u - 1 + (u - 1 >= qi).astype(jnp.int32)

    def sign_of(u):
        return jnp.where(tile_of(u) < qi, 1.0, -1.0).astype(F32)

    q_t = (q_ref[...].astype(F32) * (C_HALF ** -0.5 * LOG2E)).T
    row = lax.broadcasted_iota(jnp.int32, (LANE, 1), 0)
    ci = c * (qi * tq + lax.broadcasted_iota(jnp.int32, (1, tq), 1)).astype(F32)
    streams = []
    for cmap, base in C_MAPS:
        keep = (row < C_HALF) if cmap == 0 else (row >= C_HALF)
        q_base = jnp.where(keep, q_t, 0.0)
        aug_rows = jnp.where((row >= base) & (row < base + C_AUG), 1.0, 0.0)
        q_diag = q_base.astype(BF16)
        q_before = (q_base + aug_rows).astype(BF16)
        q_after = (q_base - aug_rows).astype(BF16)

        def q_of(u, q_diag=q_diag, q_before=q_before, q_after=q_after):
            if isinstance(u, int) and u == 0:
                return q_diag
            return jnp.where(tile_of(u) < qi, q_before, q_after)

        def off_of(u):
            if isinstance(u, int) and u == 0:
                return None
            return -sign_of(u) * ci

        streams.append(dict(q=q_of, k_ref=kaug_ref.at[cmap], vt_ref=vt_ref, off=off_of, bias0=dbias_ref))
    _attend_t(streams, tk, nk, s_ref, mt_ref, m_ref, acc_ref, tile_of)

    o_t = []
    for cmap in range(2):
        acc = acc_ref[cmap]
        o_t.append(acc[:C_VDIM, :] / acc[C_VDIM:C_VDIM + 1, :])
    o = (o_t[0] - lam * o_t[1]).T
    ms = jnp.mean(o * o, axis=-1, keepdims=True)
    o = o * lax.rsqrt(ms + EPS) * subln_ref[...] * (1.0 - lam_init)
    o_ref[...] = (o * _silu(g_ref[...].astype(F32))).astype(o_ref.dtype)


def _mixer_c(p3, lam_params, subln, layer, tq, tk):
    b, s, _ = p3.shape
    lam_init = 0.8 - 0.6 * math.exp(-0.3 * layer)
    slopes = tuple(2.0 ** (-8.0 * (i + 1) / C_HEADS) for i in range(C_HEADS))
    kern = functools.partial(_c_kernel, tk=tk, slopes=slopes, lam_init=lam_init)
    return pl.pallas_call(
        kern,
        grid=(b, C_HEADS, s // tq),
        in_specs=[pl.BlockSpec((None, tq, LANE), lambda bi, h, i: (bi, i, U_CQ + h)),
                  pl.BlockSpec((None, s, LANE), lambda bi, h, i: (bi, 0, U_CK + h)),
                  pl.BlockSpec((None, s, LANE), lambda bi, h, i: (bi, 0, U_CV + h)),
                  pl.BlockSpec((None, tq, LANE), lambda bi, h, i: (bi, i, U_CG + h)),
                  pl.BlockSpec((4, C_HALF), lambda bi, h, i: (0, 0)),
                  pl.BlockSpec((1, C_VDIM), lambda bi, h, i: (0, 0))],
        out_specs=pl.BlockSpec((None, tq, LANE), lambda bi, h, i: (bi, i, h)),
        out_shape=jax.ShapeDtypeStruct((b, s, C_WIDTH), BF16),
        scratch_shapes=[pltpu.VMEM((2, s, LANE), BF16), pltpu.VMEM((C_VROWS, s), BF16),
                        pltpu.VMEM((tk, tq), F32), pltpu.VMEM((2, 2, tk, tq), F32),
                        pltpu.VMEM((2, 2, 1, tq), F32), pltpu.VMEM((2, 1, tq), F32),
                        pltpu.VMEM((2, C_VROWS, tq), F32)],
        compiler_params=_cparams("parallel", "parallel", "arbitrary"),
        name="mixer_c",
    )(p3, p3, p3, p3, lam_params, subln)


def _dprep_kernel(cq_ref, ckv_ref, kr_ref, qn_ref, kvn_ref, wqt_ref, wqst_ref, wk_ref, wvt_ref,
                  e1_ref, e2_ref, cos_ref, sin_ref, cost_ref, sint_ref, vonet_ref, qt_out, k_out, vt_out):
    def norm(x_ref, gain_ref):
        x = x_ref[...].astype(F32)
        ms = jnp.mean(x * x, axis=-1, keepdims=True)
        return (x * lax.rsqrt(ms + EPS) * gain_ref[...]).astype(BF16)

    cqn = norm(cq_ref, qn_ref)
    ckvn = norm(ckv_ref, kvn_ref)
    kr = kr_ref[...]
    cos = jnp.concatenate([cos_ref[...]] * D_HEADS, axis=1)
    sin = jnp.concatenate([sin_ref[...]] * D_HEADS, axis=1)
    cos_t = jnp.concatenate([cost_ref[...]] * D_HEADS, axis=0)
    sin_t = jnp.concatenate([sint_ref[...]] * D_HEADS, axis=0)
    q_t = (lax.dot_general(wqt_ref[...], cqn, _NT, preferred_element_type=F32) * cos_t
           + lax.dot_general(wqst_ref[...], cqn, _NT, preferred_element_type=F32) * sin_t)
    qt_out[...] = (q_t * ((D_NOPE + D_ROPE) ** -0.5 * LOG2E)).astype(qt_out.dtype)
    k = (jnp.dot(ckvn, wk_ref[...], preferred_element_type=F32)
         + jnp.dot(kr, e1_ref[...], preferred_element_type=F32) * cos
         + jnp.dot(kr, e2_ref[...], preferred_element_type=F32) * sin)
    k_out[...] = k.astype(k_out.dtype)
    v_t = lax.dot_general(wvt_ref[...], ckvn, _NT, preferred_element_type=F32) + vonet_ref[...]
    vt_out[...] = v_t.astype(vt_out.dtype)


def _d_tables(s):
    inv = ROPE_BASE ** (-jnp.arange(0, D_ROPE, 2, dtype=F32) / D_ROPE)
    ang = jnp.arange(s, dtype=F32)[:, None] * inv[None, :]
    cos, sin = jnp.cos(ang), jnp.sin(ang)
    pad = LANE - D_NOPE - D_ROPE
    cos_t = jnp.concatenate([jnp.ones((s, D_NOPE), F32), cos, cos, jnp.zeros((s, pad), F32)], axis=1)
    sin_t = jnp.concatenate([jnp.zeros((s, D_NOPE), F32), -sin, sin, jnp.zeros((s, pad), F32)], axis=1)
    return cos_t, sin_t


def _d_weights(w_uq, w_ukv):
    half = D_ROPE // 2
    pad = LANE - D_NOPE - D_ROPE
    wq = w_uq.reshape(D_QLR, D_HEADS, D_NOPE + D_ROPE)
    zq = jnp.zeros((D_QLR, D_HEADS, pad), w_uq.dtype)
    wq_main = jnp.concatenate([wq, zq], axis=-1).reshape(D_QLR, D_HEADS * LANE)
    wq_swap = jnp.concatenate([jnp.zeros((D_QLR, D_HEADS, D_NOPE), w_uq.dtype),
                               wq[..., D_NOPE + half:], wq[..., D_NOPE:D_NOPE + half], zq],
                              axis=-1).reshape(D_QLR, D_HEADS * LANE)
    wkv = w_ukv.reshape(D_KVLR, D_HEADS, D_NOPE + D_VDIM)
    zk = jnp.zeros((D_KVLR, D_HEADS, LANE - D_NOPE), w_ukv.dtype)
    wk = jnp.concatenate([wkv[..., :D_NOPE], zk], axis=-1).reshape(D_KVLR, D_HEADS * LANE)
    wv = jnp.concatenate([wkv[..., D_NOPE:], jnp.zeros((D_KVLR, D_HEADS, D_VROWS - D_VDIM), w_ukv.dtype)],
                         axis=-1).reshape(D_KVLR, D_HEADS * D_VROWS)
    src = jnp.arange(D_ROPE)
    e1 = jnp.zeros((LANE, LANE), F32).at[src, D_NOPE + src].set(1.0)
    e2 = jnp.zeros((LANE, LANE), F32).at[(src + half) % D_ROPE, D_NOPE + src].set(1.0)
    e1 = jnp.tile(e1, (1, D_HEADS))
    e2 = jnp.tile(e2, (1, D_HEADS))
    vone = jnp.zeros((D_VROWS, 1), F32).at[D_VDIM, 0].set(1.0)
    vone = jnp.tile(vone, (D_HEADS, 1))
    return (wq_main.T.astype(BF16), wq_swap.T.astype(BF16), wk.astype(BF16), wv.T.astype(BF16),
            e1.astype(BF16), e2.astype(BF16), vone)


def _dprep(p3, q_norm, kv_norm, dw, cos_t, sin_t, tm):
    b, s, _ = p3.shape
    wqt, wqst, wk, wvt, e1, e2, vonet = dw
    wide = D_HEADS * LANE
    vwide = D_HEADS * D_VROWS
    full = lambda shape: pl.BlockSpec(shape, lambda bi, i: (0,) * len(shape))
    tok_major = pl.BlockSpec((None, tm, wide), lambda bi, i: (bi, i, 0))
    feat_major = lambda rows: pl.BlockSpec((None, rows, tm), lambda bi, i: (bi, 0, i))
    return pl.pallas_call(
        _dprep_kernel,
        grid=(b, s // tm),
        in_specs=[pl.BlockSpec((None, tm, D_QLR), lambda bi, i: (bi, i, U_DCQ // 2)),
                  pl.BlockSpec((None, tm, LANE), lambda bi, i: (bi, i, U_DCKV)),
                  pl.BlockSpec((None, tm, LANE), lambda bi, i: (bi, i, U_DKR)),
                  full((1, D_QLR)), full((1, D_KVLR)),
                  full((wide, D_QLR)), full((wide, D_QLR)), full((D_KVLR, wide)), full((vwide, D_KVLR)),
                  full((LANE, wide)), full((LANE, wide)),
                  pl.BlockSpec((tm, LANE), lambda bi, i: (i, 0)),
                  pl.BlockSpec((tm, LANE), lambda bi, i: (i, 0)),
                  pl.BlockSpec((LANE, tm), lambda bi, i: (0, i)),
                  pl.BlockSpec((LANE, tm), lambda bi, i: (0, i)),
                  full((vwide, 1))],
        out_specs=[feat_major(wide), tok_major, feat_major(vwide)],
        out_shape=[jax.ShapeDtypeStruct((b, wide, s), BF16), jax.ShapeDtypeStruct((b, s, wide), BF16),
                   jax.ShapeDtypeStruct((b, vwide, s), BF16)],
        compiler_params=_cparams("parallel", "parallel"),
        name="mixer_d_prep",
    )(p3, p3, p3, q_norm, kv_norm, wqt, wqst, wk, wvt, e1, e2, cos_t, sin_t, cos_t.T, sin_t.T, vonet)


def _d_kernel(qt_ref, k_ref, vt_ref, g_ref, o_ref, s_ref, mt_ref, m_ref, acc_ref, *, tk):
    streams = []
    for h in range(2):
        sl = slice(h * LANE, (h + 1) * LANE)
        vrows = slice(h * D_VROWS, (h + 1) * D_VROWS)
        streams.append(dict(q=lambda u, q=qt_ref[sl, :]: q, k_ref=k_ref.at[:, sl], vt_ref=vt_ref.at[vrows, :]))
    _attend_t(streams, tk, k_ref.shape[0] // tk, s_ref, mt_ref, m_ref, acc_ref)
    parts = []
    for h in range(2):
        acc = acc_ref[h]
        parts.append(acc[:D_VDIM, :] / acc[D_VDIM:D_VDIM + 1, :])
    o = jnp.concatenate(parts, axis=0).T
    o_ref[...] = (o * _silu(g_ref[...].astype(F32))).astype(o_ref.dtype)


def _mixer_d(p3, qt, kd, vt, tq, tk):
    b, s, _ = p3.shape
    npair = D_HEADS // 2
    return pl.pallas_call(
        functools.partial(_d_kernel, tk=tk),
        grid=(b, npair, s // tq),
        in_specs=[pl.BlockSpec((None, 2 * LANE, tq), lambda bi, hp, i: (bi, hp, i)),
                  pl.BlockSpec((None, s, 2 * LANE), lambda bi, hp, i: (bi, 0, hp)),
                  pl.BlockSpec((None, 2 * D_VROWS, s), lambda bi, hp, i: (bi, hp, 0)),
                  pl.BlockSpec((None, tq, LANE), lambda bi, hp, i: (bi, i, U_DG + hp))],
        out_specs=pl.BlockSpec((None, tq, LANE), lambda bi, hp, i: (bi, i, hp)),
        out_shape=jax.ShapeDtypeStruct((b, s, D_WIDTH), BF16),
        scratch_shapes=[pltpu.VMEM((2, 2, tk, tq), F32), pltpu.VMEM((2, 2, 1, tq), F32),
                        pltpu.VMEM((2, 1, tq), F32), pltpu.VMEM((2, D_VROWS, tq), F32)],
        compiler_params=_cparams("parallel", "parallel", "arbitrary"),
        name="mixer_d",
    )(qt, kd, vt, p3)


def _out_kernel(x_ref, ya_ref, yb_ref, yc_ref, yd_ref, g0_ref, g1_ref, g2_ref, g3_ref,
                wa_ref, wb_ref, wc_ref, wd_ref, bg_ref, wo_ref, np_ref, o_ref):
    merged = None
    for i, (y_ref, w_ref, g_ref) in enumerate(((ya_ref, wa_ref, g0_ref), (yb_ref, wb_ref, g1_ref),
                                                (yc_ref, wc_ref, g2_ref), (yd_ref, wd_ref, g3_ref))):
        t = jnp.dot(y_ref[...], w_ref[...], preferred_element_type=F32)
        gate = _sigmoid(g_ref[...].astype(F32) + bg_ref[i:i + 1, :])
        merged = gate * t if merged is None else merged + gate * t
    o = jnp.dot(merged.astype(BF16), wo_ref[...], preferred_element_type=F32)
    ms = jnp.mean(o * o, axis=-1, keepdims=True)
    o_ref[...] = x_ref[...] + o * lax.rsqrt(ms + EPS) * np_ref[...]


def _merge_out(x2, p2, ya, yb, yc, yd, wa, wb, wc, wd, b_gate, w_out, norm_post, layer, tm):
    n = x2.shape[0]
    row = lambda width: pl.BlockSpec((tm, width), lambda i: (i, 0))
    full = lambda shape: pl.BlockSpec((None,) + shape, lambda i: (layer, 0, 0))
    gate = lambda br: pl.BlockSpec((tm, D_MODEL), lambda i, br=br: (i, U_GATE * LANE // D_MODEL + br))
    return pl.pallas_call(
        _out_kernel,
        grid=(n // tm,),
        in_specs=[row(D_MODEL), row(A_WIDTH), row(B_WIDTH), row(C_WIDTH), row(D_WIDTH),
                  gate(0), gate(1), gate(2), gate(3),
                  full((A_WIDTH, D_MODEL)), full((B_WIDTH, D_MODEL)), full((C_WIDTH, D_MODEL)),
                  full((D_WIDTH, D_MODEL)), full((N_BRANCH, D_MODEL)), full((D_MODEL, D_MODEL)),
                  full((1, D_MODEL))],
        out_specs=row(D_MODEL),
        out_shape=jax.ShapeDtypeStruct((n, D_MODEL), F32),
        compiler_params=_cparams("parallel"),
        name="merge_out",
    )(x2, ya, yb, yc, yd, p2, p2, p2, p2, wa, wb, wc, wd, b_gate, w_out, norm_post)


def kernel(x, norm_pre, norm_post, w_in, conv_w, conv_b, lru_wr, lru_br, lru_wi, lru_bi, lru_lambda,
           diff_lam_q1, diff_lam_k1, diff_lam_q2, diff_lam_k2, diff_subln, mla_q_norm, mla_kv_norm,
           mla_w_uq, mla_w_ukv, w_br_a, w_br_b, w_br_c, w_br_d, b_gate, w_out):
    b, s, d = x.shape
    depth = w_in.shape[0]
    n = b * s
    tm_in = min(2048, n)
    tm_out = min(512, n)
    tq = min(512, s)
    tk = min(512, s)

    w_in_t = jnp.swapaxes(w_in, 1, 2)
    w_tail = _in_tail_weights(w_in_t)
    out_weights = [w.astype(BF16) for w in (w_br_a, w_br_b, w_br_c, w_br_d)]
    w_out_bf = w_out.astype(BF16)
    a_bias = _a_bias_tables(jnp.asarray([2.0 ** (-8.0 * (i + 1) / A_SLOTS) for i in range(A_SLOTS)], F32))
    cos_t, sin_t = _d_tables(s)
    softplus_neg_lam = jnp.log1p(jnp.exp(-lru_lambda.astype(F32)))

    x2 = x.reshape(n, d)
    for l in range(depth):
        p2 = _inproj(x2, norm_pre[l][None, :], w_tail, w_in_t, l, tm_in)
        p3 = p2.reshape(b, s, P_WIDTH)
        ya = _mixer_a(p3, a_bias)
        bw, bb = _b_gate_weights(lru_wr[l], lru_br[l], lru_wi[l], lru_bi[l])
        yb = _mixer_b(p3, conv_w[l], conv_b[l][None, :], bw, bb, softplus_neg_lam[l])
        lam_params = jnp.stack([diff_lam_q1[l], diff_lam_k1[l], diff_lam_q2[l], diff_lam_k2[l]])
        yc = _mixer_c(p3, lam_params, diff_subln[l][None, :], l, tq, tk)
        dw = _d_weights(mla_w_uq[l], mla_w_ukv[l])
        qd, kd, vd = _dprep(p3, mla_q_norm[l][None, :], mla_kv_norm[l][None, :], dw, cos_t, sin_t, min(1024, s))
        yd = _mixer_d(p3, qd, kd, vd, min(2 * tq, s), tk)
        x2 = _merge_out(x2, p2, ya.reshape(n, -1), yb.reshape(n, -1), yc.reshape(n, -1), yd.reshape(n, -1),
                        *out_weights, b_gate, w_out_bf, norm_post[:, None, :], l, tm_out)
    return x2.reshape(b, s, d)
```

```python
import functools
import math

import jax
import jax.numpy as jnp
from jax import lax
from jax.experimental import pallas as pl
from jax.experimental.pallas import tpu as pltpu

F32 = jnp.float32
BF16 = jnp.bfloat16

D_MODEL = 1024
EPS = 1e-6
N_BRANCH = 4

A_PATTERNS = ((128, 1), (512, 4), (2048, 16))
A_SLOTS = 6
A_HEAD_DIM = 64
A_QKV = 1152
A_WIDTH = 384
A_RADIUS = 64
A_QBLK = 128
A_KWIN = 256
A_UNROLL = 8

B_WIDTH = 384
B_BLOCK_DIM = 64
B_C = 8.0

C_HEADS = 4
C_HALF = 64
C_VDIM = 128
C_QK = 512
C_WIDTH = 512

D_HEADS = 6
D_NOPE = 64
D_ROPE = 32
D_VDIM = 64
D_QLR = 256
D_KVLR = 128
D_WIDTH = 384
D_VROWS = D_VDIM + 16
ROPE_BASE = 10000.0

LANE = 128
SUBLANE = 8
NEG = -1e30
LOG2E = math.log2(math.e)
_NT = (((1,), (1,)), ((), ()))
VMEM_LIMIT = 56 * 1024 * 1024

IN_TN = 512
IN_MAIN_COLS = 3 * A_QKV + A_WIDTH + 2 * B_WIDTH + 2 * C_QK + 2 * C_WIDTH + D_QLR + D_KVLR
IN_DG_COL = IN_MAIN_COLS + D_ROPE
IN_GATE_COL = IN_DG_COL + D_WIDTH
IN_TAIL_TILES = (N_BRANCH * D_MODEL + D_WIDTH + LANE) // IN_TN
IN_MAIN_TILES = -(-(IN_MAIN_COLS + D_ROPE) // IN_TN)
U_GATE, U_DG = 0, N_BRANCH * D_MODEL // LANE
U_MAIN = IN_TAIL_TILES * IN_TN // LANE
U_AQ, U_AK, U_AV, U_AG = U_MAIN, U_MAIN + 9, U_MAIN + 18, U_MAIN + 27
U_BX, U_BG = U_MAIN + 30, U_MAIN + 33
U_CQ, U_CK, U_CV, U_CG = U_MAIN + 36, U_MAIN + 40, U_MAIN + 44, U_MAIN + 48
U_DCQ, U_DCKV, U_DKR = U_MAIN + 52, U_MAIN + 54, U_MAIN + 55
U_TOTAL = U_MAIN + IN_MAIN_TILES * IN_TN // LANE
P_WIDTH = U_TOTAL * LANE
assert U_TOTAL == 92 and U_DCQ % 2 == 0 and IN_GATE_COL + N_BRANCH * D_MODEL == 11552


def _cparams(*sem):
    return pltpu.CompilerParams(dimension_semantics=sem, vmem_limit_bytes=VMEM_LIMIT)


def _silu(x):
    return x * (1.0 / (1.0 + jnp.exp(-x)))


def _sigmoid(x):
    return 1.0 / (1.0 + jnp.exp(-x))


def _inproj_kernel(x_ref, g_ref, wtail_ref, wmain_ref, o_ref, h_ref):
    j = pl.program_id(1)

    @pl.when(j == 0)
    def _():
        x = x_ref[...]
        ms = jnp.mean(x * x, axis=-1, keepdims=True)
        h_ref[...] = (x * lax.rsqrt(ms + EPS) * g_ref[...]).astype(BF16)

    @pl.when(j < IN_TAIL_TILES)
    def _():
        o_ref[...] = lax.dot_general(h_ref[...], wtail_ref[...], _NT,
                                     preferred_element_type=F32).astype(o_ref.dtype)

    @pl.when(j >= IN_TAIL_TILES)
    def _():
        o_ref[...] = lax.dot_general(h_ref[...], wmain_ref[...].astype(BF16), _NT,
                                     preferred_element_type=F32).astype(o_ref.dtype)


def _in_tail_weights(w_in_t):
    gate = w_in_t[:, IN_GATE_COL:IN_GATE_COL + N_BRANCH * D_MODEL]
    dg = w_in_t[:, IN_DG_COL:IN_GATE_COL]
    pad = jnp.zeros((w_in_t.shape[0], IN_TAIL_TILES * IN_TN - gate.shape[1] - dg.shape[1], D_MODEL), w_in_t.dtype)
    return jnp.concatenate([gate, dg, pad], axis=1).astype(BF16)


def _inproj(x2, gain, w_tail, w_in_t, layer, tm):
    n = x2.shape[0]
    return pl.pallas_call(
        _inproj_kernel,
        grid=(n // tm, P_WIDTH // IN_TN),
        in_specs=[pl.BlockSpec((tm, D_MODEL), lambda i, j: (i, 0)),
                  pl.BlockSpec((1, D_MODEL), lambda i, j: (0, 0)),
                  pl.BlockSpec((None, IN_TN, D_MODEL),
                               lambda i, j: (layer, jnp.minimum(j, IN_TAIL_TILES - 1), 0)),
                  pl.BlockSpec((None, IN_TN, D_MODEL),
                               lambda i, j: (layer, jnp.maximum(j - IN_TAIL_TILES, 0), 0))],
        out_specs=pl.BlockSpec((tm, IN_TN), lambda i, j: (i, j)),
        out_shape=jax.ShapeDtypeStruct((n, P_WIDTH), BF16),
        scratch_shapes=[pltpu.VMEM((tm, D_MODEL), BF16)],
        compiler_params=_cparams("parallel", "arbitrary"),
        name="inproj",
    )(x2, gain, w_tail, w_in_t)


def _a_bias_tables(slopes):
    ii = jnp.arange(A_QBLK, dtype=jnp.int32)[:, None]
    jj = jnp.arange(A_KWIN, dtype=jnp.int32)[None, :]
    out = []
    for _, dil in A_PATTERNS:
        per_edge = []
        for off in (0, A_RADIUS, A_QBLK):
            rel = jnp.abs(off + ii - jj)
            dist = (rel * dil).astype(F32)
            b = -slopes[:, None, None] * dist[None] * LOG2E
            per_edge.append(jnp.where((rel <= A_RADIUS)[None], b, NEG))
        out.append(jnp.stack(per_edge))
    return jnp.stack(out)


def _a_pitch(dil):
    return dil + SUBLANE if dil % (2 * SUBLANE) == 0 else dil


def _a_kernel(q0, k0, v0, q1, k1, v1, q2, k2, v2, gate_ref, bias_ref, o_ref,
              qf, kf, vf, u_ref, z_ref, m_ref, up_ref, zp_ref, mp_ref):
    s_len = q0.shape[0]
    lane = lax.broadcasted_iota(jnp.int32, (1, LANE), 1)
    first = lane < A_HEAD_DIM
    ones_first = jnp.where(first, 1.0, 0.0).astype(BF16)
    ones_second = jnp.where(first, 0.0, 1.0).astype(BF16)
    scale = A_HEAD_DIM ** -0.5 * LOG2E
    padded_groups = [g for g, (_, dil) in enumerate(A_PATTERNS) if _a_pitch(dil) != dil]
    assert padded_groups == [len(A_PATTERNS) - 1]

    for g, ((_, dil), (qr, kr, vr)) in enumerate(zip(A_PATTERNS, ((q0, k0, v0), (q1, k1, v1), (q2, k2, v2)))):
        sub_len = s_len // dil
        nqb = sub_len // A_QBLK
        pitch = _a_pitch(dil)
        if pitch != dil:
            def stage(l2, carry, dil=dil, pitch=pitch, qr=qr, kr=kr, vr=vr):
                src = pl.ds(pl.multiple_of(l2 * 2 * dil, 2 * dil), 2 * dil)
                dst = pl.multiple_of(l2 * 2 * pitch, SUBLANE)
                for ref, buf, mul in ((qr, qf, scale), (kr, kf, None), (vr, vf, None)):
                    x = ref[src, :].astype(F32)
                    x = x if mul is None else x * mul
                    buf[pl.ds(dst, dil), :] = x[:dil]
                    buf[pl.ds(dst + pitch, dil), :] = x[dil:]
                return carry

            lax.fori_loop(0, sub_len // 2, stage, 0, unroll=4)
        elif dil > 1:
            rows = pl.ds(0, s_len)
            qf[rows, :] = qr[...].astype(F32) * scale
            kf[rows, :] = kr[...].astype(F32)
            vf[rows, :] = vr[...].astype(F32)

        def block(idx, carry, g=g, dil=dil, pitch=pitch, sub_len=sub_len, nqb=nqb, qr=qr, kr=kr, vr=vr):
            r = idx // nqb
            qb = idx % nqb
            qs = qb * A_QBLK
            ws = jnp.clip(qs - A_RADIUS, 0, sub_len - A_KWIN)
            edge = jnp.where(qb == 0, 0, jnp.where(qb == nqb - 1, 2, 1))
            if dil == 1:
                qrows = pl.ds(pl.multiple_of(qs, A_QBLK), A_QBLK)
                krows = pl.ds(pl.multiple_of(ws, A_RADIUS), A_KWIN)
                q = (qr[qrows, :].astype(F32) * scale).astype(BF16)
                k = kr[krows, :]
                v = vr[krows, :]
            else:
                qrows = pl.ds(r + qs * pitch, A_QBLK, stride=pitch)
                krows = pl.ds(r + ws * pitch, A_KWIN, stride=pitch)
                q = qf[qrows, :].astype(BF16)
                k = kf[krows, :].astype(BF16)
                v = vf[krows, :].astype(BF16)
            zq = jnp.zeros_like(q)
            zv = jnp.zeros_like(v)
            uz = None
            ms = []
            for h, head_lanes in enumerate((first, jnp.logical_not(first))):
                qh = jnp.where(head_lanes, q, zq)
                s = lax.dot_general(qh, k, (((1,), (1,)), ((), ())), preferred_element_type=F32)
                s = s + bias_ref[g, edge, h]
                mh = jnp.max(s, axis=-1, keepdims=True)
                p = jnp.exp2(s - mh).astype(BF16)
                ones_h = ones_first if h == 0 else ones_second
                vaug = jnp.concatenate([jnp.where(head_lanes, v, zv),
                                        jnp.broadcast_to(ones_h, v.shape)], axis=1)
                part = jnp.dot(p, vaug, preferred_element_type=F32)
                uz = part if uz is None else uz + part
                ms.append(mh)
            u = uz[:, :LANE]
            z = uz[:, LANE:]
            m = jnp.where(first, ms[0], ms[1])
            if g == 0:
                u_ref[qrows, :] = u
                z_ref[qrows, :] = z
                m_ref[qrows, :] = m
            elif pitch != dil:
                up_ref[qrows, :] = u
                zp_ref[qrows, :] = z
                mp_ref[qrows, :] = m
            else:
                m_old = m_ref[qrows, :]
                m_new = jnp.maximum(m_old, m)
                a = jnp.exp2(m_old - m_new)
                b = jnp.exp2(m - m_new)
                u_ref[qrows, :] = a * u_ref[qrows, :] + b * u
                z_ref[qrows, :] = a * z_ref[qrows, :] + b * z
                m_ref[qrows, :] = m_new
            return carry

        lax.fori_loop(0, dil * nqb, block, 0, unroll=A_UNROLL)

    dil = A_PATTERNS[padded_groups[0]][1]
    pitch = _a_pitch(dil)

    def finish(l2, carry):
        rows = pl.ds(pl.multiple_of(l2 * 2 * dil, 2 * dil), 2 * dil)
        src = pl.multiple_of(l2 * 2 * pitch, SUBLANE)

        def padded(ref):
            return jnp.concatenate([ref[pl.ds(src, dil), :], ref[pl.ds(src + pitch, dil), :]], axis=0)

        m_a, m_b = m_ref[rows, :], padded(mp_ref)
        m_new = jnp.maximum(m_a, m_b)
        a = jnp.exp2(m_a - m_new)
        b = jnp.exp2(m_b - m_new)
        u = a * u_ref[rows, :] + b * padded(up_ref)
        z = a * z_ref[rows, :] + b * padded(zp_ref)
        o_ref[rows, :] = (u / z * _silu(gate_ref[rows, :].astype(F32))).astype(o_ref.dtype)
        return carry

    lax.fori_loop(0, s_len // (2 * dil), finish, 0, unroll=4)


def _mixer_a(p3, bias):
    b, s, _ = p3.shape
    npair = A_SLOTS // 2
    assert s // A_PATTERNS[-1][1] >= A_KWIN
    padded_rows = max(s // dil * _a_pitch(dil) for _, dil in A_PATTERNS)

    def col(unit):
        return pl.BlockSpec((None, s, LANE), lambda bi, hp, unit=unit: (bi, 0, unit + hp))

    in_specs = []
    for g in range(len(A_PATTERNS)):
        for base in (U_AQ, U_AK, U_AV):
            in_specs.append(col(base + g * npair))
    in_specs.append(col(U_AG))
    in_specs.append(pl.BlockSpec((len(A_PATTERNS), 3, 2, A_QBLK, A_KWIN), lambda bi, hp: (0, 0, hp, 0, 0)))
    return pl.pallas_call(
        _a_kernel,
        grid=(b, npair),
        in_specs=in_specs,
        out_specs=pl.BlockSpec((None, s, LANE), lambda bi, hp: (bi, 0, hp)),
        out_shape=jax.ShapeDtypeStruct((b, s, A_WIDTH), BF16),
        scratch_shapes=([pltpu.VMEM((padded_rows, LANE), F32)] * 3 + [pltpu.VMEM((s, LANE), F32)] * 3
                        + [pltpu.VMEM((padded_rows, LANE), F32)] * 3),
        compiler_params=_cparams("parallel", "parallel"),
        name="mixer_a",
    )(*([p3] * 10), bias)


B_CHUNK = 256
B_PAD = 8


def _b_kernel(x_ref, g_ref, cw_ref, cb_ref, w_ref, bias_ref, sp_ref, o_ref, xs_ref, a_ref, h_ref, c_ref):
    s_len = x_ref.shape[0]
    nchunk = s_len // B_CHUNK
    ntile = s_len // SUBLANE
    zpad = jnp.zeros((B_PAD, LANE), F32)
    xs_ref[pl.ds(0, B_PAD), :] = zpad
    xs_ref[pl.ds(B_PAD + s_len, B_PAD), :] = zpad
    xs_ref[pl.ds(B_PAD, s_len), :] = x_ref[...].astype(F32)
    row_in_tile = lax.broadcasted_iota(jnp.int32, (1, SUBLANE, 1), 1)
    tile_idx = lax.broadcasted_iota(jnp.int32, (ntile, 1), 0)

    def scan_rows(acc_a, acc_h, axis, shifts, pos, reverse):
        n = acc_a.shape[axis]
        for sh in shifts:
            if reverse:
                sa = pltpu.roll(acc_a, n - sh, axis=axis)
                shh = pltpu.roll(acc_h, n - sh, axis=axis)
                keep = pos < n - sh
            else:
                sa = pltpu.roll(acc_a, sh, axis=axis)
                shh = pltpu.roll(acc_h, sh, axis=axis)
                keep = pos >= sh
            acc_h = acc_h + acc_a * jnp.where(keep, shh, 0.0)
            acc_a = acc_a * jnp.where(keep, sa, 1.0)
        return acc_a, acc_h

    tile_shifts = (1, 2, 4)
    summary_shifts = tuple(1 << e for e in range(ntile.bit_length() - 1))
    assert 1 << len(summary_shifts) == ntile

    def local_scan(c, carry):
        t0 = pl.multiple_of(c * B_CHUNK, B_CHUNK)
        win = xs_ref[pl.ds(t0, B_CHUNK + 2 * B_PAD), :]
        n = B_CHUNK + 2 * B_PAD
        xc = (cb_ref[...]
              + pltpu.roll(win, 1, axis=0) * cw_ref[0:1, :]
              + win * cw_ref[1:2, :]
              + pltpu.roll(win, n - 1, axis=0) * cw_ref[2:3, :]
              + pltpu.roll(win, n - 2, axis=0) * cw_ref[3:4, :])
        xc = xc[B_PAD:B_PAD + B_CHUNK, :]
        ri = jnp.dot(xc.astype(BF16), w_ref[...], preferred_element_type=F32) + bias_ref[...]
        rows = pl.ds(t0, B_CHUNK)
        for direction in (0, 1):
            r = _sigmoid(ri[:, (2 * direction) * LANE:(2 * direction + 1) * LANE])
            i = _sigmoid(ri[:, (2 * direction + 1) * LANE:(2 * direction + 2) * LANE])
            log_a = (-B_C) * r * sp_ref[direction:direction + 1, :]
            a = jnp.exp(log_a)
            u = jnp.sqrt(-jnp.tanh(log_a) * (a * a + 1.0)) * (i * xc)
            tiles = (B_CHUNK // SUBLANE, SUBLANE, LANE)
            acc_a, acc_h = scan_rows(a.reshape(tiles), u.reshape(tiles), 1, tile_shifts, row_in_tile,
                                     direction == 1)
            a_ref[direction, rows, :] = acc_a.reshape(B_CHUNK, LANE)
            h_ref[direction, rows, :] = acc_h.reshape(B_CHUNK, LANE)
        return carry

    lax.fori_loop(0, nchunk, local_scan, 0)

    for direction in (0, 1):
        reverse = direction == 1
        edge = 0 if reverse else SUBLANE - 1
        sum_a = a_ref[direction, pl.ds(edge, ntile, stride=SUBLANE), :]
        sum_h = h_ref[direction, pl.ds(edge, ntile, stride=SUBLANE), :]
        _, inc_h = scan_rows(sum_a, sum_h, 0, summary_shifts, tile_idx, reverse)
        if reverse:
            state_in = jnp.where(tile_idx < ntile - 1, pltpu.roll(inc_h, ntile - 1, axis=0), 0.0)
        else:
            state_in = jnp.where(tile_idx >= 1, pltpu.roll(inc_h, 1, axis=0), 0.0)
        for r in range(SUBLANE):
            c_ref[direction, pl.ds(r, ntile, stride=SUBLANE), :] = state_in

    def apply_state(c, carry):
        rows = pl.ds(pl.multiple_of(c * B_CHUNK, B_CHUNK), B_CHUNK)
        h = (h_ref[0, rows, :] + a_ref[0, rows, :] * c_ref[0, rows, :]
             + h_ref[1, rows, :] + a_ref[1, rows, :] * c_ref[1, rows, :])
        o_ref[rows, :] = (h * _silu(g_ref[rows, :].astype(F32))).astype(o_ref.dtype)
        return carry

    lax.fori_loop(0, nchunk, apply_state, 0)


def _mixer_b(p3, conv_w, conv_b, w_gate, b_gate, softplus_neg_lam):
    b, s, _ = p3.shape
    ngrp = B_WIDTH // LANE
    assert s % B_CHUNK == 0
    return pl.pallas_call(
        _b_kernel,
        grid=(b, ngrp),
        in_specs=[pl.BlockSpec((None, s, LANE), lambda bi, j: (bi, 0, U_BX + j)),
                  pl.BlockSpec((None, s, LANE), lambda bi, j: (bi, 0, U_BG + j)),
                  pl.BlockSpec((4, LANE), lambda bi, j: (0, j)),
                  pl.BlockSpec((1, LANE), lambda bi, j: (0, j)),
                  pl.BlockSpec((None, LANE, 4 * LANE), lambda bi, j: (j, 0, 0)),
                  pl.BlockSpec((None, 1, 4 * LANE), lambda bi, j: (j, 0, 0)),
                  pl.BlockSpec((2, LANE), lambda bi, j: (0, j))],
        out_specs=pl.BlockSpec((None, s, LANE), lambda bi, j: (bi, 0, j)),
        out_shape=jax.ShapeDtypeStruct((b, s, B_WIDTH), BF16),
        scratch_shapes=[pltpu.VMEM((s + 2 * B_PAD, LANE), F32)] + [pltpu.VMEM((2, s, LANE), F32)] * 3,
        compiler_params=_cparams("parallel", "parallel"),
        name="mixer_b",
    )(p3, p3, conv_w, conv_b, w_gate, b_gate, softplus_neg_lam)


def _b_gate_weights(w_r, b_r, w_i, b_i):
    per = LANE // B_BLOCK_DIM
    ngrp = B_WIDTH // LANE

    def blockdiag(w):
        w = w.reshape(2, ngrp, per, B_BLOCK_DIM, B_BLOCK_DIM)
        eye = jnp.eye(per, dtype=w.dtype)
        full = jnp.einsum("dgpcx,pq->dgpcqx", w, eye).reshape(2, ngrp, LANE, LANE)
        return full.transpose(1, 0, 2, 3)

    w = jnp.concatenate([blockdiag(w_r), blockdiag(w_i)], axis=-1)
    w = w.transpose(0, 2, 1, 3).reshape(ngrp, LANE, 4 * LANE).astype(BF16)
    bias = jnp.concatenate([b_r.reshape(2, ngrp, LANE), b_i.reshape(2, ngrp, LANE)], axis=-1)
    return w, bias.transpose(1, 0, 2).reshape(ngrp, 1, 4 * LANE)


def _attend_t(streams, tk, nk, s_ref, mt_ref, m_ref, acc_ref, tile_of=None):
    if isinstance(nk, int):
        assert nk % 2 == 0 and nk >= 2
    if tile_of is None:
        tile_of = lambda u: u
    for i in range(len(streams)):
        m_ref[i] = jnp.full(m_ref.shape[1:], NEG, F32)
        acc_ref[i] = jnp.zeros(acc_ref.shape[1:], F32)

    def key_rows(u):
        return pl.ds(pl.multiple_of(tile_of(u) * tk, tk), tk)

    def scores(u, slot):
        rows = key_rows(u)
        for i, st in enumerate(streams):
            s = jnp.dot(st["k_ref"][rows, :], st["q"](u), preferred_element_type=F32)
            s_ref[slot, i] = s
            mt_ref[slot, i] = jnp.max(s, axis=0, keepdims=True)

    def accumulate(u, slot, first=False):
        rows = key_rows(u)
        for i, st in enumerate(streams):
            s = s_ref[slot, i]
            if first and st.get("bias0") is not None:
                s = s + st["bias0"][...]
                m_tile = jnp.max(s, axis=0, keepdims=True)
            else:
                m_tile = mt_ref[slot, i]
            off = st["off"](u) if st.get("off") is not None else None
            m_old = m_ref[i]
            m_new = jnp.maximum(m_old, m_tile if off is None else m_tile + off)
            alpha = jnp.exp2(m_old - m_new)
            p = jnp.exp2(s - (m_new if off is None else m_new - off)).astype(BF16)
            acc_ref[i] = alpha * acc_ref[i] + jnp.dot(st["vt_ref"][:, rows], p, preferred_element_type=F32)
            m_ref[i] = m_new

    scores(0, 0)
    scores(1, 1)
    accumulate(0, 0, first=True)

    def body(j, carry):
        u = 1 + 2 * j
        scores(u + 1, 0)
        accumulate(u, 1)
        scores(u + 2, 1)
        accumulate(u + 1, 0)
        return carry

    pairs = (nk - 2) // 2 if isinstance(nk, int) else lax.shift_right_logical(nk - 2, 1)
    lax.fori_loop(0, pairs, body, 0)
    accumulate(nk - 1, 1)


C_AUG = 3
C_VROWS = C_VDIM + 16
C_MAPS = ((0, C_HALF), (1, 0))
C_UNDERFLOW = 128.0


def _c_kernel(q_ref, k_ref, v_ref, g_ref, lam_ref, subln_ref, o_ref,
              kaug_ref, vt_ref, dbias_ref, knorm_ref, s_ref, mt_ref, m_ref, acc_ref, *, tk, slopes, lam_init):
    h = pl.program_id(1)
    qi = pl.program_id(2)
    tq = q_ref.shape[0]
    s_len = k_ref.shape[0]
    nk = s_len // tk
    assert tq == tk
    lp = lam_ref[...]
    lam = (jnp.exp(jnp.sum(lp[0:1, :] * lp[1:2, :], axis=-1, keepdims=True))
           - jnp.exp(jnp.sum(lp[2:3, :] * lp[3:4, :], axis=-1, keepdims=True)) + lam_init)
    slope = jnp.where(h == 0, slopes[0], jnp.where(h == 1, slopes[1], jnp.where(h == 2, slopes[2], slopes[3])))
    c = slope.astype(F32) * LOG2E

    @pl.when(qi == 0)
    def _build_key_side():
        lane = lax.broadcasted_iota(jnp.int32, (1, LANE), 1)
        knorm2 = [jnp.zeros((tk, 1), F32), jnp.zeros((tk, 1), F32)]
        for ch in range(nk):
            rows = slice(ch * tk, (ch + 1) * tk)
            k = k_ref[rows, :]
            ksq = k.astype(F32) * k.astype(F32)
            for cmap in range(2):
                keep = (lane < C_HALF) if cmap == 0 else (lane >= C_HALF)
                knorm2[cmap] = jnp.maximum(knorm2[cmap], jnp.sum(jnp.where(keep, ksq, 0.0), axis=1, keepdims=True))
            jpos = (ch * tk + lax.broadcasted_iota(jnp.int32, (tk, 1), 0)).astype(F32)
            cj = c * jpos
            hi = cj.astype(BF16).astype(F32)
            mid = (cj - hi).astype(BF16).astype(F32)
            lo = (cj - hi - mid).astype(BF16).astype(F32)
            for cmap, base in C_MAPS:
                aug = jnp.where(lane == base, hi, jnp.where(lane == base + 1, mid,
                                                            jnp.where(lane == base + 2, lo, 0.0)))
                keep = (lane < C_HALF) if cmap == 0 else (lane >= C_HALF)
                kaug_ref[cmap, rows, :] = jnp.where(keep, k, aug.astype(BF16))
            vt_ref[0:C_VDIM, rows] = v_ref[rows, :].astype(F32).T.astype(BF16)
        r16 = lax.broadcasted_iota(jnp.int32, (C_VROWS - C_VDIM, s_len), 0)
        vt_ref[C_VDIM:C_VROWS, :] = jnp.where(r16 == 0, 1.0, 0.0).astype(BF16)
        ii = lax.broadcasted_iota(jnp.int32, (tk, tq), 1)
        jj = lax.broadcasted_iota(jnp.int32, (tk, tq), 0)
        dbias_ref[...] = -c * jnp.abs(ii - jj).astype(F32)
        for cmap in range(2):
            knorm_ref[cmap] = jnp.broadcast_to(jnp.max(knorm2[cmap], axis=0, keepdims=True), knorm_ref.shape[1:])

    q_t = (q_ref[...].astype(F32) * (C_HALF ** -0.5 * LOG2E)).T
    row = lax.broadcasted_iota(jnp.int32, (LANE, 1), 0)
    ci = c * (qi * tq + lax.broadcasted_iota(jnp.int32, (1, tq), 1)).astype(F32)

    qk_bound = jnp.zeros((1, 1), F32)
    for cmap in range(2):
        keep = (row < C_HALF) if cmap == 0 else (row >= C_HALF)
        qn2 = jnp.max(jnp.sum(jnp.where(keep, q_t * q_t, 0.0), axis=0, keepdims=True), axis=1, keepdims=True)
        qk_bound = jnp.maximum(qk_bound, jnp.sqrt(qn2 * knorm_ref[cmap][0:1, 0:1]))
    qk_bound = qk_bound * 1.01
    reach = ((2.0 * qk_bound + C_UNDERFLOW) / c - 1.0) * (1.0 / tk)
    reach = jnp.minimum(jnp.ceil(jnp.maximum(reach, 0.0)), float(nk)).astype(jnp.int32)
    w = jnp.max(reach)
    lo = jnp.maximum(qi - w, 0)
    hi = jnp.minimum(qi + w, nk - 1)
    n_win = hi - lo + 1
    odd = jnp.bitwise_and(n_win, 1)
    lo = lo - jnp.where(hi == nk - 1, odd, 0)
    n_steps = n_win + odd

    def tile_of(u):
        if isinstance(u, int) and u == 0:
            return qi
        t = lo + u - 1
        return t + (t >= qi).astype(jnp.int32)

    def sign_of(u):
        return jnp.where(tile_of(u) < qi, 1.0, -1.0).astype(F32)

    streams = []
    for cmap, base in C_MAPS:
        keep = (row < C_HALF) if cmap == 0 else (row >= C_HALF)
        q_base = jnp.where(keep, q_t, 0.0)
        aug_rows = jnp.where((row >= base) & (row < base + C_AUG), 1.0, 0.0)
        q_diag = q_base.astype(BF16)
        q_before = (q_base + aug_rows).astype(BF16)
        q_after = (q_base - aug_rows).astype(BF16)

        def q_of(u, q_diag=q_diag, q_before=q_before, q_after=q_after):
            if isinstance(u, int) and u == 0:
                return q_diag
            return jnp.where(tile_of(u) < qi, q_before, q_after)

        def off_of(u):
            if isinstance(u, int) and u == 0:
                return None
            return -sign_of(u) * ci

        streams.append(dict(q=q_of, k_ref=kaug_ref.at[cmap], vt_ref=vt_ref, off=off_of, bias0=dbias_ref))
    _attend_t(streams, tk, n_steps, s_ref, mt_ref, m_ref, acc_ref, tile_of)

    o_t = []
    for cmap in range(2):
        acc = acc_ref[cmap]
        o_t.append(acc[:C_VDIM, :] / acc[C_VDIM:C_VDIM + 1, :])
    o = (o_t[0] - lam * o_t[1]).T
    ms = jnp.mean(o * o, axis=-1, keepdims=True)
    o = o * lax.rsqrt(ms + EPS) * subln_ref[...] * (1.0 - lam_init)
    o_ref[...] = (o * _silu(g_ref[...].astype(F32))).astype(o_ref.dtype)


def _mixer_c(p3, lam_params, subln, layer, tq, tk):
    b, s, _ = p3.shape
    lam_init = 0.8 - 0.6 * math.exp(-0.3 * layer)
    slopes = tuple(2.0 ** (-8.0 * (i + 1) / C_HEADS) for i in range(C_HEADS))
    kern = functools.partial(_c_kernel, tk=tk, slopes=slopes, lam_init=lam_init)
    return pl.pallas_call(
        kern,
        grid=(b, C_HEADS, s // tq),
        in_specs=[pl.BlockSpec((None, tq, LANE), lambda bi, h, i: (bi, i, U_CQ + h)),
                  pl.BlockSpec((None, s, LANE), lambda bi, h, i: (bi, 0, U_CK + h)),
                  pl.BlockSpec((None, s, LANE), lambda bi, h, i: (bi, 0, U_CV + h)),
                  pl.BlockSpec((None, tq, LANE), lambda bi, h, i: (bi, i, U_CG + h)),
                  pl.BlockSpec((4, C_HALF), lambda bi, h, i: (0, 0)),
                  pl.BlockSpec((1, C_VDIM), lambda bi, h, i: (0, 0))],
        out_specs=pl.BlockSpec((None, tq, LANE), lambda bi, h, i: (bi, i, h)),
        out_shape=jax.ShapeDtypeStruct((b, s, C_WIDTH), BF16),
        scratch_shapes=[pltpu.VMEM((2, s, LANE), BF16), pltpu.VMEM((C_VROWS, s), BF16),
                        pltpu.VMEM((tk, tq), F32), pltpu.VMEM((2, SUBLANE, LANE), F32),
                        pltpu.VMEM((2, 2, tk, tq), F32),
                        pltpu.VMEM((2, 2, 1, tq), F32), pltpu.VMEM((2, 1, tq), F32),
                        pltpu.VMEM((2, C_VROWS, tq), F32)],
        compiler_params=_cparams("parallel", "parallel", "arbitrary"),
        name="mixer_c",
    )(p3, p3, p3, p3, lam_params, subln)


def _dprep_kernel(cq_ref, ckv_ref, kr_ref, qn_ref, kvn_ref, wqt_ref, wqst_ref, wk_ref, wvt_ref,
                  e1_ref, e2_ref, cos_ref, sin_ref, cost_ref, sint_ref, vonet_ref, qt_out, k_out, vt_out):
    def norm(x_ref, gain_ref):
        x = x_ref[...].astype(F32)
        ms = jnp.mean(x * x, axis=-1, keepdims=True)
        return (x * lax.rsqrt(ms + EPS) * gain_ref[...]).astype(BF16)

    cqn = norm(cq_ref, qn_ref)
    ckvn = norm(ckv_ref, kvn_ref)
    kr = kr_ref[...]
    cos = jnp.concatenate([cos_ref[...]] * D_HEADS, axis=1)
    sin = jnp.concatenate([sin_ref[...]] * D_HEADS, axis=1)
    cos_t = jnp.concatenate([cost_ref[...]] * D_HEADS, axis=0)
    sin_t = jnp.concatenate([sint_ref[...]] * D_HEADS, axis=0)
    q_t = (lax.dot_general(wqt_ref[...], cqn, _NT, preferred_element_type=F32) * cos_t
           + lax.dot_general(wqst_ref[...], cqn, _NT, preferred_element_type=F32) * sin_t)
    qt_out[...] = (q_t * ((D_NOPE + D_ROPE) ** -0.5 * LOG2E)).astype(qt_out.dtype)
    k = (jnp.dot(ckvn, wk_ref[...], preferred_element_type=F32)
         + jnp.dot(kr, e1_ref[...], preferred_element_type=F32) * cos
         + jnp.dot(kr, e2_ref[...], preferred_element_type=F32) * sin)
    k_out[...] = k.astype(k_out.dtype)
    v_t = lax.dot_general(wvt_ref[...], ckvn, _NT, preferred_element_type=F32) + vonet_ref[...]
    vt_out[...] = v_t.astype(vt_out.dtype)


def _d_tables(s):
    inv = ROPE_BASE ** (-jnp.arange(0, D_ROPE, 2, dtype=F32) / D_ROPE)
    ang = jnp.arange(s, dtype=F32)[:, None] * inv[None, :]
    cos, sin = jnp.cos(ang), jnp.sin(ang)
    pad = LANE - D_NOPE - D_ROPE
    cos_t = jnp.concatenate([jnp.ones((s, D_NOPE), F32), cos, cos, jnp.zeros((s, pad), F32)], axis=1)
    sin_t = jnp.concatenate([jnp.zeros((s, D_NOPE), F32), -sin, sin, jnp.zeros((s, pad), F32)], axis=1)
    return cos_t, sin_t


def _d_weights(w_uq, w_ukv):
    half = D_ROPE // 2
    pad = LANE - D_NOPE - D_ROPE
    wq = w_uq.reshape(D_QLR, D_HEADS, D_NOPE + D_ROPE)
    zq = jnp.zeros((D_QLR, D_HEADS, pad), w_uq.dtype)
    wq_main = jnp.concatenate([wq, zq], axis=-1).reshape(D_QLR, D_HEADS * LANE)
    wq_swap = jnp.concatenate([jnp.zeros((D_QLR, D_HEADS, D_NOPE), w_uq.dtype),
                               wq[..., D_NOPE + half:], wq[..., D_NOPE:D_NOPE + half], zq],
                              axis=-1).reshape(D_QLR, D_HEADS * LANE)
    wkv = w_ukv.reshape(D_KVLR, D_HEADS, D_NOPE + D_VDIM)
    zk = jnp.zeros((D_KVLR, D_HEADS, LANE - D_NOPE), w_ukv.dtype)
    wk = jnp.concatenate([wkv[..., :D_NOPE], zk], axis=-1).reshape(D_KVLR, D_HEADS * LANE)
    wv = jnp.concatenate([wkv[..., D_NOPE:], jnp.zeros((D_KVLR, D_HEADS, D_VROWS - D_VDIM), w_ukv.dtype)],
                         axis=-1).reshape(D_KVLR, D_HEADS * D_VROWS)
    src = jnp.arange(D_ROPE)
    e1 = jnp.zeros((LANE, LANE), F32).at[src, D_NOPE + src].set(1.0)
    e2 = jnp.zeros((LANE, LANE), F32).at[(src + half) % D_ROPE, D_NOPE + src].set(1.0)
    e1 = jnp.tile(e1, (1, D_HEADS))
    e2 = jnp.tile(e2, (1, D_HEADS))
    vone = jnp.zeros((D_VROWS, 1), F32).at[D_VDIM, 0].set(1.0)
    vone = jnp.tile(vone, (D_HEADS, 1))
    return (wq_main.T.astype(BF16), wq_swap.T.astype(BF16), wk.astype(BF16), wv.T.astype(BF16),
            e1.astype(BF16), e2.astype(BF16), vone)


def _dprep(p3, q_norm, kv_norm, dw, cos_t, sin_t, tm):
    b, s, _ = p3.shape
    wqt, wqst, wk, wvt, e1, e2, vonet = dw
    wide = D_HEADS * LANE
    vwide = D_HEADS * D_VROWS
    full = lambda shape: pl.BlockSpec(shape, lambda bi, i: (0,) * len(shape))
    tok_major = pl.BlockSpec((None, tm, wide), lambda bi, i: (bi, i, 0))
    feat_major = lambda rows: pl.BlockSpec((None, rows, tm), lambda bi, i: (bi, 0, i))
    return pl.pallas_call(
        _dprep_kernel,
        grid=(b, s // tm),
        in_specs=[pl.BlockSpec((None, tm, D_QLR), lambda bi, i: (bi, i, U_DCQ // 2)),
                  pl.BlockSpec((None, tm, LANE), lambda bi, i: (bi, i, U_DCKV)),
                  pl.BlockSpec((None, tm, LANE), lambda bi, i: (bi, i, U_DKR)),
                  full((1, D_QLR)), full((1, D_KVLR)),
                  full((wide, D_QLR)), full((wide, D_QLR)), full((D_KVLR, wide)), full((vwide, D_KVLR)),
                  full((LANE, wide)), full((LANE, wide)),
                  pl.BlockSpec((tm, LANE), lambda bi, i: (i, 0)),
                  pl.BlockSpec((tm, LANE), lambda bi, i: (i, 0)),
                  pl.BlockSpec((LANE, tm), lambda bi, i: (0, i)),
                  pl.BlockSpec((LANE, tm), lambda bi, i: (0, i)),
                  full((vwide, 1))],
        out_specs=[feat_major(wide), tok_major, feat_major(vwide)],
        out_shape=[jax.ShapeDtypeStruct((b, wide, s), BF16), jax.ShapeDtypeStruct((b, s, wide), BF16),
                   jax.ShapeDtypeStruct((b, vwide, s), BF16)],
        compiler_params=_cparams("parallel", "parallel"),
        name="mixer_d_prep",
    )(p3, p3, p3, q_norm, kv_norm, wqt, wqst, wk, wvt, e1, e2, cos_t, sin_t, cos_t.T, sin_t.T, vonet)


def _d_kernel(qt_ref, k_ref, vt_ref, g_ref, o_ref, s_ref, mt_ref, m_ref, acc_ref, *, tk):
    streams = []
    for h in range(2):
        sl = slice(h * LANE, (h + 1) * LANE)
        vrows = slice(h * D_VROWS, (h + 1) * D_VROWS)
        streams.append(dict(q=lambda u, q=qt_ref[sl, :]: q, k_ref=k_ref.at[:, sl], vt_ref=vt_ref.at[vrows, :]))
    _attend_t(streams, tk, k_ref.shape[0] // tk, s_ref, mt_ref, m_ref, acc_ref)
    parts = []
    for h in range(2):
        acc = acc_ref[h]
        parts.append(acc[:D_VDIM, :] / acc[D_VDIM:D_VDIM + 1, :])
    o = jnp.concatenate(parts, axis=0).T
    o_ref[...] = (o * _silu(g_ref[...].astype(F32))).astype(o_ref.dtype)


def _mixer_d(p3, qt, kd, vt, tq, tk):
    b, s, _ = p3.shape
    npair = D_HEADS // 2
    return pl.pallas_call(
        functools.partial(_d_kernel, tk=tk),
        grid=(b, npair, s // tq),
        in_specs=[pl.BlockSpec((None, 2 * LANE, tq), lambda bi, hp, i: (bi, hp, i)),
                  pl.BlockSpec((None, s, 2 * LANE), lambda bi, hp, i: (bi, 0, hp)),
                  pl.BlockSpec((None, 2 * D_VROWS, s), lambda bi, hp, i: (bi, hp, 0)),
                  pl.BlockSpec((None, tq, LANE), lambda bi, hp, i: (bi, i, U_DG + hp))],
        out_specs=pl.BlockSpec((None, tq, LANE), lambda bi, hp, i: (bi, i, hp)),
        out_shape=jax.ShapeDtypeStruct((b, s, D_WIDTH), BF16),
        scratch_shapes=[pltpu.VMEM((2, 2, tk, tq), F32), pltpu.VMEM((2, 2, 1, tq), F32),
                        pltpu.VMEM((2, 1, tq), F32), pltpu.VMEM((2, D_VROWS, tq), F32)],
        compiler_params=_cparams("parallel", "parallel", "arbitrary"),
        name="mixer_d",
    )(qt, kd, vt, p3)


def _out_kernel(x_ref, ya_ref, yb_ref, yc_ref, yd_ref, g0_ref, g1_ref, g2_ref, g3_ref,
                wa_ref, wb_ref, wc_ref, wd_ref, bg_ref, wo_ref, np_ref, o_ref):
    merged = None
    for i, (y_ref, w_ref, g_ref) in enumerate(((ya_ref, wa_ref, g0_ref), (yb_ref, wb_ref, g1_ref),
                                                (yc_ref, wc_ref, g2_ref), (yd_ref, wd_ref, g3_ref))):
        t = jnp.dot(y_ref[...], w_ref[...], preferred_element_type=F32)
        gate = _sigmoid(g_ref[...].astype(F32) + bg_ref[i:i + 1, :])
        merged = gate * t if merged is None else merged + gate * t
    o = jnp.dot(merged.astype(BF16), wo_ref[...], preferred_element_type=F32)
    ms = jnp.mean(o * o, axis=-1, keepdims=True)
    o_ref[...] = x_ref[...] + o * lax.rsqrt(ms + EPS) * np_ref[...]


def _merge_out(x2, p2, ya, yb, yc, yd, wa, wb, wc, wd, b_gate, w_out, norm_post, layer, tm):
    n = x2.shape[0]
    row = lambda width: pl.BlockSpec((tm, width), lambda i: (i, 0))
    full = lambda shape: pl.BlockSpec((None,) + shape, lambda i: (layer, 0, 0))
    gate = lambda br: pl.BlockSpec((tm, D_MODEL), lambda i, br=br: (i, U_GATE * LANE // D_MODEL + br))
    return pl.pallas_call(
        _out_kernel,
        grid=(n // tm,),
        in_specs=[row(D_MODEL), row(A_WIDTH), row(B_WIDTH), row(C_WIDTH), row(D_WIDTH),
                  gate(0), gate(1), gate(2), gate(3),
                  full((A_WIDTH, D_MODEL)), full((B_WIDTH, D_MODEL)), full((C_WIDTH, D_MODEL)),
                  full((D_WIDTH, D_MODEL)), full((N_BRANCH, D_MODEL)), full((D_MODEL, D_MODEL)),
                  full((1, D_MODEL))],
        out_specs=row(D_MODEL),
        out_shape=jax.ShapeDtypeStruct((n, D_MODEL), F32),
        compiler_params=_cparams("parallel"),
        name="merge_out",
    )(x2, ya, yb, yc, yd, p2, p2, p2, p2, wa, wb, wc, wd, b_gate, w_out, norm_post)


def kernel(x, norm_pre, norm_post, w_in, conv_w, conv_b, lru_wr, lru_br, lru_wi, lru_bi, lru_lambda,
           diff_lam_q1, diff_lam_k1, diff_lam_q2, diff_lam_k2, diff_subln, mla_q_norm, mla_kv_norm,
           mla_w_uq, mla_w_ukv, w_br_a, w_br_b, w_br_c, w_br_d, b_gate, w_out):
    b, s, d = x.shape
    depth = w_in.shape[0]
    n = b * s
    tm_in = min(2048, n)
    tm_out = min(512, n)
    tq = min(512, s)
    tk = min(512, s)

    w_in_t = jnp.swapaxes(w_in, 1, 2)
    w_tail = _in_tail_weights(w_in_t)
    out_weights = [w.astype(BF16) for w in (w_br_a, w_br_b, w_br_c, w_br_d)]
    w_out_bf = w_out.astype(BF16)
    a_bias = _a_bias_tables(jnp.asarray([2.0 ** (-8.0 * (i + 1) / A_SLOTS) for i in range(A_SLOTS)], F32))
    cos_t, sin_t = _d_tables(s)
    softplus_neg_lam = jnp.log1p(jnp.exp(-lru_lambda.astype(F32)))

    x2 = x.reshape(n, d)
    for l in range(depth):
        p2 = _inproj(x2, norm_pre[l][None, :], w_tail, w_in_t, l, tm_in)
        p3 = p2.reshape(b, s, P_WIDTH)
        ya = _mixer_a(p3, a_bias)
        bw, bb = _b_gate_weights(lru_wr[l], lru_br[l], lru_wi[l], lru_bi[l])
        yb = _mixer_b(p3, conv_w[l], conv_b[l][None, :], bw, bb, softplus_neg_lam[l])
        lam_params = jnp.stack([diff_lam_q1[l], diff_lam_k1[l], diff_lam_q2[l], diff_lam_k2[l]])
        yc = _mixer_c(p3, lam_params, diff_subln[l][None, :], l, tq, tk)
        dw = _d_weights(mla_w_uq[l], mla_w_ukv[l])
        qd, kd, vd = _dprep(p3, mla_q_norm[l][None, :], mla_kv_norm[l][None, :], dw, cos_t, sin_t, min(1024, s))
        yd = _mixer_d(p3, qd, kd, vd, min(2 * tq, s), tk)
        x2 = _merge_out(x2, p2, ya.reshape(n, -1), yb.reshape(n, -1), yc.reshape(n, -1), yd.reshape(n, -1),
                        *out_weights, b_gate, w_out_bf, norm_post[:, None, :], l, tm_out)
    return x2.reshape(b, s, d)
```

```python
import functools
import math

import jax
import jax.numpy as jnp
from jax import lax
from jax.experimental import pallas as pl
from jax.experimental.pallas import tpu as pltpu

F32 = jnp.float32
BF16 = jnp.bfloat16

D_MODEL = 1024
EPS = 1e-6
N_BRANCH = 4

A_PATTERNS = ((128, 1), (512, 4), (2048, 16))
A_SLOTS = 6
A_HEAD_DIM = 64
A_QKV = 1152
A_WIDTH = 384
A_RADIUS = 64
A_QBLK = 128
A_KWIN = 256
A_UNROLL = 16

B_WIDTH = 384
B_BLOCK_DIM = 64
B_C = 8.0

C_HEADS = 4
C_HALF = 64
C_VDIM = 128
C_QK = 512
C_WIDTH = 512

D_HEADS = 6
D_NOPE = 64
D_ROPE = 32
D_VDIM = 64
D_QLR = 256
D_KVLR = 128
D_WIDTH = 384
D_VROWS = D_VDIM + 16
ROPE_BASE = 10000.0

LANE = 128
SUBLANE = 8
NEG = -1e30
LOG2E = math.log2(math.e)
_NT = (((1,), (1,)), ((), ()))
VMEM_LIMIT = 56 * 1024 * 1024

IN_TN = 512
IN_MAIN_COLS = 3 * A_QKV + A_WIDTH + 2 * B_WIDTH + 2 * C_QK + 2 * C_WIDTH + D_QLR + D_KVLR
IN_DG_COL = IN_MAIN_COLS + D_ROPE
IN_GATE_COL = IN_DG_COL + D_WIDTH
IN_TAIL_TILES = (N_BRANCH * D_MODEL + D_WIDTH + LANE) // IN_TN
IN_MAIN_TILES = -(-(IN_MAIN_COLS + D_ROPE) // IN_TN)
U_GATE, U_DG = 0, N_BRANCH * D_MODEL // LANE
U_MAIN = IN_TAIL_TILES * IN_TN // LANE
U_AQ, U_AK, U_AV, U_AG = U_MAIN, U_MAIN + 9, U_MAIN + 18, U_MAIN + 27
U_BX, U_BG = U_MAIN + 30, U_MAIN + 33
U_CQ, U_CK, U_CV, U_CG = U_MAIN + 36, U_MAIN + 40, U_MAIN + 44, U_MAIN + 48
U_DCQ, U_DCKV, U_DKR = U_MAIN + 52, U_MAIN + 54, U_MAIN + 55
U_TOTAL = U_MAIN + IN_MAIN_TILES * IN_TN // LANE
P_WIDTH = U_TOTAL * LANE
assert U_TOTAL == 92 and U_DCQ % 2 == 0 and IN_GATE_COL + N_BRANCH * D_MODEL == 11552


def _cparams(*sem):
    return pltpu.CompilerParams(dimension_semantics=sem, vmem_limit_bytes=VMEM_LIMIT)


def _silu(x):
    return x * (1.0 / (1.0 + jnp.exp(-x)))


def _sigmoid(x):
    return 1.0 / (1.0 + jnp.exp(-x))


def _inproj_kernel(x_ref, g_ref, wtail_ref, wmain_ref, o_ref, h_ref):
    j = pl.program_id(1)

    @pl.when(j == 0)
    def _():
        x = x_ref[...]
        ms = jnp.mean(x * x, axis=-1, keepdims=True)
        h_ref[...] = (x * lax.rsqrt(ms + EPS) * g_ref[...]).astype(BF16)

    @pl.when(j < IN_TAIL_TILES)
    def _():
        o_ref[...] = lax.dot_general(h_ref[...], wtail_ref[0].astype(BF16), _NT,
                                     preferred_element_type=F32).astype(o_ref.dtype)

    @pl.when(j >= IN_TAIL_TILES)
    def _():
        o_ref[...] = lax.dot_general(h_ref[...], wmain_ref[...].astype(BF16), _NT,
                                     preferred_element_type=F32).astype(o_ref.dtype)


def _inproj(x2, gain, w_in_t, layer, tm):
    n = x2.shape[0]
    gate_tiles = IN_TAIL_TILES - 1

    def tail_row(j):
        per = IN_TN // D_ROPE
        unit = jnp.where(j < gate_tiles, IN_GATE_COL // D_ROPE + jnp.minimum(j, gate_tiles - 1) * per,
                         IN_DG_COL // D_ROPE)
        return unit * D_ROPE

    return pl.pallas_call(
        _inproj_kernel,
        grid=(n // tm, P_WIDTH // IN_TN),
        in_specs=[pl.BlockSpec((tm, D_MODEL), lambda i, j: (i, 0)),
                  pl.BlockSpec((1, D_MODEL), lambda i, j: (0, 0)),
                  pl.BlockSpec((pl.Element(1), pl.Element(IN_TN), pl.Element(D_MODEL)),
                               lambda i, j: (layer, tail_row(j), 0)),
                  pl.BlockSpec((None, IN_TN, D_MODEL),
                               lambda i, j: (layer, jnp.maximum(j - IN_TAIL_TILES, 0), 0))],
        out_specs=pl.BlockSpec((tm, IN_TN), lambda i, j: (i, j)),
        out_shape=jax.ShapeDtypeStruct((n, P_WIDTH), BF16),
        scratch_shapes=[pltpu.VMEM((tm, D_MODEL), BF16)],
        compiler_params=_cparams("parallel", "arbitrary"),
        name="inproj",
    )(x2, gain, w_in_t, w_in_t)


def _a_bias_tables(slopes):
    ii = jnp.arange(A_QBLK, dtype=jnp.int32)[:, None]
    jj = jnp.arange(A_KWIN, dtype=jnp.int32)[None, :]
    out = []
    for _, dil in A_PATTERNS:
        per_edge = []
        for off in (0, A_RADIUS, A_QBLK):
            rel = jnp.abs(off + ii - jj)
            dist = (rel * dil).astype(F32)
            b = -slopes[:, None, None] * dist[None] * LOG2E
            per_edge.append(jnp.where((rel <= A_RADIUS)[None], b, NEG))
        out.append(jnp.stack(per_edge))
    return jnp.stack(out)


def _a_pitch(dil):
    return dil + SUBLANE if dil % (2 * SUBLANE) == 0 else dil


def _a_kernel(q0, k0, v0, q1, k1, v1, q2, k2, v2, gate_ref, bias_ref, o_ref,
              qf, kf, vf, u_ref, z_ref, m_ref, up_ref, zp_ref, mp_ref):
    s_len = q0.shape[0]
    lane = lax.broadcasted_iota(jnp.int32, (1, LANE), 1)
    first = lane < A_HEAD_DIM
    ones_first = jnp.where(first, 1.0, 0.0).astype(BF16)
    ones_second = jnp.where(first, 0.0, 1.0).astype(BF16)
    scale = A_HEAD_DIM ** -0.5 * LOG2E
    padded_groups = [g for g, (_, dil) in enumerate(A_PATTERNS) if _a_pitch(dil) != dil]
    assert padded_groups == [len(A_PATTERNS) - 1]

    for g, ((_, dil), (qr, kr, vr)) in enumerate(zip(A_PATTERNS, ((q0, k0, v0), (q1, k1, v1), (q2, k2, v2)))):
        sub_len = s_len // dil
        nqb = sub_len // A_QBLK
        pitch = _a_pitch(dil)
        if pitch != dil:
            def stage(l2, carry, dil=dil, pitch=pitch, qr=qr, kr=kr, vr=vr):
                src = pl.ds(pl.multiple_of(l2 * 2 * dil, 2 * dil), 2 * dil)
                dst = pl.multiple_of(l2 * 2 * pitch, SUBLANE)
                for ref, buf, mul in ((qr, qf, scale), (kr, kf, None), (vr, vf, None)):
                    x = ref[src, :].astype(F32)
                    x = x if mul is None else x * mul
                    buf[pl.ds(dst, dil), :] = x[:dil]
                    buf[pl.ds(dst + pitch, dil), :] = x[dil:]
                return carry

            lax.fori_loop(0, sub_len // 2, stage, 0, unroll=4)
        elif dil > 1:
            rows = pl.ds(0, s_len)
            qf[rows, :] = qr[...].astype(F32) * scale
            kf[rows, :] = kr[...].astype(F32)
            vf[rows, :] = vr[...].astype(F32)

        def block(idx, carry, g=g, dil=dil, pitch=pitch, sub_len=sub_len, nqb=nqb, qr=qr, kr=kr, vr=vr):
            r = idx // nqb
            qb = idx % nqb
            qs = qb * A_QBLK
            ws = jnp.clip(qs - A_RADIUS, 0, sub_len - A_KWIN)
            edge = jnp.where(qb == 0, 0, jnp.where(qb == nqb - 1, 2, 1))
            if dil == 1:
                qrows = pl.ds(pl.multiple_of(qs, A_QBLK), A_QBLK)
                krows = pl.ds(pl.multiple_of(ws, A_RADIUS), A_KWIN)
                q = (qr[qrows, :].astype(F32) * scale).astype(BF16)
                k = kr[krows, :]
                v = vr[krows, :]
            else:
                qrows = pl.ds(r + qs * pitch, A_QBLK, stride=pitch)
                krows = pl.ds(r + ws * pitch, A_KWIN, stride=pitch)
                q = qf[qrows, :].astype(BF16)
                k = kf[krows, :].astype(BF16)
                v = vf[krows, :].astype(BF16)
            zq = jnp.zeros_like(q)
            zv = jnp.zeros_like(v)
            uz = None
            ms = []
            for h, head_lanes in enumerate((first, jnp.logical_not(first))):
                qh = jnp.where(head_lanes, q, zq)
                s = lax.dot_general(qh, k, (((1,), (1,)), ((), ())), preferred_element_type=F32)
                s = s + bias_ref[g, edge, h]
                mh = jnp.max(s, axis=-1, keepdims=True)
                p = jnp.exp2(s - mh).astype(BF16)
                ones_h = ones_first if h == 0 else ones_second
                vaug = jnp.concatenate([jnp.where(head_lanes, v, zv),
                                        jnp.broadcast_to(ones_h, v.shape)], axis=1)
                part = jnp.dot(p, vaug, preferred_element_type=F32)
                uz = part if uz is None else uz + part
                ms.append(mh)
            u = uz[:, :LANE]
            z = uz[:, LANE:]
            m = jnp.where(first, ms[0], ms[1])
            if g == 0:
                u_ref[qrows, :] = u
                z_ref[qrows, :] = z
                m_ref[qrows, :] = m
            elif pitch != dil:
                up_ref[qrows, :] = u
                zp_ref[qrows, :] = z
                mp_ref[qrows, :] = m
            else:
                m_old = m_ref[qrows, :]
                m_new = jnp.maximum(m_old, m)
                a = jnp.exp2(m_old - m_new)
                b = jnp.exp2(m - m_new)
                u_ref[qrows, :] = a * u_ref[qrows, :] + b * u
                z_ref[qrows, :] = a * z_ref[qrows, :] + b * z
                m_ref[qrows, :] = m_new
            return carry

        lax.fori_loop(0, dil * nqb, block, 0, unroll=A_UNROLL)

    dil = A_PATTERNS[padded_groups[0]][1]
    pitch = _a_pitch(dil)

    def finish(l2, carry):
        rows = pl.ds(pl.multiple_of(l2 * 2 * dil, 2 * dil), 2 * dil)
        src = pl.multiple_of(l2 * 2 * pitch, SUBLANE)

        def padded(ref):
            return jnp.concatenate([ref[pl.ds(src, dil), :], ref[pl.ds(src + pitch, dil), :]], axis=0)

        m_a, m_b = m_ref[rows, :], padded(mp_ref)
        m_new = jnp.maximum(m_a, m_b)
        a = jnp.exp2(m_a - m_new)
        b = jnp.exp2(m_b - m_new)
        u = a * u_ref[rows, :] + b * padded(up_ref)
        z = a * z_ref[rows, :] + b * padded(zp_ref)
        o_ref[rows, :] = (u / z * _silu(gate_ref[rows, :].astype(F32))).astype(o_ref.dtype)
        return carry

    lax.fori_loop(0, s_len // (2 * dil), finish, 0, unroll=4)


def _mixer_a(p3, bias):
    b, s, _ = p3.shape
    npair = A_SLOTS // 2
    assert s // A_PATTERNS[-1][1] >= A_KWIN
    padded_rows = max(s // dil * _a_pitch(dil) for _, dil in A_PATTERNS)

    def col(unit):
        return pl.BlockSpec((None, s, LANE), lambda bi, hp, unit=unit: (bi, 0, unit + hp))

    in_specs = []
    for g in range(len(A_PATTERNS)):
        for base in (U_AQ, U_AK, U_AV):
            in_specs.append(col(base + g * npair))
    in_specs.append(col(U_AG))
    in_specs.append(pl.BlockSpec((len(A_PATTERNS), 3, 2, A_QBLK, A_KWIN), lambda bi, hp: (0, 0, hp, 0, 0)))
    return pl.pallas_call(
        _a_kernel,
        grid=(b, npair),
        in_specs=in_specs,
        out_specs=pl.BlockSpec((None, s, LANE), lambda bi, hp: (bi, 0, hp)),
        out_shape=jax.ShapeDtypeStruct((b, s, A_WIDTH), BF16),
        scratch_shapes=([pltpu.VMEM((padded_rows, LANE), F32)] * 3 + [pltpu.VMEM((s, LANE), F32)] * 3
                        + [pltpu.VMEM((padded_rows, LANE), F32)] * 3),
        compiler_params=_cparams("parallel", "parallel"),
        name="mixer_a",
    )(*([p3] * 10), bias)


B_CHUNK = 256
B_PAD = 8


def _b_kernel(x_ref, g_ref, cw_ref, cb_ref, w_ref, bias_ref, sp_ref, o_ref, xs_ref, a_ref, h_ref, c_ref):
    s_len = x_ref.shape[0]
    nchunk = s_len // B_CHUNK
    ntile = s_len // SUBLANE
    zpad = jnp.zeros((B_PAD, LANE), F32)
    xs_ref[pl.ds(0, B_PAD), :] = zpad
    xs_ref[pl.ds(B_PAD + s_len, B_PAD), :] = zpad
    xs_ref[pl.ds(B_PAD, s_len), :] = x_ref[...].astype(F32)
    row_in_tile = lax.broadcasted_iota(jnp.int32, (1, SUBLANE, 1), 1)
    tile_idx = lax.broadcasted_iota(jnp.int32, (ntile, 1), 0)

    def scan_rows(acc_a, acc_h, axis, shifts, pos, reverse):
        n = acc_a.shape[axis]
        for sh in shifts:
            if reverse:
                sa = pltpu.roll(acc_a, n - sh, axis=axis)
                shh = pltpu.roll(acc_h, n - sh, axis=axis)
                keep = pos < n - sh
            else:
                sa = pltpu.roll(acc_a, sh, axis=axis)
                shh = pltpu.roll(acc_h, sh, axis=axis)
                keep = pos >= sh
            acc_h = acc_h + acc_a * jnp.where(keep, shh, 0.0)
            acc_a = acc_a * jnp.where(keep, sa, 1.0)
        return acc_a, acc_h

    tile_shifts = (1, 2, 4)
    summary_shifts = tuple(1 << e for e in range(ntile.bit_length() - 1))
    assert 1 << len(summary_shifts) == ntile

    def local_scan(c, carry):
        t0 = pl.multiple_of(c * B_CHUNK, B_CHUNK)
        win = xs_ref[pl.ds(t0, B_CHUNK + 2 * B_PAD), :]
        n = B_CHUNK + 2 * B_PAD
        xc = (cb_ref[...]
              + pltpu.roll(win, 1, axis=0) * cw_ref[0:1, :]
              + win * cw_ref[1:2, :]
              + pltpu.roll(win, n - 1, axis=0) * cw_ref[2:3, :]
              + pltpu.roll(win, n - 2, axis=0) * cw_ref[3:4, :])
        xc = xc[B_PAD:B_PAD + B_CHUNK, :]
        ri = jnp.dot(xc.astype(BF16), w_ref[...], preferred_element_type=F32) + bias_ref[...]
        rows = pl.ds(t0, B_CHUNK)
        for direction in (0, 1):
            r = _sigmoid(ri[:, (2 * direction) * LANE:(2 * direction + 1) * LANE])
            i = _sigmoid(ri[:, (2 * direction + 1) * LANE:(2 * direction + 2) * LANE])
            log_a = (-B_C) * r * sp_ref[direction:direction + 1, :]
            a = jnp.exp(log_a)
            u = jnp.sqrt(-jnp.tanh(log_a) * (a * a + 1.0)) * (i * xc)
            tiles = (B_CHUNK // SUBLANE, SUBLANE, LANE)
            acc_a, acc_h = scan_rows(a.reshape(tiles), u.reshape(tiles), 1, tile_shifts, row_in_tile,
                                     direction == 1)
            a_ref[direction, rows, :] = acc_a.reshape(B_CHUNK, LANE)
            h_ref[direction, rows, :] = acc_h.reshape(B_CHUNK, LANE)
        return carry

    lax.fori_loop(0, nchunk, local_scan, 0)

    for direction in (0, 1):
        reverse = direction == 1
        edge = 0 if reverse else SUBLANE - 1
        sum_a = a_ref[direction, pl.ds(edge, ntile, stride=SUBLANE), :]
        sum_h = h_ref[direction, pl.ds(edge, ntile, stride=SUBLANE), :]
        _, inc_h = scan_rows(sum_a, sum_h, 0, summary_shifts, tile_idx, reverse)
        if reverse:
            state_in = jnp.where(tile_idx < ntile - 1, pltpu.roll(inc_h, ntile - 1, axis=0), 0.0)
        else:
            state_in = jnp.where(tile_idx >= 1, pltpu.roll(inc_h, 1, axis=0), 0.0)
        for r in range(SUBLANE):
            c_ref[direction, pl.ds(r, ntile, stride=SUBLANE), :] = state_in

    def apply_state(c, carry):
        rows = pl.ds(pl.multiple_of(c * B_CHUNK, B_CHUNK), B_CHUNK)
        h = (h_ref[0, rows, :] + a_ref[0, rows, :] * c_ref[0, rows, :]
             + h_ref[1, rows, :] + a_ref[1, rows, :] * c_ref[1, rows, :])
        o_ref[rows, :] = (h * _silu(g_ref[rows, :].astype(F32))).astype(o_ref.dtype)
        return carry

    lax.fori_loop(0, nchunk, apply_state, 0)


def _mixer_b(p3, conv_w, conv_b, w_gate, b_gate, softplus_neg_lam):
    b, s, _ = p3.shape
    ngrp = B_WIDTH // LANE
    assert s % B_CHUNK == 0
    return pl.pallas_call(
        _b_kernel,
        grid=(b, ngrp),
        in_specs=[pl.BlockSpec((None, s, LANE), lambda bi, j: (bi, 0, U_BX + j)),
                  pl.BlockSpec((None, s, LANE), lambda bi, j: (bi, 0, U_BG + j)),
                  pl.BlockSpec((4, LANE), lambda bi, j: (0, j)),
                  pl.BlockSpec((1, LANE), lambda bi, j: (0, j)),
                  pl.BlockSpec((None, LANE, 4 * LANE), lambda bi, j: (j, 0, 0)),
                  pl.BlockSpec((None, 1, 4 * LANE), lambda bi, j: (j, 0, 0)),
                  pl.BlockSpec((2, LANE), lambda bi, j: (0, j))],
        out_specs=pl.BlockSpec((None, s, LANE), lambda bi, j: (bi, 0, j)),
        out_shape=jax.ShapeDtypeStruct((b, s, B_WIDTH), BF16),
        scratch_shapes=[pltpu.VMEM((s + 2 * B_PAD, LANE), F32)] + [pltpu.VMEM((2, s, LANE), F32)] * 3,
        compiler_params=_cparams("parallel", "parallel"),
        name="mixer_b",
    )(p3, p3, conv_w, conv_b, w_gate, b_gate, softplus_neg_lam)


def _b_gate_weights(w_r, b_r, w_i, b_i):
    per = LANE // B_BLOCK_DIM
    ngrp = B_WIDTH // LANE

    def blockdiag(w):
        w = w.reshape(2, ngrp, per, B_BLOCK_DIM, B_BLOCK_DIM)
        eye = jnp.eye(per, dtype=w.dtype)
        full = jnp.einsum("dgpcx,pq->dgpcqx", w, eye).reshape(2, ngrp, LANE, LANE)
        return full.transpose(1, 0, 2, 3)

    w = jnp.concatenate([blockdiag(w_r), blockdiag(w_i)], axis=-1)
    w = w.transpose(0, 2, 1, 3).reshape(ngrp, LANE, 4 * LANE).astype(BF16)
    bias = jnp.concatenate([b_r.reshape(2, ngrp, LANE), b_i.reshape(2, ngrp, LANE)], axis=-1)
    return w, bias.transpose(1, 0, 2).reshape(ngrp, 1, 4 * LANE)


def _attend_t(streams, tk, nk, s_ref, mt_ref, m_ref, acc_ref, tile_of=None):
    if isinstance(nk, int):
        assert nk % 2 == 0 and nk >= 2
    if tile_of is None:
        tile_of = lambda u: u
    for i in range(len(streams)):
        m_ref[i] = jnp.full(m_ref.shape[1:], NEG, F32)
        acc_ref[i] = jnp.zeros(acc_ref.shape[1:], F32)

    def key_rows(u):
        return pl.ds(pl.multiple_of(tile_of(u) * tk, tk), tk)

    def scores(u, slot):
        rows = key_rows(u)
        for i, st in enumerate(streams):
            s = jnp.dot(st["k_ref"][rows, :], st["q"](u), preferred_element_type=F32)
            s_ref[slot, i] = s
            mt_ref[slot, i] = jnp.max(s, axis=0, keepdims=True)

    def accumulate(u, slot, first=False):
        rows = key_rows(u)
        for i, st in enumerate(streams):
            s = s_ref[slot, i]
            if first and st.get("bias0") is not None:
                s = s + st["bias0"][...]
                m_tile = jnp.max(s, axis=0, keepdims=True)
            else:
                m_tile = mt_ref[slot, i]
            off = st["off"](u) if st.get("off") is not None else None
            m_old = m_ref[i]
            m_new = jnp.maximum(m_old, m_tile if off is None else m_tile + off)
            alpha = jnp.exp2(m_old - m_new)
            p = jnp.exp2(s - (m_new if off is None else m_new - off)).astype(BF16)
            acc_ref[i] = alpha * acc_ref[i] + jnp.dot(st["vt_ref"][:, rows], p, preferred_element_type=F32)
            m_ref[i] = m_new

    scores(0, 0)
    scores(1, 1)
    accumulate(0, 0, first=True)

    def body(j, carry):
        u = 1 + 2 * j
        scores(u + 1, 0)
        accumulate(u, 1)
        scores(u + 2, 1)
        accumulate(u + 1, 0)
        return carry

    pairs = (nk - 2) // 2 if isinstance(nk, int) else lax.shift_right_logical(nk - 2, 1)
    lax.fori_loop(0, pairs, body, 0)
    accumulate(nk - 1, 1)


C_AUG = 3
C_VROWS = C_VDIM + 16
C_MAPS = ((0, C_HALF), (1, 0))
C_UNDERFLOW = 128.0


def _c_kernel(q_ref, k_ref, v_ref, g_ref, lam_ref, subln_ref, o_ref,
              kaug_ref, vt_ref, dbias_ref, knorm_ref, s_ref, mt_ref, m_ref, acc_ref, *, tk, slopes, lam_init):
    h = pl.program_id(1)
    qi = pl.program_id(2)
    tq = q_ref.shape[0]
    s_len = k_ref.shape[0]
    nk = s_len // tk
    assert tq == tk
    lp = lam_ref[...]
    lam = (jnp.exp(jnp.sum(lp[0:1, :] * lp[1:2, :], axis=-1, keepdims=True))
           - jnp.exp(jnp.sum(lp[2:3, :] * lp[3:4, :], axis=-1, keepdims=True)) + lam_init)
    slope = jnp.where(h == 0, slopes[0], jnp.where(h == 1, slopes[1], jnp.where(h == 2, slopes[2], slopes[3])))
    c = slope.astype(F32) * LOG2E

    @pl.when(qi == 0)
    def _build_key_side():
        lane = lax.broadcasted_iota(jnp.int32, (1, LANE), 1)
        knorm2 = [jnp.zeros((tk, 1), F32), jnp.zeros((tk, 1), F32)]
        for ch in range(nk):
            rows = slice(ch * tk, (ch + 1) * tk)
            k = k_ref[rows, :]
            ksq = k.astype(F32) * k.astype(F32)
            for cmap in range(2):
                keep = (lane < C_HALF) if cmap == 0 else (lane >= C_HALF)
                knorm2[cmap] = jnp.maximum(knorm2[cmap], jnp.sum(jnp.where(keep, ksq, 0.0), axis=1, keepdims=True))
            jpos = (ch * tk + lax.broadcasted_iota(jnp.int32, (tk, 1), 0)).astype(F32)
            cj = c * jpos
            hi = cj.astype(BF16).astype(F32)
            mid = (cj - hi).astype(BF16).astype(F32)
            lo = (cj - hi - mid).astype(BF16).astype(F32)
            for cmap, base in C_MAPS:
                aug = jnp.where(lane == base, hi, jnp.where(lane == base + 1, mid,
                                                            jnp.where(lane == base + 2, lo, 0.0)))
                keep = (lane < C_HALF) if cmap == 0 else (lane >= C_HALF)
                kaug_ref[cmap, rows, :] = jnp.where(keep, k, aug.astype(BF16))
            vt_ref[0:C_VDIM, rows] = v_ref[rows, :].astype(F32).T.astype(BF16)
        r16 = lax.broadcasted_iota(jnp.int32, (C_VROWS - C_VDIM, s_len), 0)
        vt_ref[C_VDIM:C_VROWS, :] = jnp.where(r16 == 0, 1.0, 0.0).astype(BF16)
        ii = lax.broadcasted_iota(jnp.int32, (tk, tq), 1)
        jj = lax.broadcasted_iota(jnp.int32, (tk, tq), 0)
        dbias_ref[...] = -c * jnp.abs(ii - jj).astype(F32)
        for cmap in range(2):
            knorm_ref[cmap] = jnp.broadcast_to(jnp.max(knorm2[cmap], axis=0, keepdims=True), knorm_ref.shape[1:])

    q_t = (q_ref[...].astype(F32) * (C_HALF ** -0.5 * LOG2E)).T
    row = lax.broadcasted_iota(jnp.int32, (LANE, 1), 0)
    ci = c * (qi * tq + lax.broadcasted_iota(jnp.int32, (1, tq), 1)).astype(F32)

    qk_bound = jnp.zeros((1, 1), F32)
    for cmap in range(2):
        keep = (row < C_HALF) if cmap == 0 else (row >= C_HALF)
        qn2 = jnp.max(jnp.sum(jnp.where(keep, q_t * q_t, 0.0), axis=0, keepdims=True), axis=1, keepdims=True)
        qk_bound = jnp.maximum(qk_bound, jnp.sqrt(qn2 * knorm_ref[cmap][0:1, 0:1]))
    qk_bound = qk_bound * 1.01
    reach = ((2.0 * qk_bound + C_UNDERFLOW) / c - 1.0) * (1.0 / tk)
    reach = jnp.minimum(jnp.ceil(jnp.maximum(reach, 0.0)), float(nk)).astype(jnp.int32)
    w = jnp.max(reach)
    lo = jnp.maximum(qi - w, 0)
    hi = jnp.minimum(qi + w, nk - 1)
    n_win = hi - lo + 1
    odd = jnp.bitwise_and(n_win, 1)
    lo = lo - jnp.where(hi == nk - 1, odd, 0)
    n_steps = n_win + odd

    def tile_of(u):
        if isinstance(u, int) and u == 0:
            return qi
        t = lo + u - 1
        return t + (t >= qi).astype(jnp.int32)

    def sign_of(u):
        return jnp.where(tile_of(u) < qi, 1.0, -1.0).astype(F32)

    streams = []
    for cmap, base in C_MAPS:
        keep = (row < C_HALF) if cmap == 0 else (row >= C_HALF)
        q_base = jnp.where(keep, q_t, 0.0)
        aug_rows = jnp.where((row >= base) & (row < base + C_AUG), 1.0, 0.0)
        q_diag = q_base.astype(BF16)
        q_before = (q_base + aug_rows).astype(BF16)
        q_after = (q_base - aug_rows).astype(BF16)

        def q_of(u, q_diag=q_diag, q_before=q_before, q_after=q_after):
            if isinstance(u, int) and u == 0:
                return q_diag
            return jnp.where(tile_of(u) < qi, q_before, q_after)

        def off_of(u):
            if isinstance(u, int) and u == 0:
                return None
            return -sign_of(u) * ci

        streams.append(dict(q=q_of, k_ref=kaug_ref.at[cmap], vt_ref=vt_ref, off=off_of, bias0=dbias_ref))
    _attend_t(streams, tk, n_steps, s_ref, mt_ref, m_ref, acc_ref, tile_of)

    o_t = []
    for cmap in range(2):
        acc = acc_ref[cmap]
        o_t.append(acc[:C_VDIM, :] / acc[C_VDIM:C_VDIM + 1, :])
    o = (o_t[0] - lam * o_t[1]).T
    ms = jnp.mean(o * o, axis=-1, keepdims=True)
    o = o * lax.rsqrt(ms + EPS) * subln_ref[...] * (1.0 - lam_init)
    o_ref[...] = (o * _silu(g_ref[...].astype(F32))).astype(o_ref.dtype)


def _mixer_c(p3, lam_params, subln, layer, tq, tk):
    b, s, _ = p3.shape
    lam_init = 0.8 - 0.6 * math.exp(-0.3 * layer)
    slopes = tuple(2.0 ** (-8.0 * (i + 1) / C_HEADS) for i in range(C_HEADS))
    kern = functools.partial(_c_kernel, tk=tk, slopes=slopes, lam_init=lam_init)
    return pl.pallas_call(
        kern,
        grid=(b, C_HEADS, s // tq),
        in_specs=[pl.BlockSpec((None, tq, LANE), lambda bi, h, i: (bi, i, U_CQ + h)),
                  pl.BlockSpec((None, s, LANE), lambda bi, h, i: (bi, 0, U_CK + h)),
                  pl.BlockSpec((None, s, LANE), lambda bi, h, i: (bi, 0, U_CV + h)),
                  pl.BlockSpec((None, tq, LANE), lambda bi, h, i: (bi, i, U_CG + h)),
                  pl.BlockSpec((4, C_HALF), lambda bi, h, i: (0, 0)),
                  pl.BlockSpec((1, C_VDIM), lambda bi, h, i: (0, 0))],
        out_specs=pl.BlockSpec((None, tq, LANE), lambda bi, h, i: (bi, i, h)),
        out_shape=jax.ShapeDtypeStruct((b, s, C_WIDTH), BF16),
        scratch_shapes=[pltpu.VMEM((2, s, LANE), BF16), pltpu.VMEM((C_VROWS, s), BF16),
                        pltpu.VMEM((tk, tq), F32), pltpu.VMEM((2, SUBLANE, LANE), F32),
                        pltpu.VMEM((2, 2, tk, tq), F32),
                        pltpu.VMEM((2, 2, 1, tq), F32), pltpu.VMEM((2, 1, tq), F32),
                        pltpu.VMEM((2, C_VROWS, tq), F32)],
        compiler_params=_cparams("parallel", "parallel", "arbitrary"),
        name="mixer_c",
    )(p3, p3, p3, p3, lam_params, subln)


def _dprep_kernel(cq_ref, ckv_ref, kr_ref, qn_ref, kvn_ref, wqt_ref, wqst_ref, wk_ref, wvt_ref,
                  e1_ref, e2_ref, cos_ref, sin_ref, cost_ref, sint_ref, vonet_ref, qt_out, k_out, vt_out):
    def norm(x_ref, gain_ref):
        x = x_ref[...].astype(F32)
        ms = jnp.mean(x * x, axis=-1, keepdims=True)
        return (x * lax.rsqrt(ms + EPS) * gain_ref[...]).astype(BF16)

    cqn = norm(cq_ref, qn_ref)
    ckvn = norm(ckv_ref, kvn_ref)
    kr = kr_ref[...]
    cos = jnp.concatenate([cos_ref[...]] * D_HEADS, axis=1)
    sin = jnp.concatenate([sin_ref[...]] * D_HEADS, axis=1)
    cos_t = jnp.concatenate([cost_ref[...]] * D_HEADS, axis=0)
    sin_t = jnp.concatenate([sint_ref[...]] * D_HEADS, axis=0)
    q_t = (lax.dot_general(wqt_ref[...], cqn, _NT, preferred_element_type=F32) * cos_t
           + lax.dot_general(wqst_ref[...], cqn, _NT, preferred_element_type=F32) * sin_t)
    qt_out[...] = (q_t * ((D_NOPE + D_ROPE) ** -0.5 * LOG2E)).astype(qt_out.dtype)
    k = (jnp.dot(ckvn, wk_ref[...], preferred_element_type=F32)
         + jnp.dot(kr, e1_ref[...], preferred_element_type=F32) * cos
         + jnp.dot(kr, e2_ref[...], preferred_element_type=F32) * sin)
    k_out[...] = k.astype(k_out.dtype)
    v_t = lax.dot_general(wvt_ref[...], ckvn, _NT, preferred_element_type=F32) + vonet_ref[...]
    vt_out[...] = v_t.astype(vt_out.dtype)


def _d_tables(s):
    inv = ROPE_BASE ** (-jnp.arange(0, D_ROPE, 2, dtype=F32) / D_ROPE)
    ang = jnp.arange(s, dtype=F32)[:, None] * inv[None, :]
    cos, sin = jnp.cos(ang), jnp.sin(ang)
    pad = LANE - D_NOPE - D_ROPE
    cos_t = jnp.concatenate([jnp.ones((s, D_NOPE), F32), cos, cos, jnp.zeros((s, pad), F32)], axis=1)
    sin_t = jnp.concatenate([jnp.zeros((s, D_NOPE), F32), -sin, sin, jnp.zeros((s, pad), F32)], axis=1)
    return cos_t, sin_t


def _d_weights(w_uq, w_ukv):
    half = D_ROPE // 2
    pad = LANE - D_NOPE - D_ROPE
    wq = w_uq.reshape(D_QLR, D_HEADS, D_NOPE + D_ROPE)
    zq = jnp.zeros((D_QLR, D_HEADS, pad), w_uq.dtype)
    wq_main = jnp.concatenate([wq, zq], axis=-1).reshape(D_QLR, D_HEADS * LANE)
    wq_swap = jnp.concatenate([jnp.zeros((D_QLR, D_HEADS, D_NOPE), w_uq.dtype),
                               wq[..., D_NOPE + half:], wq[..., D_NOPE:D_NOPE + half], zq],
                              axis=-1).reshape(D_QLR, D_HEADS * LANE)
    wkv = w_ukv.reshape(D_KVLR, D_HEADS, D_NOPE + D_VDIM)
    zk = jnp.zeros((D_KVLR, D_HEADS, LANE - D_NOPE), w_ukv.dtype)
    wk = jnp.concatenate([wkv[..., :D_NOPE], zk], axis=-1).reshape(D_KVLR, D_HEADS * LANE)
    wv = jnp.concatenate([wkv[..., D_NOPE:], jnp.zeros((D_KVLR, D_HEADS, D_VROWS - D_VDIM), w_ukv.dtype)],
                         axis=-1).reshape(D_KVLR, D_HEADS * D_VROWS)
    src = jnp.arange(D_ROPE)
    e1 = jnp.zeros((LANE, LANE), F32).at[src, D_NOPE + src].set(1.0)
    e2 = jnp.zeros((LANE, LANE), F32).at[(src + half) % D_ROPE, D_NOPE + src].set(1.0)
    e1 = jnp.tile(e1, (1, D_HEADS))
    e2 = jnp.tile(e2, (1, D_HEADS))
    vone = jnp.zeros((D_VROWS, 1), F32).at[D_VDIM, 0].set(1.0)
    vone = jnp.tile(vone, (D_HEADS, 1))
    return (wq_main.T.astype(BF16), wq_swap.T.astype(BF16), wk.astype(BF16), wv.T.astype(BF16),
            e1.astype(BF16), e2.astype(BF16), vone)


def _dprep(p3, q_norm, kv_norm, dw, cos_t, sin_t, tm):
    b, s, _ = p3.shape
    wqt, wqst, wk, wvt, e1, e2, vonet = dw
    wide = D_HEADS * LANE
    vwide = D_HEADS * D_VROWS
    full = lambda shape: pl.BlockSpec(shape, lambda bi, i: (0,) * len(shape))
    tok_major = pl.BlockSpec((None, tm, wide), lambda bi, i: (bi, i, 0))
    feat_major = lambda rows: pl.BlockSpec((None, rows, tm), lambda bi, i: (bi, 0, i))
    return pl.pallas_call(
        _dprep_kernel,
        grid=(b, s // tm),
        in_specs=[pl.BlockSpec((None, tm, D_QLR), lambda bi, i: (bi, i, U_DCQ // 2)),
                  pl.BlockSpec((None, tm, LANE), lambda bi, i: (bi, i, U_DCKV)),
                  pl.BlockSpec((None, tm, LANE), lambda bi, i: (bi, i, U_DKR)),
                  full((1, D_QLR)), full((1, D_KVLR)),
                  full((wide, D_QLR)), full((wide, D_QLR)), full((D_KVLR, wide)), full((vwide, D_KVLR)),
                  full((LANE, wide)), full((LANE, wide)),
                  pl.BlockSpec((tm, LANE), lambda bi, i: (i, 0)),
                  pl.BlockSpec((tm, LANE), lambda bi, i: (i, 0)),
                  pl.BlockSpec((LANE, tm), lambda bi, i: (0, i)),
                  pl.BlockSpec((LANE, tm), lambda bi, i: (0, i)),
                  full((vwide, 1))],
        out_specs=[feat_major(wide), tok_major, feat_major(vwide)],
        out_shape=[jax.ShapeDtypeStruct((b, wide, s), BF16), jax.ShapeDtypeStruct((b, s, wide), BF16),
                   jax.ShapeDtypeStruct((b, vwide, s), BF16)],
        compiler_params=_cparams("parallel", "parallel"),
        name="mixer_d_prep",
    )(p3, p3, p3, q_norm, kv_norm, wqt, wqst, wk, wvt, e1, e2, cos_t, sin_t, cos_t.T, sin_t.T, vonet)


def _d_kernel(qt_ref, k_ref, vt_ref, g_ref, o_ref, s_ref, mt_ref, m_ref, acc_ref, *, tk):
    streams = []
    for h in range(2):
        sl = slice(h * LANE, (h + 1) * LANE)
        vrows = slice(h * D_VROWS, (h + 1) * D_VROWS)
        streams.append(dict(q=lambda u, q=qt_ref[sl, :]: q, k_ref=k_ref.at[:, sl], vt_ref=vt_ref.at[vrows, :]))
    _attend_t(streams, tk, k_ref.shape[0] // tk, s_ref, mt_ref, m_ref, acc_ref)
    parts = []
    for h in range(2):
        acc = acc_ref[h]
        parts.append(acc[:D_VDIM, :] / acc[D_VDIM:D_VDIM + 1, :])
    o = jnp.concatenate(parts, axis=0).T
    o_ref[...] = (o * _silu(g_ref[...].astype(F32))).astype(o_ref.dtype)


def _mixer_d(p3, qt, kd, vt, tq, tk):
    b, s, _ = p3.shape
    npair = D_HEADS // 2
    return pl.pallas_call(
        functools.partial(_d_kernel, tk=tk),
        grid=(b, npair, s // tq),
        in_specs=[pl.BlockSpec((None, 2 * LANE, tq), lambda bi, hp, i: (bi, hp, i)),
                  pl.BlockSpec((None, s, 2 * LANE), lambda bi, hp, i: (bi, 0, hp)),
                  pl.BlockSpec((None, 2 * D_VROWS, s), lambda bi, hp, i: (bi, hp, 0)),
                  pl.BlockSpec((None, tq, LANE), lambda bi, hp, i: (bi, i, U_DG + hp))],
        out_specs=pl.BlockSpec((None, tq, LANE), lambda bi, hp, i: (bi, i, hp)),
        out_shape=jax.ShapeDtypeStruct((b, s, D_WIDTH), BF16),
        scratch_shapes=[pltpu.VMEM((2, 2, tk, tq), F32), pltpu.VMEM((2, 2, 1, tq), F32),
                        pltpu.VMEM((2, 1, tq), F32), pltpu.VMEM((2, D_VROWS, tq), F32)],
        compiler_params=_cparams("parallel", "parallel", "arbitrary"),
        name="mixer_d",
    )(qt, kd, vt, p3)


def _out_kernel(x_ref, ya_ref, yb_ref, yc_ref, yd_ref, g0_ref, g1_ref, g2_ref, g3_ref,
                wa_ref, wb_ref, wc_ref, wd_ref, bg_ref, wo_ref, np_ref, o_ref):
    merged = None
    for i, (y_ref, w_ref, g_ref) in enumerate(((ya_ref, wa_ref, g0_ref), (yb_ref, wb_ref, g1_ref),
                                                (yc_ref, wc_ref, g2_ref), (yd_ref, wd_ref, g3_ref))):
        t = jnp.dot(y_ref[...], w_ref[...], preferred_element_type=F32)
        gate = _sigmoid(g_ref[...].astype(F32) + bg_ref[i:i + 1, :])
        merged = gate * t if merged is None else merged + gate * t
    o = jnp.dot(merged.astype(BF16), wo_ref[...], preferred_element_type=F32)
    ms = jnp.mean(o * o, axis=-1, keepdims=True)
    o_ref[...] = x_ref[...] + o * lax.rsqrt(ms + EPS) * np_ref[...]


def _merge_out(x2, p2, ya, yb, yc, yd, wa, wb, wc, wd, b_gate, w_out, norm_post, layer, tm):
    n = x2.shape[0]
    row = lambda width: pl.BlockSpec((tm, width), lambda i: (i, 0))
    full = lambda shape: pl.BlockSpec((None,) + shape, lambda i: (layer, 0, 0))
    gate = lambda br: pl.BlockSpec((tm, D_MODEL), lambda i, br=br: (i, U_GATE * LANE // D_MODEL + br))
    return pl.pallas_call(
        _out_kernel,
        grid=(n // tm,),
        in_specs=[row(D_MODEL), row(A_WIDTH), row(B_WIDTH), row(C_WIDTH), row(D_WIDTH),
                  gate(0), gate(1), gate(2), gate(3),
                  full((A_WIDTH, D_MODEL)), full((B_WIDTH, D_MODEL)), full((C_WIDTH, D_MODEL)),
                  full((D_WIDTH, D_MODEL)), full((N_BRANCH, D_MODEL)), full((D_MODEL, D_MODEL)),
                  full((1, D_MODEL))],
        out_specs=row(D_MODEL),
        out_shape=jax.ShapeDtypeStruct((n, D_MODEL), F32),
        compiler_params=_cparams("parallel"),
        name="merge_out",
    )(x2, ya, yb, yc, yd, p2, p2, p2, p2, wa, wb, wc, wd, b_gate, w_out, norm_post)


def kernel(x, norm_pre, norm_post, w_in, conv_w, conv_b, lru_wr, lru_br, lru_wi, lru_bi, lru_lambda,
           diff_lam_q1, diff_lam_k1, diff_lam_q2, diff_lam_k2, diff_subln, mla_q_norm, mla_kv_norm,
           mla_w_uq, mla_w_ukv, w_br_a, w_br_b, w_br_c, w_br_d, b_gate, w_out):
    b, s, d = x.shape
    depth = w_in.shape[0]
    n = b * s
    tm_in = min(2048, n)
    tm_out = min(512, n)
    tq = min(512, s)
    tk = min(512, s)

    w_in_t = jnp.swapaxes(w_in, 1, 2)
    out_weights = [w.astype(BF16) for w in (w_br_a, w_br_b, w_br_c, w_br_d)]
    w_out_bf = w_out.astype(BF16)
    a_bias = _a_bias_tables(jnp.asarray([2.0 ** (-8.0 * (i + 1) / A_SLOTS) for i in range(A_SLOTS)], F32))
    cos_t, sin_t = _d_tables(s)
    softplus_neg_lam = jnp.log1p(jnp.exp(-lru_lambda.astype(F32)))

    x2 = x.reshape(n, d)
    for l in range(depth):
        p2 = _inproj(x2, norm_pre[l][None, :], w_in_t, l, tm_in)
        p3 = p2.reshape(b, s, P_WIDTH)
        ya = _mixer_a(p3, a_bias)
        bw, bb = _b_gate_weights(lru_wr[l], lru_br[l], lru_wi[l], lru_bi[l])
        yb = _mixer_b(p3, conv_w[l], conv_b[l][None, :], bw, bb, softplus_neg_lam[l])
        lam_params = jnp.stack([diff_lam_q1[l], diff_lam_k1[l], diff_lam_q2[l], diff_lam_k2[l]])
        yc = _mixer_c(p3, lam_params, diff_subln[l][None, :], l, tq, tk)
        dw = _d_weights(mla_w_uq[l], mla_w_ukv[l])
        qd, kd, vd = _dprep(p3, mla_q_norm[l][None, :], mla_kv_norm[l][None, :], dw, cos_t, sin_t, min(1024, s))
        yd = _mixer_d(p3, qd, kd, vd, min(2 * tq, s), tk)
        x2 = _merge_out(x2, p2, ya.reshape(n, -1), yb.reshape(n, -1), yc.reshape(n, -1), yd.reshape(n, -1),
                        *out_weights, b_gate, w_out_bf, norm_post[:, None, :], l, tm_out)
    return x2.reshape(b, s, d)
```

```python
import functools
import math

import jax
import jax.numpy as jnp
from jax import lax
from jax.experimental import pallas as pl
from jax.experimental.pallas import tpu as pltpu

F32 = jnp.float32
BF16 = jnp.bfloat16

D_MODEL = 1024
EPS = 1e-6
N_BRANCH = 4

A_PATTERNS = ((128, 1), (512, 4), (2048, 16))
A_SLOTS = 6
A_HEAD_DIM = 64
A_QKV = 1152
A_WIDTH = 384
A_RADIUS = 64
A_QBLK = 128
A_KWIN = 256
A_UNROLL = 16

B_WIDTH = 384
B_BLOCK_DIM = 64
B_C = 8.0

C_HEADS = 4
C_HALF = 64
C_VDIM = 128
C_QK = 512
C_WIDTH = 512

D_HEADS = 6
D_NOPE = 64
D_ROPE = 32
D_VDIM = 64
D_QLR = 256
D_KVLR = 128
D_WIDTH = 384
D_VROWS = D_VDIM + 16
ROPE_BASE = 10000.0

LANE = 128
SUBLANE = 8
NEG = -1e30
LOG2E = math.log2(math.e)
_NT = (((1,), (1,)), ((), ()))
VMEM_LIMIT = 56 * 1024 * 1024

IN_TN = 512
IN_MAIN_COLS = 3 * A_QKV + A_WIDTH + 2 * B_WIDTH + 2 * C_QK + 2 * C_WIDTH + D_QLR + D_KVLR
IN_DG_COL = IN_MAIN_COLS + D_ROPE
IN_GATE_COL = IN_DG_COL + D_WIDTH
IN_TAIL_TILES = (N_BRANCH * D_MODEL + D_WIDTH + LANE) // IN_TN
IN_MAIN_TILES = -(-(IN_MAIN_COLS + D_ROPE) // IN_TN)
U_GATE, U_DG = 0, N_BRANCH * D_MODEL // LANE
U_MAIN = IN_TAIL_TILES * IN_TN // LANE
U_AQ, U_AK, U_AV, U_AG = U_MAIN, U_MAIN + 9, U_MAIN + 18, U_MAIN + 27
U_BX, U_BG = U_MAIN + 30, U_MAIN + 33
U_CQ, U_CK, U_CV, U_CG = U_MAIN + 36, U_MAIN + 40, U_MAIN + 44, U_MAIN + 48
U_DCQ, U_DCKV, U_DKR = U_MAIN + 52, U_MAIN + 54, U_MAIN + 55
U_TOTAL = U_MAIN + IN_MAIN_TILES * IN_TN // LANE
P_WIDTH = U_TOTAL * LANE
assert U_TOTAL == 92 and U_DCQ % 2 == 0 and IN_GATE_COL + N_BRANCH * D_MODEL == 11552


def _cparams(*sem):
    return pltpu.CompilerParams(dimension_semantics=sem, vmem_limit_bytes=VMEM_LIMIT)


def _sigmoid(x):
    return 0.5 * jnp.tanh(0.5 * x) + 0.5


def _silu(x):
    return x * _sigmoid(x)


def _inproj_kernel(x_ref, g_ref, wtail_ref, wmain_ref, o_ref, h_ref):
    j = pl.program_id(1)

    @pl.when(j == 0)
    def _():
        x = x_ref[...]
        ms = jnp.mean(x * x, axis=-1, keepdims=True)
        h_ref[...] = (x * lax.rsqrt(ms + EPS) * g_ref[...]).astype(BF16)

    @pl.when(j < IN_TAIL_TILES)
    def _():
        o_ref[...] = lax.dot_general(h_ref[...], wtail_ref[0].astype(BF16), _NT,
                                     preferred_element_type=F32).astype(o_ref.dtype)

    @pl.when(j >= IN_TAIL_TILES)
    def _():
        o_ref[...] = lax.dot_general(h_ref[...], wmain_ref[...].astype(BF16), _NT,
                                     preferred_element_type=F32).astype(o_ref.dtype)


def _inproj(x2, gain, w_in_t, layer, tm):
    n = x2.shape[0]
    gate_tiles = IN_TAIL_TILES - 1

    def tail_row(j):
        per = IN_TN // D_ROPE
        unit = jnp.where(j < gate_tiles, IN_GATE_COL // D_ROPE + jnp.minimum(j, gate_tiles - 1) * per,
                         IN_DG_COL // D_ROPE)
        return unit * D_ROPE

    return pl.pallas_call(
        _inproj_kernel,
        grid=(n // tm, P_WIDTH // IN_TN),
        in_specs=[pl.BlockSpec((tm, D_MODEL), lambda i, j: (i, 0)),
                  pl.BlockSpec((1, D_MODEL), lambda i, j: (0, 0)),
                  pl.BlockSpec((pl.Element(1), pl.Element(IN_TN), pl.Element(D_MODEL)),
                               lambda i, j: (layer, tail_row(j), 0)),
                  pl.BlockSpec((None, IN_TN, D_MODEL),
                               lambda i, j: (layer, jnp.maximum(j - IN_TAIL_TILES, 0), 0))],
        out_specs=pl.BlockSpec((tm, IN_TN), lambda i, j: (i, j)),
        out_shape=jax.ShapeDtypeStruct((n, P_WIDTH), BF16),
        scratch_shapes=[pltpu.VMEM((tm, D_MODEL), BF16)],
        compiler_params=_cparams("parallel", "arbitrary"),
        name="inproj",
    )(x2, gain, w_in_t, w_in_t)


def _a_bias_tables(slopes):
    ii = jnp.arange(A_QBLK, dtype=jnp.int32)[:, None]
    jj = jnp.arange(A_KWIN, dtype=jnp.int32)[None, :]
    out = []
    for _, dil in A_PATTERNS:
        per_edge = []
        for off in (0, A_RADIUS, A_QBLK):
            rel = jnp.abs(off + ii - jj)
            dist = (rel * dil).astype(F32)
            b = -slopes[:, None, None] * dist[None] * LOG2E
            per_edge.append(jnp.where((rel <= A_RADIUS)[None], b, NEG))
        out.append(jnp.stack(per_edge))
    return jnp.stack(out)


def _a_pitch(dil):
    return dil + SUBLANE if dil % (2 * SUBLANE) == 0 else dil


def _a_kernel(q0, k0, v0, q1, k1, v1, q2, k2, v2, gate_ref, bias_ref, o_ref,
              qf, kf, vf, u_ref, z_ref, m_ref, up_ref, zp_ref, mp_ref):
    s_len = q0.shape[0]
    lane = lax.broadcasted_iota(jnp.int32, (1, LANE), 1)
    first = lane < A_HEAD_DIM
    ones_first = jnp.where(first, 1.0, 0.0).astype(BF16)
    ones_second = jnp.where(first, 0.0, 1.0).astype(BF16)
    scale = A_HEAD_DIM ** -0.5 * LOG2E
    padded_groups = [g for g, (_, dil) in enumerate(A_PATTERNS) if _a_pitch(dil) != dil]
    assert padded_groups == [len(A_PATTERNS) - 1]

    for g, ((_, dil), (qr, kr, vr)) in enumerate(zip(A_PATTERNS, ((q0, k0, v0), (q1, k1, v1), (q2, k2, v2)))):
        sub_len = s_len // dil
        nqb = sub_len // A_QBLK
        pitch = _a_pitch(dil)
        if pitch != dil:
            def stage(l2, carry, dil=dil, pitch=pitch, qr=qr, kr=kr, vr=vr):
                src = pl.ds(pl.multiple_of(l2 * 2 * dil, 2 * dil), 2 * dil)
                dst = pl.multiple_of(l2 * 2 * pitch, SUBLANE)
                for ref, buf, mul in ((qr, qf, scale), (kr, kf, None), (vr, vf, None)):
                    x = ref[src, :].astype(F32)
                    x = x if mul is None else x * mul
                    buf[pl.ds(dst, dil), :] = x[:dil]
                    buf[pl.ds(dst + pitch, dil), :] = x[dil:]
                return carry

            lax.fori_loop(0, sub_len // 2, stage, 0, unroll=4)
        elif dil > 1:
            rows = pl.ds(0, s_len)
            qf[rows, :] = qr[...].astype(F32) * scale
            kf[rows, :] = kr[...].astype(F32)
            vf[rows, :] = vr[...].astype(F32)

        def block(idx, carry, g=g, dil=dil, pitch=pitch, sub_len=sub_len, nqb=nqb, qr=qr, kr=kr, vr=vr):
            r = idx // nqb
            qb = idx % nqb
            qs = qb * A_QBLK
            ws = jnp.clip(qs - A_RADIUS, 0, sub_len - A_KWIN)
            edge = jnp.where(qb == 0, 0, jnp.where(qb == nqb - 1, 2, 1))
            if dil == 1:
                qrows = pl.ds(pl.multiple_of(qs, A_QBLK), A_QBLK)
                krows = pl.ds(pl.multiple_of(ws, A_RADIUS), A_KWIN)
                q = (qr[qrows, :].astype(F32) * scale).astype(BF16)
                k = kr[krows, :]
                v = vr[krows, :]
            else:
                qrows = pl.ds(r + qs * pitch, A_QBLK, stride=pitch)
                krows = pl.ds(r + ws * pitch, A_KWIN, stride=pitch)
                q = qf[qrows, :].astype(BF16)
                k = kf[krows, :].astype(BF16)
                v = vf[krows, :].astype(BF16)
            zq = jnp.zeros_like(q)
            zv = jnp.zeros_like(v)
            uz = None
            ms = []
            for h, head_lanes in enumerate((first, jnp.logical_not(first))):
                qh = jnp.where(head_lanes, q, zq)
                s = lax.dot_general(qh, k, (((1,), (1,)), ((), ())), preferred_element_type=F32)
                s = s + bias_ref[g, edge, h]
                mh = jnp.max(s, axis=-1, keepdims=True)
                p = jnp.exp2(s - mh).astype(BF16)
                ones_h = ones_first if h == 0 else ones_second
                vaug = jnp.concatenate([jnp.where(head_lanes, v, zv),
                                        jnp.broadcast_to(ones_h, v.shape)], axis=1)
                part = jnp.dot(p, vaug, preferred_element_type=F32)
                uz = part if uz is None else uz + part
                ms.append(mh)
            u = uz[:, :LANE]
            z = uz[:, LANE:]
            m = jnp.where(first, ms[0], ms[1])
            if g == 0:
                u_ref[qrows, :] = u
                z_ref[qrows, :] = z
                m_ref[qrows, :] = m
            elif pitch != dil:
                up_ref[qrows, :] = u
                zp_ref[qrows, :] = z
                mp_ref[qrows, :] = m
            else:
                m_old = m_ref[qrows, :]
                m_new = jnp.maximum(m_old, m)
                a = jnp.exp2(m_old - m_new)
                b = jnp.exp2(m - m_new)
                u_ref[qrows, :] = a * u_ref[qrows, :] + b * u
                z_ref[qrows, :] = a * z_ref[qrows, :] + b * z
                m_ref[qrows, :] = m_new
            return carry

        lax.fori_loop(0, dil * nqb, block, 0, unroll=A_UNROLL)

    dil = A_PATTERNS[padded_groups[0]][1]
    pitch = _a_pitch(dil)

    def finish(l2, carry):
        rows = pl.ds(pl.multiple_of(l2 * 2 * dil, 2 * dil), 2 * dil)
        src = pl.multiple_of(l2 * 2 * pitch, SUBLANE)

        def padded(ref):
            return jnp.concatenate([ref[pl.ds(src, dil), :], ref[pl.ds(src + pitch, dil), :]], axis=0)

        m_a, m_b = m_ref[rows, :], padded(mp_ref)
        m_new = jnp.maximum(m_a, m_b)
        a = jnp.exp2(m_a - m_new)
        b = jnp.exp2(m_b - m_new)
        u = a * u_ref[rows, :] + b * padded(up_ref)
        z = a * z_ref[rows, :] + b * padded(zp_ref)
        o_ref[rows, :] = (u / z * _silu(gate_ref[rows, :].astype(F32))).astype(o_ref.dtype)
        return carry

    lax.fori_loop(0, s_len // (2 * dil), finish, 0, unroll=4)


def _mixer_a(p3, bias):
    b, s, _ = p3.shape
    npair = A_SLOTS // 2
    assert s // A_PATTERNS[-1][1] >= A_KWIN
    padded_rows = max(s // dil * _a_pitch(dil) for _, dil in A_PATTERNS)

    def col(unit):
        return pl.BlockSpec((None, s, LANE), lambda bi, hp, unit=unit: (bi, 0, unit + hp))

    in_specs = []
    for g in range(len(A_PATTERNS)):
        for base in (U_AQ, U_AK, U_AV):
            in_specs.append(col(base + g * npair))
    in_specs.append(col(U_AG))
    in_specs.append(pl.BlockSpec((len(A_PATTERNS), 3, 2, A_QBLK, A_KWIN), lambda bi, hp: (0, 0, hp, 0, 0)))
    return pl.pallas_call(
        _a_kernel,
        grid=(b, npair),
        in_specs=in_specs,
        out_specs=pl.BlockSpec((None, s, LANE), lambda bi, hp: (bi, 0, hp)),
        out_shape=jax.ShapeDtypeStruct((b, s, A_WIDTH), BF16),
        scratch_shapes=([pltpu.VMEM((padded_rows, LANE), F32)] * 3 + [pltpu.VMEM((s, LANE), F32)] * 3
                        + [pltpu.VMEM((padded_rows, LANE), F32)] * 3),
        compiler_params=_cparams("parallel", "parallel"),
        name="mixer_a",
    )(*([p3] * 10), bias)


B_CHUNK = 256
B_PAD = 8


def _b_kernel(x_ref, g_ref, cw_ref, cb_ref, w_ref, bias_ref, sp_ref, o_ref, xs_ref, a_ref, h_ref, c_ref):
    s_len = x_ref.shape[0]
    nchunk = s_len // B_CHUNK
    ntile = s_len // SUBLANE
    zpad = jnp.zeros((B_PAD, LANE), F32)
    xs_ref[pl.ds(0, B_PAD), :] = zpad
    xs_ref[pl.ds(B_PAD + s_len, B_PAD), :] = zpad
    xs_ref[pl.ds(B_PAD, s_len), :] = x_ref[...].astype(F32)
    row_in_tile = lax.broadcasted_iota(jnp.int32, (1, SUBLANE, 1), 1)
    tile_idx = lax.broadcasted_iota(jnp.int32, (ntile, 1), 0)

    def scan_rows(acc_a, acc_h, axis, shifts, pos, reverse):
        n = acc_a.shape[axis]
        for sh in shifts:
            if reverse:
                sa = pltpu.roll(acc_a, n - sh, axis=axis)
                shh = pltpu.roll(acc_h, n - sh, axis=axis)
                keep = pos < n - sh
            else:
                sa = pltpu.roll(acc_a, sh, axis=axis)
                shh = pltpu.roll(acc_h, sh, axis=axis)
                keep = pos >= sh
            acc_h = acc_h + acc_a * jnp.where(keep, shh, 0.0)
            acc_a = acc_a * jnp.where(keep, sa, 1.0)
        return acc_a, acc_h

    tile_shifts = (1, 2, 4)
    summary_shifts = tuple(1 << e for e in range(ntile.bit_length() - 1))
    assert 1 << len(summary_shifts) == ntile

    def local_scan(c, carry):
        t0 = pl.multiple_of(c * B_CHUNK, B_CHUNK)
        win = xs_ref[pl.ds(t0, B_CHUNK + 2 * B_PAD), :]
        n = B_CHUNK + 2 * B_PAD
        xc = (cb_ref[...]
              + pltpu.roll(win, 1, axis=0) * cw_ref[0:1, :]
              + win * cw_ref[1:2, :]
              + pltpu.roll(win, n - 1, axis=0) * cw_ref[2:3, :]
              + pltpu.roll(win, n - 2, axis=0) * cw_ref[3:4, :])
        xc = xc[B_PAD:B_PAD + B_CHUNK, :]
        ri = jnp.dot(xc.astype(BF16), w_ref[...], preferred_element_type=F32) + bias_ref[...]
        rows = pl.ds(t0, B_CHUNK)
        for direction in (0, 1):
            r = _sigmoid(ri[:, (2 * direction) * LANE:(2 * direction + 1) * LANE])
            i = _sigmoid(ri[:, (2 * direction + 1) * LANE:(2 * direction + 2) * LANE])
            log_a = (-B_C) * r * sp_ref[direction:direction + 1, :]
            a = jnp.exp(log_a)
            u = jnp.sqrt(-jnp.tanh(log_a) * (a * a + 1.0)) * (i * xc)
            tiles = (B_CHUNK // SUBLANE, SUBLANE, LANE)
            acc_a, acc_h = scan_rows(a.reshape(tiles), u.reshape(tiles), 1, tile_shifts, row_in_tile,
                                     direction == 1)
            a_ref[direction, rows, :] = acc_a.reshape(B_CHUNK, LANE)
            h_ref[direction, rows, :] = acc_h.reshape(B_CHUNK, LANE)
        return carry

    lax.fori_loop(0, nchunk, local_scan, 0)

    for direction in (0, 1):
        reverse = direction == 1
        edge = 0 if reverse else SUBLANE - 1
        sum_a = a_ref[direction, pl.ds(edge, ntile, stride=SUBLANE), :]
        sum_h = h_ref[direction, pl.ds(edge, ntile, stride=SUBLANE), :]
        _, inc_h = scan_rows(sum_a, sum_h, 0, summary_shifts, tile_idx, reverse)
        if reverse:
            state_in = jnp.where(tile_idx < ntile - 1, pltpu.roll(inc_h, ntile - 1, axis=0), 0.0)
        else:
            state_in = jnp.where(tile_idx >= 1, pltpu.roll(inc_h, 1, axis=0), 0.0)
        for r in range(SUBLANE):
            c_ref[direction, pl.ds(r, ntile, stride=SUBLANE), :] = state_in

    def apply_state(c, carry):
        rows = pl.ds(pl.multiple_of(c * B_CHUNK, B_CHUNK), B_CHUNK)
        h = (h_ref[0, rows, :] + a_ref[0, rows, :] * c_ref[0, rows, :]
             + h_ref[1, rows, :] + a_ref[1, rows, :] * c_ref[1, rows, :])
        o_ref[rows, :] = (h * _silu(g_ref[rows, :].astype(F32))).astype(o_ref.dtype)
        return carry

    lax.fori_loop(0, nchunk, apply_state, 0)


def _mixer_b(p3, conv_w, conv_b, w_gate, b_gate, softplus_neg_lam):
    b, s, _ = p3.shape
    ngrp = B_WIDTH // LANE
    assert s % B_CHUNK == 0
    return pl.pallas_call(
        _b_kernel,
        grid=(b, ngrp),
        in_specs=[pl.BlockSpec((None, s, LANE), lambda bi, j: (bi, 0, U_BX + j)),
                  pl.BlockSpec((None, s, LANE), lambda bi, j: (bi, 0, U_BG + j)),
                  pl.BlockSpec((4, LANE), lambda bi, j: (0, j)),
                  pl.BlockSpec((1, LANE), lambda bi, j: (0, j)),
                  pl.BlockSpec((None, LANE, 4 * LANE), lambda bi, j: (j, 0, 0)),
                  pl.BlockSpec((None, 1, 4 * LANE), lambda bi, j: (j, 0, 0)),
                  pl.BlockSpec((2, LANE), lambda bi, j: (0, j))],
        out_specs=pl.BlockSpec((None, s, LANE), lambda bi, j: (bi, 0, j)),
        out_shape=jax.ShapeDtypeStruct((b, s, B_WIDTH), BF16),
        scratch_shapes=[pltpu.VMEM((s + 2 * B_PAD, LANE), F32)] + [pltpu.VMEM((2, s, LANE), F32)] * 3,
        compiler_params=_cparams("parallel", "parallel"),
        name="mixer_b",
    )(p3, p3, conv_w, conv_b, w_gate, b_gate, softplus_neg_lam)


def _b_gate_weights(w_r, b_r, w_i, b_i):
    per = LANE // B_BLOCK_DIM
    ngrp = B_WIDTH // LANE

    def blockdiag(w):
        w = w.reshape(2, ngrp, per, B_BLOCK_DIM, B_BLOCK_DIM)
        eye = jnp.eye(per, dtype=w.dtype)
        full = jnp.einsum("dgpcx,pq->dgpcqx", w, eye).reshape(2, ngrp, LANE, LANE)
        return full.transpose(1, 0, 2, 3)

    w = jnp.concatenate([blockdiag(w_r), blockdiag(w_i)], axis=-1)
    w = w.transpose(0, 2, 1, 3).reshape(ngrp, LANE, 4 * LANE).astype(BF16)
    bias = jnp.concatenate([b_r.reshape(2, ngrp, LANE), b_i.reshape(2, ngrp, LANE)], axis=-1)
    return w, bias.transpose(1, 0, 2).reshape(ngrp, 1, 4 * LANE)


def _attend_t(streams, tk, nk, s_ref, mt_ref, m_ref, acc_ref, tile_of=None):
    if isinstance(nk, int):
        assert nk % 2 == 0 and nk >= 2
    if tile_of is None:
        tile_of = lambda u: u
    for i in range(len(streams)):
        m_ref[i] = jnp.full(m_ref.shape[1:], NEG, F32)
        acc_ref[i] = jnp.zeros(acc_ref.shape[1:], F32)

    def key_rows(u):
        return pl.ds(pl.multiple_of(tile_of(u) * tk, tk), tk)

    def scores(u, slot):
        rows = key_rows(u)
        for i, st in enumerate(streams):
            s = jnp.dot(st["k_ref"][rows, :], st["q"](u), preferred_element_type=F32)
            s_ref[slot, i] = s
            mt_ref[slot, i] = jnp.max(s, axis=0, keepdims=True)

    def accumulate(u, slot, first=False):
        rows = key_rows(u)
        for i, st in enumerate(streams):
            s = s_ref[slot, i]
            if first and st.get("bias0") is not None:
                s = s + st["bias0"][...]
                m_tile = jnp.max(s, axis=0, keepdims=True)
            else:
                m_tile = mt_ref[slot, i]
            off = st["off"](u) if st.get("off") is not None else None
            m_old = m_ref[i]
            m_new = jnp.maximum(m_old, m_tile if off is None else m_tile + off)
            alpha = jnp.exp2(m_old - m_new)
            p = jnp.exp2(s - (m_new if off is None else m_new - off)).astype(BF16)
            acc_ref[i] = alpha * acc_ref[i] + jnp.dot(st["vt_ref"][:, rows], p, preferred_element_type=F32)
            m_ref[i] = m_new

    scores(0, 0)
    scores(1, 1)
    accumulate(0, 0, first=True)

    def body(j, carry):
        u = 1 + 2 * j
        scores(u + 1, 0)
        accumulate(u, 1)
        scores(u + 2, 1)
        accumulate(u + 1, 0)
        return carry

    pairs = (nk - 2) // 2 if isinstance(nk, int) else lax.shift_right_logical(nk - 2, 1)
    lax.fori_loop(0, pairs, body, 0)
    accumulate(nk - 1, 1)


C_AUG = 3
C_VROWS = C_VDIM + 16
C_MAPS = ((0, C_HALF), (1, 0))
C_UNDERFLOW = 128.0


def _c_kernel(q_ref, k_ref, v_ref, g_ref, lam_ref, subln_ref, o_ref,
              kaug_ref, vt_ref, dbias_ref, knorm_ref, s_ref, mt_ref, m_ref, acc_ref, *, tk, slopes, lam_init):
    h = pl.program_id(1)
    qi = pl.program_id(2)
    tq = q_ref.shape[0]
    s_len = k_ref.shape[0]
    nk = s_len // tk
    assert tq == tk
    lp = lam_ref[...]
    lam = (jnp.exp(jnp.sum(lp[0:1, :] * lp[1:2, :], axis=-1, keepdims=True))
           - jnp.exp(jnp.sum(lp[2:3, :] * lp[3:4, :], axis=-1, keepdims=True)) + lam_init)
    slope = jnp.where(h == 0, slopes[0], jnp.where(h == 1, slopes[1], jnp.where(h == 2, slopes[2], slopes[3])))
    c = slope.astype(F32) * LOG2E

    @pl.when(qi == 0)
    def _build_key_side():
        lane = lax.broadcasted_iota(jnp.int32, (1, LANE), 1)
        knorm2 = [jnp.zeros((tk, 1), F32), jnp.zeros((tk, 1), F32)]
        for ch in range(nk):
            rows = slice(ch * tk, (ch + 1) * tk)
            k = k_ref[rows, :]
            ksq = k.astype(F32) * k.astype(F32)
            for cmap in range(2):
                keep = (lane < C_HALF) if cmap == 0 else (lane >= C_HALF)
                knorm2[cmap] = jnp.maximum(knorm2[cmap], jnp.sum(jnp.where(keep, ksq, 0.0), axis=1, keepdims=True))
            jpos = (ch * tk + lax.broadcasted_iota(jnp.int32, (tk, 1), 0)).astype(F32)
            cj = c * jpos
            hi = cj.astype(BF16).astype(F32)
            mid = (cj - hi).astype(BF16).astype(F32)
            lo = (cj - hi - mid).astype(BF16).astype(F32)
            for cmap, base in C_MAPS:
                aug = jnp.where(lane == base, hi, jnp.where(lane == base + 1, mid,
                                                            jnp.where(lane == base + 2, lo, 0.0)))
                keep = (lane < C_HALF) if cmap == 0 else (lane >= C_HALF)
                kaug_ref[cmap, rows, :] = jnp.where(keep, k, aug.astype(BF16))
            vt_ref[0:C_VDIM, rows] = v_ref[rows, :].astype(F32).T.astype(BF16)
        r16 = lax.broadcasted_iota(jnp.int32, (C_VROWS - C_VDIM, s_len), 0)
        vt_ref[C_VDIM:C_VROWS, :] = jnp.where(r16 == 0, 1.0, 0.0).astype(BF16)
        ii = lax.broadcasted_iota(jnp.int32, (tk, tq), 1)
        jj = lax.broadcasted_iota(jnp.int32, (tk, tq), 0)
        dbias_ref[...] = -c * jnp.abs(ii - jj).astype(F32)
        for cmap in range(2):
            knorm_ref[cmap] = jnp.broadcast_to(jnp.max(knorm2[cmap], axis=0, keepdims=True), knorm_ref.shape[1:])

    q_t = (q_ref[...].astype(F32) * (C_HALF ** -0.5 * LOG2E)).T
    row = lax.broadcasted_iota(jnp.int32, (LANE, 1), 0)
    ci = c * (qi * tq + lax.broadcasted_iota(jnp.int32, (1, tq), 1)).astype(F32)

    qk_bound = jnp.zeros((1, 1), F32)
    for cmap in range(2):
        keep = (row < C_HALF) if cmap == 0 else (row >= C_HALF)
        qn2 = jnp.max(jnp.sum(jnp.where(keep, q_t * q_t, 0.0), axis=0, keepdims=True), axis=1, keepdims=True)
        qk_bound = jnp.maximum(qk_bound, jnp.sqrt(qn2 * knorm_ref[cmap][0:1, 0:1]))
    qk_bound = qk_bound * 1.01
    reach = ((2.0 * qk_bound + C_UNDERFLOW) / c - 1.0) * (1.0 / tk)
    reach = jnp.minimum(jnp.ceil(jnp.maximum(reach, 0.0)), float(nk)).astype(jnp.int32)
    w = jnp.max(reach)
    lo = jnp.maximum(qi - w, 0)
    hi = jnp.minimum(qi + w, nk - 1)
    n_win = hi - lo + 1
    odd = jnp.bitwise_and(n_win, 1)
    lo = lo - jnp.where(hi == nk - 1, odd, 0)
    n_steps = n_win + odd

    def tile_of(u):
        if isinstance(u, int) and u == 0:
            return qi
        t = lo + u - 1
        return t + (t >= qi).astype(jnp.int32)

    def sign_of(u):
        return jnp.where(tile_of(u) < qi, 1.0, -1.0).astype(F32)

    streams = []
    for cmap, base in C_MAPS:
        keep = (row < C_HALF) if cmap == 0 else (row >= C_HALF)
        q_base = jnp.where(keep, q_t, 0.0)
        aug_rows = jnp.where((row >= base) & (row < base + C_AUG), 1.0, 0.0)
        q_diag = q_base.astype(BF16)
        q_before = (q_base + aug_rows).astype(BF16)
        q_after = (q_base - aug_rows).astype(BF16)

        def q_of(u, q_diag=q_diag, q_before=q_before, q_after=q_after):
            if isinstance(u, int) and u == 0:
                return q_diag
            return jnp.where(tile_of(u) < qi, q_before, q_after)

        def off_of(u):
            if isinstance(u, int) and u == 0:
                return None
            return -sign_of(u) * ci

        streams.append(dict(q=q_of, k_ref=kaug_ref.at[cmap], vt_ref=vt_ref, off=off_of, bias0=dbias_ref))
    _attend_t(streams, tk, n_steps, s_ref, mt_ref, m_ref, acc_ref, tile_of)

    o_t = []
    for cmap in range(2):
        acc = acc_ref[cmap]
        o_t.append(acc[:C_VDIM, :] / acc[C_VDIM:C_VDIM + 1, :])
    o = (o_t[0] - lam * o_t[1]).T
    ms = jnp.mean(o * o, axis=-1, keepdims=True)
    o = o * lax.rsqrt(ms + EPS) * subln_ref[...] * (1.0 - lam_init)
    o_ref[...] = (o * _silu(g_ref[...].astype(F32))).astype(o_ref.dtype)


def _mixer_c(p3, lam_params, subln, layer, tq, tk):
    b, s, _ = p3.shape
    lam_init = 0.8 - 0.6 * math.exp(-0.3 * layer)
    slopes = tuple(2.0 ** (-8.0 * (i + 1) / C_HEADS) for i in range(C_HEADS))
    kern = functools.partial(_c_kernel, tk=tk, slopes=slopes, lam_init=lam_init)
    return pl.pallas_call(
        kern,
        grid=(b, C_HEADS, s // tq),
        in_specs=[pl.BlockSpec((None, tq, LANE), lambda bi, h, i: (bi, i, U_CQ + h)),
                  pl.BlockSpec((None, s, LANE), lambda bi, h, i: (bi, 0, U_CK + h)),
                  pl.BlockSpec((None, s, LANE), lambda bi, h, i: (bi, 0, U_CV + h)),
                  pl.BlockSpec((None, tq, LANE), lambda bi, h, i: (bi, i, U_CG + h)),
                  pl.BlockSpec((4, C_HALF), lambda bi, h, i: (0, 0)),
                  pl.BlockSpec((1, C_VDIM), lambda bi, h, i: (0, 0))],
        out_specs=pl.BlockSpec((None, tq, LANE), lambda bi, h, i: (bi, i, h)),
        out_shape=jax.ShapeDtypeStruct((b, s, C_WIDTH), BF16),
        scratch_shapes=[pltpu.VMEM((2, s, LANE), BF16), pltpu.VMEM((C_VROWS, s), BF16),
                        pltpu.VMEM((tk, tq), F32), pltpu.VMEM((2, SUBLANE, LANE), F32),
                        pltpu.VMEM((2, 2, tk, tq), F32),
                        pltpu.VMEM((2, 2, 1, tq), F32), pltpu.VMEM((2, 1, tq), F32),
                        pltpu.VMEM((2, C_VROWS, tq), F32)],
        compiler_params=_cparams("parallel", "parallel", "arbitrary"),
        name="mixer_c",
    )(p3, p3, p3, p3, lam_params, subln)


def _dprep_kernel(cq_ref, ckv_ref, kr_ref, qn_ref, kvn_ref, wqt_ref, wqst_ref, wk_ref, wvt_ref,
                  e1_ref, e2_ref, cos_ref, sin_ref, cost_ref, sint_ref, vonet_ref, qt_out, k_out, vt_out):
    def norm(x_ref, gain_ref):
        x = x_ref[...].astype(F32)
        ms = jnp.mean(x * x, axis=-1, keepdims=True)
        return (x * lax.rsqrt(ms + EPS) * gain_ref[...]).astype(BF16)

    cqn = norm(cq_ref, qn_ref)
    ckvn = norm(ckv_ref, kvn_ref)
    kr = kr_ref[...]
    cos = jnp.concatenate([cos_ref[...]] * D_HEADS, axis=1)
    sin = jnp.concatenate([sin_ref[...]] * D_HEADS, axis=1)
    cos_t = jnp.concatenate([cost_ref[...]] * D_HEADS, axis=0)
    sin_t = jnp.concatenate([sint_ref[...]] * D_HEADS, axis=0)
    q_t = (lax.dot_general(wqt_ref[...], cqn, _NT, preferred_element_type=F32) * cos_t
           + lax.dot_general(wqst_ref[...], cqn, _NT, preferred_element_type=F32) * sin_t)
    qt_out[...] = (q_t * ((D_NOPE + D_ROPE) ** -0.5 * LOG2E)).astype(qt_out.dtype)
    k = (jnp.dot(ckvn, wk_ref[...], preferred_element_type=F32)
         + jnp.dot(kr, e1_ref[...], preferred_element_type=F32) * cos
         + jnp.dot(kr, e2_ref[...], preferred_element_type=F32) * sin)
    k_out[...] = k.astype(k_out.dtype)
    v_t = lax.dot_general(wvt_ref[...], ckvn, _NT, preferred_element_type=F32) + vonet_ref[...]
    vt_out[...] = v_t.astype(vt_out.dtype)


def _d_tables(s):
    inv = ROPE_BASE ** (-jnp.arange(0, D_ROPE, 2, dtype=F32) / D_ROPE)
    ang = jnp.arange(s, dtype=F32)[:, None] * inv[None, :]
    cos, sin = jnp.cos(ang), jnp.sin(ang)
    pad = LANE - D_NOPE - D_ROPE
    cos_t = jnp.concatenate([jnp.ones((s, D_NOPE), F32), cos, cos, jnp.zeros((s, pad), F32)], axis=1)
    sin_t = jnp.concatenate([jnp.zeros((s, D_NOPE), F32), -sin, sin, jnp.zeros((s, pad), F32)], axis=1)
    return cos_t, sin_t


def _d_weights(w_uq, w_ukv):
    half = D_ROPE // 2
    pad = LANE - D_NOPE - D_ROPE
    wq = w_uq.reshape(D_QLR, D_HEADS, D_NOPE + D_ROPE)
    zq = jnp.zeros((D_QLR, D_HEADS, pad), w_uq.dtype)
    wq_main = jnp.concatenate([wq, zq], axis=-1).reshape(D_QLR, D_HEADS * LANE)
    wq_swap = jnp.concatenate([jnp.zeros((D_QLR, D_HEADS, D_NOPE), w_uq.dtype),
                               wq[..., D_NOPE + half:], wq[..., D_NOPE:D_NOPE + half], zq],
                              axis=-1).reshape(D_QLR, D_HEADS * LANE)
    wkv = w_ukv.reshape(D_KVLR, D_HEADS, D_NOPE + D_VDIM)
    zk = jnp.zeros((D_KVLR, D_HEADS, LANE - D_NOPE), w_ukv.dtype)
    wk = jnp.concatenate([wkv[..., :D_NOPE], zk], axis=-1).reshape(D_KVLR, D_HEADS * LANE)
    wv = jnp.concatenate([wkv[..., D_NOPE:], jnp.zeros((D_KVLR, D_HEADS, D_VROWS - D_VDIM), w_ukv.dtype)],
                         axis=-1).reshape(D_KVLR, D_HEADS * D_VROWS)
    src = jnp.arange(D_ROPE)
    e1 = jnp.zeros((LANE, LANE), F32).at[src, D_NOPE + src].set(1.0)
    e2 = jnp.zeros((LANE, LANE), F32).at[(src + half) % D_ROPE, D_NOPE + src].set(1.0)
    e1 = jnp.tile(e1, (1, D_HEADS))
    e2 = jnp.tile(e2, (1, D_HEADS))
    vone = jnp.zeros((D_VROWS, 1), F32).at[D_VDIM, 0].set(1.0)
    vone = jnp.tile(vone, (D_HEADS, 1))
    return (wq_main.T.astype(BF16), wq_swap.T.astype(BF16), wk.astype(BF16), wv.T.astype(BF16),
            e1.astype(BF16), e2.astype(BF16), vone)


def _dprep(p3, q_norm, kv_norm, dw, cos_t, sin_t, tm):
    b, s, _ = p3.shape
    wqt, wqst, wk, wvt, e1, e2, vonet = dw
    wide = D_HEADS * LANE
    vwide = D_HEADS * D_VROWS
    full = lambda shape: pl.BlockSpec(shape, lambda bi, i: (0,) * len(shape))
    tok_major = pl.BlockSpec((None, tm, wide), lambda bi, i: (bi, i, 0))
    feat_major = lambda rows: pl.BlockSpec((None, rows, tm), lambda bi, i: (bi, 0, i))
    return pl.pallas_call(
        _dprep_kernel,
        grid=(b, s // tm),
        in_specs=[pl.BlockSpec((None, tm, D_QLR), lambda bi, i: (bi, i, U_DCQ // 2)),
                  pl.BlockSpec((None, tm, LANE), lambda bi, i: (bi, i, U_DCKV)),
                  pl.BlockSpec((None, tm, LANE), lambda bi, i: (bi, i, U_DKR)),
                  full((1, D_QLR)), full((1, D_KVLR)),
                  full((wide, D_QLR)), full((wide, D_QLR)), full((D_KVLR, wide)), full((vwide, D_KVLR)),
                  full((LANE, wide)), full((LANE, wide)),
                  pl.BlockSpec((tm, LANE), lambda bi, i: (i, 0)),
                  pl.BlockSpec((tm, LANE), lambda bi, i: (i, 0)),
                  pl.BlockSpec((LANE, tm), lambda bi, i: (0, i)),
                  pl.BlockSpec((LANE, tm), lambda bi, i: (0, i)),
                  full((vwide, 1))],
        out_specs=[feat_major(wide), tok_major, feat_major(vwide)],
        out_shape=[jax.ShapeDtypeStruct((b, wide, s), BF16), jax.ShapeDtypeStruct((b, s, wide), BF16),
                   jax.ShapeDtypeStruct((b, vwide, s), BF16)],
        compiler_params=_cparams("parallel", "parallel"),
        name="mixer_d_prep",
    )(p3, p3, p3, q_norm, kv_norm, wqt, wqst, wk, wvt, e1, e2, cos_t, sin_t, cos_t.T, sin_t.T, vonet)


def _d_kernel(qt_ref, k_ref, vt_ref, g_ref, o_ref, s_ref, mt_ref, m_ref, acc_ref, *, tk):
    streams = []
    for h in range(2):
        sl = slice(h * LANE, (h + 1) * LANE)
        vrows = slice(h * D_VROWS, (h + 1) * D_VROWS)
        streams.append(dict(q=lambda u, q=qt_ref[sl, :]: q, k_ref=k_ref.at[:, sl], vt_ref=vt_ref.at[vrows, :]))
    _attend_t(streams, tk, k_ref.shape[0] // tk, s_ref, mt_ref, m_ref, acc_ref)
    parts = []
    for h in range(2):
        acc = acc_ref[h]
        parts.append(acc[:D_VDIM, :] / acc[D_VDIM:D_VDIM + 1, :])
    o = jnp.concatenate(parts, axis=0).T
    o_ref[...] = (o * _silu(g_ref[...].astype(F32))).astype(o_ref.dtype)


def _mixer_d(p3, qt, kd, vt, tq, tk):
    b, s, _ = p3.shape
    npair = D_HEADS // 2
    return pl.pallas_call(
        functools.partial(_d_kernel, tk=tk),
        grid=(b, npair, s // tq),
        in_specs=[pl.BlockSpec((None, 2 * LANE, tq), lambda bi, hp, i: (bi, hp, i)),
                  pl.BlockSpec((None, s, 2 * LANE), lambda bi, hp, i: (bi, 0, hp)),
                  pl.BlockSpec((None, 2 * D_VROWS, s), lambda bi, hp, i: (bi, hp, 0)),
                  pl.BlockSpec((None, tq, LANE), lambda bi, hp, i: (bi, i, U_DG + hp))],
        out_specs=pl.BlockSpec((None, tq, LANE), lambda bi, hp, i: (bi, i, hp)),
        out_shape=jax.ShapeDtypeStruct((b, s, D_WIDTH), BF16),
        scratch_shapes=[pltpu.VMEM((2, 2, tk, tq), F32), pltpu.VMEM((2, 2, 1, tq), F32),
                        pltpu.VMEM((2, 1, tq), F32), pltpu.VMEM((2, D_VROWS, tq), F32)],
        compiler_params=_cparams("parallel", "parallel", "arbitrary"),
        name="mixer_d",
    )(qt, kd, vt, p3)


def _out_kernel(x_ref, ya_ref, yb_ref, yc_ref, yd_ref, g0_ref, g1_ref, g2_ref, g3_ref,
                wa_ref, wb_ref, wc_ref, wd_ref, bg_ref, wo_ref, np_ref, o_ref):
    merged = None
    for i, (y_ref, w_ref, g_ref) in enumerate(((ya_ref, wa_ref, g0_ref), (yb_ref, wb_ref, g1_ref),
                                                (yc_ref, wc_ref, g2_ref), (yd_ref, wd_ref, g3_ref))):
        t = jnp.dot(y_ref[...], w_ref[...], preferred_element_type=F32)
        gate2 = jnp.tanh(0.5 * g_ref[...].astype(F32) + 0.5 * bg_ref[i:i + 1, :]) + 1.0
        merged = gate2 * t if merged is None else merged + gate2 * t
    o = jnp.dot((0.5 * merged).astype(BF16), wo_ref[...], preferred_element_type=F32)
    ms = jnp.mean(o * o, axis=-1, keepdims=True)
    o_ref[...] = x_ref[...] + o * lax.rsqrt(ms + EPS) * np_ref[...]


def _merge_out(x2, p2, ya, yb, yc, yd, wa, wb, wc, wd, b_gate, w_out, norm_post, layer, tm):
    n = x2.shape[0]
    row = lambda width: pl.BlockSpec((tm, width), lambda i: (i, 0))
    full = lambda shape: pl.BlockSpec((None,) + shape, lambda i: (layer, 0, 0))
    gate = lambda br: pl.BlockSpec((tm, D_MODEL), lambda i, br=br: (i, U_GATE * LANE // D_MODEL + br))
    return pl.pallas_call(
        _out_kernel,
        grid=(n // tm,),
        in_specs=[row(D_MODEL), row(A_WIDTH), row(B_WIDTH), row(C_WIDTH), row(D_WIDTH),
                  gate(0), gate(1), gate(2), gate(3),
                  full((A_WIDTH, D_MODEL)), full((B_WIDTH, D_MODEL)), full((C_WIDTH, D_MODEL)),
                  full((D_WIDTH, D_MODEL)), full((N_BRANCH, D_MODEL)), full((D_MODEL, D_MODEL)),
                  full((1, D_MODEL))],
        out_specs=row(D_MODEL),
        out_shape=jax.ShapeDtypeStruct((n, D_MODEL), F32),
        compiler_params=_cparams("parallel"),
        name="merge_out",
    )(x2, ya, yb, yc, yd, p2, p2, p2, p2, wa, wb, wc, wd, b_gate, w_out, norm_post)


def kernel(x, norm_pre, norm_post, w_in, conv_w, conv_b, lru_wr, lru_br, lru_wi, lru_bi, lru_lambda,
           diff_lam_q1, diff_lam_k1, diff_lam_q2, diff_lam_k2, diff_subln, mla_q_norm, mla_kv_norm,
           mla_w_uq, mla_w_ukv, w_br_a, w_br_b, w_br_c, w_br_d, b_gate, w_out):
    b, s, d = x.shape
    depth = w_in.shape[0]
    n = b * s
    tm_in = min(2048, n)
    tm_out = min(512, n)
    tq = min(512, s)
    tk = min(512, s)

    w_in_t = jnp.swapaxes(w_in, 1, 2)
    out_weights = [w.astype(BF16) for w in (w_br_a, w_br_b, w_br_c, w_br_d)]
    w_out_bf = w_out.astype(BF16)
    a_bias = _a_bias_tables(jnp.asarray([2.0 ** (-8.0 * (i + 1) / A_SLOTS) for i in range(A_SLOTS)], F32))
    cos_t, sin_t = _d_tables(s)
    softplus_neg_lam = jnp.log1p(jnp.exp(-lru_lambda.astype(F32)))

    x2 = x.reshape(n, d)
    for l in range(depth):
        p2 = _inproj(x2, norm_pre[l][None, :], w_in_t, l, tm_in)
        p3 = p2.reshape(b, s, P_WIDTH)
        ya = _mixer_a(p3, a_bias)
        bw, bb = _b_gate_weights(lru_wr[l], lru_br[l], lru_wi[l], lru_bi[l])
        yb = _mixer_b(p3, conv_w[l], conv_b[l][None, :], bw, bb, softplus_neg_lam[l])
        lam_params = jnp.stack([diff_lam_q1[l], diff_lam_k1[l], diff_lam_q2[l], diff_lam_k2[l]])
        yc = _mixer_c(p3, lam_params, diff_subln[l][None, :], l, tq, tk)
        dw = _d_weights(mla_w_uq[l], mla_w_ukv[l])
        qd, kd, vd = _dprep(p3, mla_q_norm[l][None, :], mla_kv_norm[l][None, :], dw, cos_t, sin_t, min(1024, s))
        yd = _mixer_d(p3, qd, kd, vd, min(2 * tq, s), tk)
        x2 = _merge_out(x2, p2, ya.reshape(n, -1), yb.reshape(n, -1), yc.reshape(n, -1), yd.reshape(n, -1),
                        *out_weights, b_gate, w_out_bf, norm_post[:, None, :], l, tm_out)
    return x2.reshape(b, s, d)
```

```python
import functools
import math

import jax
import jax.numpy as jnp
from jax import lax
from jax.experimental import pallas as pl
from jax.experimental.pallas import tpu as pltpu

F32 = jnp.float32
BF16 = jnp.bfloat16

D_MODEL = 1024
EPS = 1e-6
N_BRANCH = 4

A_PATTERNS = ((128, 1), (512, 4), (2048, 16))
A_SLOTS = 6
A_HEAD_DIM = 64
A_QKV = 1152
A_WIDTH = 384
A_RADIUS = 64
A_QBLK = 128
A_KWIN = 256
A_UNROLL = 16

B_WIDTH = 384
B_BLOCK_DIM = 64
B_C = 8.0

C_HEADS = 4
C_HALF = 64
C_VDIM = 128
C_QK = 512
C_WIDTH = 512

D_HEADS = 6
D_NOPE = 64
D_ROPE = 32
D_VDIM = 64
D_QLR = 256
D_KVLR = 128
D_WIDTH = 384
D_VROWS = D_VDIM + 16
ROPE_BASE = 10000.0

LANE = 128
SUBLANE = 8
NEG = -1e30
LOG2E = math.log2(math.e)
_NT = (((1,), (1,)), ((), ()))
VMEM_LIMIT = 56 * 1024 * 1024

IN_TN = 512
IN_MAIN_COLS = 3 * A_QKV + A_WIDTH + 2 * B_WIDTH + 2 * C_QK + 2 * C_WIDTH + D_QLR + D_KVLR
IN_DG_COL = IN_MAIN_COLS + D_ROPE
IN_GATE_COL = IN_DG_COL + D_WIDTH
IN_TAIL_TILES = (N_BRANCH * D_MODEL + D_WIDTH + LANE) // IN_TN
IN_MAIN_TILES = -(-(IN_MAIN_COLS + D_ROPE) // IN_TN)
U_GATE, U_DG = 0, N_BRANCH * D_MODEL // LANE
U_MAIN = IN_TAIL_TILES * IN_TN // LANE
U_AQ, U_AK, U_AV, U_AG = U_MAIN, U_MAIN + 9, U_MAIN + 18, U_MAIN + 27
U_BX, U_BG = U_MAIN + 30, U_MAIN + 33
U_CQ, U_CK, U_CV, U_CG = U_MAIN + 36, U_MAIN + 40, U_MAIN + 44, U_MAIN + 48
U_DCQ, U_DCKV, U_DKR = U_MAIN + 52, U_MAIN + 54, U_MAIN + 55
U_TOTAL = U_MAIN + IN_MAIN_TILES * IN_TN // LANE
P_WIDTH = U_TOTAL * LANE
assert U_TOTAL == 92 and U_DCQ % 2 == 0 and IN_GATE_COL + N_BRANCH * D_MODEL == 11552


def _cparams(*sem):
    return pltpu.CompilerParams(dimension_semantics=sem, vmem_limit_bytes=VMEM_LIMIT)


def _sigmoid(x):
    return 0.5 * jnp.tanh(0.5 * x) + 0.5


def _silu(x):
    return x * _sigmoid(x)


def _inproj_kernel(x_ref, g_ref, wtail_ref, wmain_ref, o_ref, h_ref):
    j = pl.program_id(1)

    @pl.when(j == 0)
    def _():
        x = x_ref[...]
        ms = jnp.mean(x * x, axis=-1, keepdims=True)
        h_ref[...] = (x * lax.rsqrt(ms + EPS) * g_ref[...]).astype(BF16)

    @pl.when(j < IN_TAIL_TILES)
    def _():
        o_ref[...] = lax.dot_general(h_ref[...], wtail_ref[0].astype(BF16), _NT,
                                     preferred_element_type=F32).astype(o_ref.dtype)

    @pl.when(j >= IN_TAIL_TILES)
    def _():
        o_ref[...] = lax.dot_general(h_ref[...], wmain_ref[...].astype(BF16), _NT,
                                     preferred_element_type=F32).astype(o_ref.dtype)


def _inproj(x2, gain, w_in_t, layer, tm):
    n = x2.shape[0]
    gate_tiles = IN_TAIL_TILES - 1

    def tail_row(j):
        per = IN_TN // D_ROPE
        unit = jnp.where(j < gate_tiles, IN_GATE_COL // D_ROPE + jnp.minimum(j, gate_tiles - 1) * per,
                         IN_DG_COL // D_ROPE)
        return unit * D_ROPE

    return pl.pallas_call(
        _inproj_kernel,
        grid=(n // tm, P_WIDTH // IN_TN),
        in_specs=[pl.BlockSpec((tm, D_MODEL), lambda i, j: (i, 0)),
                  pl.BlockSpec((1, D_MODEL), lambda i, j: (0, 0)),
                  pl.BlockSpec((pl.Element(1), pl.Element(IN_TN), pl.Element(D_MODEL)),
                               lambda i, j: (layer, tail_row(j), 0)),
                  pl.BlockSpec((None, IN_TN, D_MODEL),
                               lambda i, j: (layer, jnp.maximum(j - IN_TAIL_TILES, 0), 0))],
        out_specs=pl.BlockSpec((tm, IN_TN), lambda i, j: (i, j)),
        out_shape=jax.ShapeDtypeStruct((n, P_WIDTH), BF16),
        scratch_shapes=[pltpu.VMEM((tm, D_MODEL), BF16)],
        compiler_params=_cparams("parallel", "arbitrary"),
        name="inproj",
    )(x2, gain, w_in_t, w_in_t)


def _a_bias_tables(slopes):
    ii = jnp.arange(A_QBLK, dtype=jnp.int32)[:, None]
    jj = jnp.arange(A_KWIN, dtype=jnp.int32)[None, :]
    out = []
    for _, dil in A_PATTERNS:
        per_edge = []
        for off in (0, A_RADIUS, A_QBLK):
            rel = jnp.abs(off + ii - jj)
            dist = (rel * dil).astype(F32)
            b = -slopes[:, None, None] * dist[None] * LOG2E
            per_edge.append(jnp.where((rel <= A_RADIUS)[None], b, NEG))
        out.append(jnp.stack(per_edge))
    return jnp.stack(out)


def _a_pitch(dil):
    return dil + SUBLANE if dil % (2 * SUBLANE) == 0 else dil


def _a_kernel(q0, k0, v0, q1, k1, v1, q2, k2, v2, gate_ref, bias_ref, o_ref,
              qf, kf, vf, u_ref, z_ref, m_ref, up_ref, zp_ref, mp_ref):
    s_len = q0.shape[0]
    lane = lax.broadcasted_iota(jnp.int32, (1, LANE), 1)
    first = lane < A_HEAD_DIM
    ones_first = jnp.where(first, 1.0, 0.0).astype(BF16)
    ones_second = jnp.where(first, 0.0, 1.0).astype(BF16)
    scale = A_HEAD_DIM ** -0.5 * LOG2E
    padded_groups = [g for g, (_, dil) in enumerate(A_PATTERNS) if _a_pitch(dil) != dil]
    assert padded_groups == [len(A_PATTERNS) - 1]

    for g, ((_, dil), (qr, kr, vr)) in enumerate(zip(A_PATTERNS, ((q0, k0, v0), (q1, k1, v1), (q2, k2, v2)))):
        sub_len = s_len // dil
        nqb = sub_len // A_QBLK
        pitch = _a_pitch(dil)
        if pitch != dil:
            def stage(l2, carry, dil=dil, pitch=pitch, qr=qr, kr=kr, vr=vr):
                src = pl.ds(pl.multiple_of(l2 * 2 * dil, 2 * dil), 2 * dil)
                dst = pl.multiple_of(l2 * 2 * pitch, SUBLANE)
                for ref, buf, mul in ((qr, qf, scale), (kr, kf, None), (vr, vf, None)):
                    x = ref[src, :].astype(F32)
                    x = x if mul is None else x * mul
                    buf[pl.ds(dst, dil), :] = x[:dil]
                    buf[pl.ds(dst + pitch, dil), :] = x[dil:]
                return carry

            lax.fori_loop(0, sub_len // 2, stage, 0, unroll=4)
        elif dil > 1:
            rows = pl.ds(0, s_len)
            qf[rows, :] = qr[...].astype(F32) * scale
            kf[rows, :] = kr[...].astype(F32)
            vf[rows, :] = vr[...].astype(F32)

        def block(idx, carry, g=g, dil=dil, pitch=pitch, sub_len=sub_len, nqb=nqb, qr=qr, kr=kr, vr=vr):
            r = idx // nqb
            qb = idx % nqb
            qs = qb * A_QBLK
            ws = jnp.clip(qs - A_RADIUS, 0, sub_len - A_KWIN)
            edge = jnp.where(qb == 0, 0, jnp.where(qb == nqb - 1, 2, 1))
            if dil == 1:
                qrows = pl.ds(pl.multiple_of(qs, A_QBLK), A_QBLK)
                krows = pl.ds(pl.multiple_of(ws, A_RADIUS), A_KWIN)
                q = (qr[qrows, :].astype(F32) * scale).astype(BF16)
                k = kr[krows, :]
                v = vr[krows, :]
            else:
                qrows = pl.ds(r + qs * pitch, A_QBLK, stride=pitch)
                krows = pl.ds(r + ws * pitch, A_KWIN, stride=pitch)
                q = qf[qrows, :].astype(BF16)
                k = kf[krows, :].astype(BF16)
                v = vf[krows, :].astype(BF16)
            zq = jnp.zeros_like(q)
            zv = jnp.zeros_like(v)
            uz = None
            ms = []
            for h, head_lanes in enumerate((first, jnp.logical_not(first))):
                qh = jnp.where(head_lanes, q, zq)
                s = lax.dot_general(qh, k, (((1,), (1,)), ((), ())), preferred_element_type=F32)
                s = s + bias_ref[g, edge, h]
                mh = jnp.max(s, axis=-1, keepdims=True)
                p = jnp.exp2(s - mh).astype(BF16)
                ones_h = ones_first if h == 0 else ones_second
                vaug = jnp.concatenate([jnp.where(head_lanes, v, zv),
                                        jnp.broadcast_to(ones_h, v.shape)], axis=1)
                part = jnp.dot(p, vaug, preferred_element_type=F32)
                uz = part if uz is None else uz + part
                ms.append(mh)
            u = uz[:, :LANE]
            z = uz[:, LANE:]
            m = jnp.where(first, ms[0], ms[1])
            if g == 0:
                u_ref[qrows, :] = u
                z_ref[qrows, :] = z
                m_ref[qrows, :] = m
            elif pitch != dil:
                up_ref[qrows, :] = u
                zp_ref[qrows, :] = z
                mp_ref[qrows, :] = m
            else:
                m_old = m_ref[qrows, :]
                m_new = jnp.maximum(m_old, m)
                a = jnp.exp2(m_old - m_new)
                b = jnp.exp2(m - m_new)
                u_ref[qrows, :] = a * u_ref[qrows, :] + b * u
                z_ref[qrows, :] = a * z_ref[qrows, :] + b * z
                m_ref[qrows, :] = m_new
            return carry

        lax.fori_loop(0, dil * nqb, block, 0, unroll=A_UNROLL)

    dil = A_PATTERNS[padded_groups[0]][1]
    pitch = _a_pitch(dil)

    def finish(l2, carry):
        rows = pl.ds(pl.multiple_of(l2 * 2 * dil, 2 * dil), 2 * dil)
        src = pl.multiple_of(l2 * 2 * pitch, SUBLANE)

        def padded(ref):
            return jnp.concatenate([ref[pl.ds(src, dil), :], ref[pl.ds(src + pitch, dil), :]], axis=0)

        m_a, m_b = m_ref[rows, :], padded(mp_ref)
        m_new = jnp.maximum(m_a, m_b)
        a = jnp.exp2(m_a - m_new)
        b = jnp.exp2(m_b - m_new)
        u = a * u_ref[rows, :] + b * padded(up_ref)
        z = a * z_ref[rows, :] + b * padded(zp_ref)
        o_ref[rows, :] = (u / z * _silu(gate_ref[rows, :].astype(F32))).astype(o_ref.dtype)
        return carry

    lax.fori_loop(0, s_len // (2 * dil), finish, 0, unroll=4)


def _mixer_a(p3, bias):
    b, s, _ = p3.shape
    npair = A_SLOTS // 2
    assert s // A_PATTERNS[-1][1] >= A_KWIN
    padded_rows = max(s // dil * _a_pitch(dil) for _, dil in A_PATTERNS)

    def col(unit):
        return pl.BlockSpec((None, s, LANE), lambda bi, hp, unit=unit: (bi, 0, unit + hp))

    in_specs = []
    for g in range(len(A_PATTERNS)):
        for base in (U_AQ, U_AK, U_AV):
            in_specs.append(col(base + g * npair))
    in_specs.append(col(U_AG))
    in_specs.append(pl.BlockSpec((len(A_PATTERNS), 3, 2, A_QBLK, A_KWIN), lambda bi, hp: (0, 0, hp, 0, 0)))
    return pl.pallas_call(
        _a_kernel,
        grid=(b, npair),
        in_specs=in_specs,
        out_specs=pl.BlockSpec((None, s, LANE), lambda bi, hp: (bi, 0, hp)),
        out_shape=jax.ShapeDtypeStruct((b, s, A_WIDTH), BF16),
        scratch_shapes=([pltpu.VMEM((padded_rows, LANE), F32)] * 3 + [pltpu.VMEM((s, LANE), F32)] * 3
                        + [pltpu.VMEM((padded_rows, LANE), F32)] * 3),
        compiler_params=_cparams("parallel", "parallel"),
        name="mixer_a",
    )(*([p3] * 10), bias)


B_CHUNK = 256
B_PAD = 8


def _b_kernel(x_ref, g_ref, cw_ref, cb_ref, w_ref, bias_ref, sp_ref, o_ref, xs_ref, a_ref, h_ref, c_ref):
    s_len = x_ref.shape[0]
    nchunk = s_len // B_CHUNK
    ntile = s_len // SUBLANE
    zpad = jnp.zeros((B_PAD, LANE), F32)
    xs_ref[pl.ds(0, B_PAD), :] = zpad
    xs_ref[pl.ds(B_PAD + s_len, B_PAD), :] = zpad
    xs_ref[pl.ds(B_PAD, s_len), :] = x_ref[...].astype(F32)
    row_in_tile = lax.broadcasted_iota(jnp.int32, (1, SUBLANE, 1), 1)
    tile_idx = lax.broadcasted_iota(jnp.int32, (ntile, 1), 0)

    def scan_rows(acc_a, acc_h, axis, shifts, pos, reverse):
        n = acc_a.shape[axis]
        for sh in shifts:
            if reverse:
                sa = pltpu.roll(acc_a, n - sh, axis=axis)
                shh = pltpu.roll(acc_h, n - sh, axis=axis)
                keep = pos < n - sh
            else:
                sa = pltpu.roll(acc_a, sh, axis=axis)
                shh = pltpu.roll(acc_h, sh, axis=axis)
                keep = pos >= sh
            acc_h = acc_h + acc_a * jnp.where(keep, shh, 0.0)
            acc_a = acc_a * jnp.where(keep, sa, 1.0)
        return acc_a, acc_h

    tile_shifts = (1, 2, 4)
    summary_shifts = tuple(1 << e for e in range(ntile.bit_length() - 1))
    assert 1 << len(summary_shifts) == ntile

    def local_scan(c, carry):
        t0 = pl.multiple_of(c * B_CHUNK, B_CHUNK)
        win = xs_ref[pl.ds(t0, B_CHUNK + 2 * B_PAD), :]
        n = B_CHUNK + 2 * B_PAD
        xc = (cb_ref[...]
              + pltpu.roll(win, 1, axis=0) * cw_ref[0:1, :]
              + win * cw_ref[1:2, :]
              + pltpu.roll(win, n - 1, axis=0) * cw_ref[2:3, :]
              + pltpu.roll(win, n - 2, axis=0) * cw_ref[3:4, :])
        xc = xc[B_PAD:B_PAD + B_CHUNK, :]
        ri = jnp.dot(xc.astype(BF16), w_ref[...], preferred_element_type=F32) + bias_ref[...]
        rows = pl.ds(t0, B_CHUNK)
        for direction in (0, 1):
            r = _sigmoid(ri[:, (2 * direction) * LANE:(2 * direction + 1) * LANE])
            i = _sigmoid(ri[:, (2 * direction + 1) * LANE:(2 * direction + 2) * LANE])
            log_a = (-B_C) * r * sp_ref[direction:direction + 1, :]
            a = jnp.exp(log_a)
            var = -jnp.tanh(log_a) * (a * a + 1.0)
            u = jnp.where(var > 0.0, var * lax.rsqrt(var), 0.0) * (i * xc)
            tiles = (B_CHUNK // SUBLANE, SUBLANE, LANE)
            acc_a, acc_h = scan_rows(a.reshape(tiles), u.reshape(tiles), 1, tile_shifts, row_in_tile,
                                     direction == 1)
            a_ref[direction, rows, :] = acc_a.reshape(B_CHUNK, LANE)
            h_ref[direction, rows, :] = acc_h.reshape(B_CHUNK, LANE)
        return carry

    lax.fori_loop(0, nchunk, local_scan, 0)

    for direction in (0, 1):
        reverse = direction == 1
        edge = 0 if reverse else SUBLANE - 1
        sum_a = a_ref[direction, pl.ds(edge, ntile, stride=SUBLANE), :]
        sum_h = h_ref[direction, pl.ds(edge, ntile, stride=SUBLANE), :]
        _, inc_h = scan_rows(sum_a, sum_h, 0, summary_shifts, tile_idx, reverse)
        if reverse:
            state_in = jnp.where(tile_idx < ntile - 1, pltpu.roll(inc_h, ntile - 1, axis=0), 0.0)
        else:
            state_in = jnp.where(tile_idx >= 1, pltpu.roll(inc_h, 1, axis=0), 0.0)
        for r in range(SUBLANE):
            c_ref[direction, pl.ds(r, ntile, stride=SUBLANE), :] = state_in

    def apply_state(c, carry):
        rows = pl.ds(pl.multiple_of(c * B_CHUNK, B_CHUNK), B_CHUNK)
        h = (h_ref[0, rows, :] + a_ref[0, rows, :] * c_ref[0, rows, :]
             + h_ref[1, rows, :] + a_ref[1, rows, :] * c_ref[1, rows, :])
        o_ref[rows, :] = (h * _silu(g_ref[rows, :].astype(F32))).astype(o_ref.dtype)
        return carry

    lax.fori_loop(0, nchunk, apply_state, 0)


def _mixer_b(p3, conv_w, conv_b, w_gate, b_gate, softplus_neg_lam):
    b, s, _ = p3.shape
    ngrp = B_WIDTH // LANE
    assert s % B_CHUNK == 0
    return pl.pallas_call(
        _b_kernel,
        grid=(b, ngrp),
        in_specs=[pl.BlockSpec((None, s, LANE), lambda bi, j: (bi, 0, U_BX + j)),
                  pl.BlockSpec((None, s, LANE), lambda bi, j: (bi, 0, U_BG + j)),
                  pl.BlockSpec((4, LANE), lambda bi, j: (0, j)),
                  pl.BlockSpec((1, LANE), lambda bi, j: (0, j)),
                  pl.BlockSpec((None, LANE, 4 * LANE), lambda bi, j: (j, 0, 0)),
                  pl.BlockSpec((None, 1, 4 * LANE), lambda bi, j: (j, 0, 0)),
                  pl.BlockSpec((2, LANE), lambda bi, j: (0, j))],
        out_specs=pl.BlockSpec((None, s, LANE), lambda bi, j: (bi, 0, j)),
        out_shape=jax.ShapeDtypeStruct((b, s, B_WIDTH), BF16),
        scratch_shapes=[pltpu.VMEM((s + 2 * B_PAD, LANE), F32)] + [pltpu.VMEM((2, s, LANE), F32)] * 3,
        compiler_params=_cparams("parallel", "parallel"),
        name="mixer_b",
    )(p3, p3, conv_w, conv_b, w_gate, b_gate, softplus_neg_lam)


def _b_gate_weights(w_r, b_r, w_i, b_i):
    per = LANE // B_BLOCK_DIM
    ngrp = B_WIDTH // LANE

    def blockdiag(w):
        w = w.reshape(2, ngrp, per, B_BLOCK_DIM, B_BLOCK_DIM)
        eye = jnp.eye(per, dtype=w.dtype)
        full = jnp.einsum("dgpcx,pq->dgpcqx", w, eye).reshape(2, ngrp, LANE, LANE)
        return full.transpose(1, 0, 2, 3)

    w = jnp.concatenate([blockdiag(w_r), blockdiag(w_i)], axis=-1)
    w = w.transpose(0, 2, 1, 3).reshape(ngrp, LANE, 4 * LANE).astype(BF16)
    bias = jnp.concatenate([b_r.reshape(2, ngrp, LANE), b_i.reshape(2, ngrp, LANE)], axis=-1)
    return w, bias.transpose(1, 0, 2).reshape(ngrp, 1, 4 * LANE)


def _attend_t(streams, tk, nk, s_ref, mt_ref, m_ref, acc_ref, tile_of=None):
    if isinstance(nk, int):
        assert nk % 2 == 0 and nk >= 2
    if tile_of is None:
        tile_of = lambda u: u
    for i in range(len(streams)):
        m_ref[i] = jnp.full(m_ref.shape[1:], NEG, F32)
        acc_ref[i] = jnp.zeros(acc_ref.shape[1:], F32)

    def key_rows(u):
        return pl.ds(pl.multiple_of(tile_of(u) * tk, tk), tk)

    def scores(u, slot):
        rows = key_rows(u)
        for i, st in enumerate(streams):
            s = jnp.dot(st["k_ref"][rows, :], st["q"](u), preferred_element_type=F32)
            s_ref[slot, i] = s
            mt_ref[slot, i] = jnp.max(s, axis=0, keepdims=True)

    def accumulate(u, slot, first=False):
        rows = key_rows(u)
        for i, st in enumerate(streams):
            s = s_ref[slot, i]
            if first and st.get("bias0") is not None:
                s = s + st["bias0"][...]
                m_tile = jnp.max(s, axis=0, keepdims=True)
            else:
                m_tile = mt_ref[slot, i]
            off = st["off"](u) if st.get("off") is not None else None
            m_old = m_ref[i]
            m_new = jnp.maximum(m_old, m_tile if off is None else m_tile + off)
            alpha = jnp.exp2(m_old - m_new)
            p = jnp.exp2(s - (m_new if off is None else m_new - off)).astype(BF16)
            acc_ref[i] = alpha * acc_ref[i] + jnp.dot(st["vt_ref"][:, rows], p, preferred_element_type=F32)
            m_ref[i] = m_new

    scores(0, 0)
    scores(1, 1)
    accumulate(0, 0, first=True)

    def body(j, carry):
        u = 1 + 2 * j
        scores(u + 1, 0)
        accumulate(u, 1)
        scores(u + 2, 1)
        accumulate(u + 1, 0)
        return carry

    pairs = (nk - 2) // 2 if isinstance(nk, int) else lax.shift_right_logical(nk - 2, 1)
    lax.fori_loop(0, pairs, body, 0)
    accumulate(nk - 1, 1)


C_AUG = 3
C_VROWS = C_VDIM + 16
C_MAPS = ((0, C_HALF), (1, 0))
C_UNDERFLOW = 128.0


def _c_kernel(q_ref, k_ref, v_ref, g_ref, lam_ref, subln_ref, o_ref,
              kaug_ref, vt_ref, dbias_ref, knorm_ref, s_ref, mt_ref, m_ref, acc_ref, *, tk, slopes, lam_init):
    h = pl.program_id(1)
    qi = pl.program_id(2)
    tq = q_ref.shape[0]
    s_len = k_ref.shape[0]
    nk = s_len // tk
    assert tq == tk
    lp = lam_ref[...]
    lam = (jnp.exp(jnp.sum(lp[0:1, :] * lp[1:2, :], axis=-1, keepdims=True))
           - jnp.exp(jnp.sum(lp[2:3, :] * lp[3:4, :], axis=-1, keepdims=True)) + lam_init)
    slope = jnp.where(h == 0, slopes[0], jnp.where(h == 1, slopes[1], jnp.where(h == 2, slopes[2], slopes[3])))
    c = slope.astype(F32) * LOG2E

    @pl.when(qi == 0)
    def _build_key_side():
        lane = lax.broadcasted_iota(jnp.int32, (1, LANE), 1)
        dim = lax.broadcasted_iota(jnp.int32, (LANE, 1), 0)
        halves = jnp.where((dim < C_HALF) == (lane == 0), jnp.where(lane < 2, 1.0, 0.0), 0.0).astype(BF16)
        knorm2 = jnp.zeros((tk, LANE), F32)
        for ch in range(nk):
            rows = slice(ch * tk, (ch + 1) * tk)
            k = k_ref[rows, :]
            ksq = (k.astype(F32) * k.astype(F32)).astype(BF16)
            knorm2 = jnp.maximum(knorm2, jnp.dot(ksq, halves, preferred_element_type=F32))
            jpos = (ch * tk + lax.broadcasted_iota(jnp.int32, (tk, 1), 0)).astype(F32)
            cj = c * jpos
            hi = cj.astype(BF16).astype(F32)
            mid = (cj - hi).astype(BF16).astype(F32)
            lo = (cj - hi - mid).astype(BF16).astype(F32)
            for cmap, base in C_MAPS:
                aug = jnp.where(lane == base, hi, jnp.where(lane == base + 1, mid,
                                                            jnp.where(lane == base + 2, lo, 0.0)))
                keep = (lane < C_HALF) if cmap == 0 else (lane >= C_HALF)
                kaug_ref[cmap, rows, :] = jnp.where(keep, k, aug.astype(BF16))
            vt_ref[0:C_VDIM, rows] = v_ref[rows, :].astype(F32).T.astype(BF16)
        r16 = lax.broadcasted_iota(jnp.int32, (C_VROWS - C_VDIM, s_len), 0)
        vt_ref[C_VDIM:C_VROWS, :] = jnp.where(r16 == 0, 1.0, 0.0).astype(BF16)
        ii = lax.broadcasted_iota(jnp.int32, (tk, tq), 1)
        jj = lax.broadcasted_iota(jnp.int32, (tk, tq), 0)
        dbias_ref[...] = -c * jnp.abs(ii - jj).astype(F32)
        kmax = jnp.max(knorm2, axis=0, keepdims=True)
        for cmap in range(2):
            val = jnp.sum(jnp.where(lane == cmap, kmax, 0.0), axis=1, keepdims=True)
            knorm_ref[cmap] = jnp.broadcast_to(val, knorm_ref.shape[1:])

    q_t = (q_ref[...].astype(F32) * (C_HALF ** -0.5 * LOG2E)).T
    row = lax.broadcasted_iota(jnp.int32, (LANE, 1), 0)
    ci = c * (qi * tq + lax.broadcasted_iota(jnp.int32, (1, tq), 1)).astype(F32)

    qk_bound = jnp.zeros((1, 1), F32)
    for cmap in range(2):
        keep = (row < C_HALF) if cmap == 0 else (row >= C_HALF)
        qn2 = jnp.max(jnp.sum(jnp.where(keep, q_t * q_t, 0.0), axis=0, keepdims=True), axis=1, keepdims=True)
        qk_bound = jnp.maximum(qk_bound, jnp.sqrt(qn2 * knorm_ref[cmap][0:1, 0:1]))
    qk_bound = qk_bound * 1.02
    reach = ((2.0 * qk_bound + C_UNDERFLOW) / c - 1.0) * (1.0 / tk)
    reach = jnp.minimum(jnp.ceil(jnp.maximum(reach, 0.0)), float(nk)).astype(jnp.int32)
    w = jnp.max(reach)
    lo = jnp.maximum(qi - w, 0)
    hi = jnp.minimum(qi + w, nk - 1)
    n_win = hi - lo + 1
    odd = jnp.bitwise_and(n_win, 1)
    lo = lo - jnp.where(hi == nk - 1, odd, 0)
    n_steps = n_win + odd

    def tile_of(u):
        if isinstance(u, int) and u == 0:
            return qi
        t = lo + u - 1
        return t + (t >= qi).astype(jnp.int32)

    def sign_of(u):
        return jnp.where(tile_of(u) < qi, 1.0, -1.0).astype(F32)

    streams = []
    for cmap, base in C_MAPS:
        keep = (row < C_HALF) if cmap == 0 else (row >= C_HALF)
        q_base = jnp.where(keep, q_t, 0.0)
        aug_rows = jnp.where((row >= base) & (row < base + C_AUG), 1.0, 0.0)
        q_diag = q_base.astype(BF16)
        q_before = (q_base + aug_rows).astype(BF16)
        q_after = (q_base - aug_rows).astype(BF16)

        def q_of(u, q_diag=q_diag, q_before=q_before, q_after=q_after):
            if isinstance(u, int) and u == 0:
                return q_diag
            return jnp.where(tile_of(u) < qi, q_before, q_after)

        def off_of(u):
            if isinstance(u, int) and u == 0:
                return None
            return -sign_of(u) * ci

        streams.append(dict(q=q_of, k_ref=kaug_ref.at[cmap], vt_ref=vt_ref, off=off_of, bias0=dbias_ref))
    _attend_t(streams, tk, n_steps, s_ref, mt_ref, m_ref, acc_ref, tile_of)

    o_t = []
    for cmap in range(2):
        acc = acc_ref[cmap]
        o_t.append(acc[:C_VDIM, :] / acc[C_VDIM:C_VDIM + 1, :])
    o = (o_t[0] - lam * o_t[1]).T
    ms = jnp.mean(o * o, axis=-1, keepdims=True)
    o = o * lax.rsqrt(ms + EPS) * subln_ref[...] * (1.0 - lam_init)
    o_ref[...] = (o * _silu(g_ref[...].astype(F32))).astype(o_ref.dtype)


def _mixer_c(p3, lam_params, subln, layer, tq, tk):
    b, s, _ = p3.shape
    lam_init = 0.8 - 0.6 * math.exp(-0.3 * layer)
    slopes = tuple(2.0 ** (-8.0 * (i + 1) / C_HEADS) for i in range(C_HEADS))
    kern = functools.partial(_c_kernel, tk=tk, slopes=slopes, lam_init=lam_init)
    return pl.pallas_call(
        kern,
        grid=(b, C_HEADS, s // tq),
        in_specs=[pl.BlockSpec((None, tq, LANE), lambda bi, h, i: (bi, i, U_CQ + h)),
                  pl.BlockSpec((None, s, LANE), lambda bi, h, i: (bi, 0, U_CK + h)),
                  pl.BlockSpec((None, s, LANE), lambda bi, h, i: (bi, 0, U_CV + h)),
                  pl.BlockSpec((None, tq, LANE), lambda bi, h, i: (bi, i, U_CG + h)),
                  pl.BlockSpec((4, C_HALF), lambda bi, h, i: (0, 0)),
                  pl.BlockSpec((1, C_VDIM), lambda bi, h, i: (0, 0))],
        out_specs=pl.BlockSpec((None, tq, LANE), lambda bi, h, i: (bi, i, h)),
        out_shape=jax.ShapeDtypeStruct((b, s, C_WIDTH), BF16),
        scratch_shapes=[pltpu.VMEM((2, s, LANE), BF16), pltpu.VMEM((C_VROWS, s), BF16),
                        pltpu.VMEM((tk, tq), F32), pltpu.VMEM((2, SUBLANE, LANE), F32),
                        pltpu.VMEM((2, 2, tk, tq), F32),
                        pltpu.VMEM((2, 2, 1, tq), F32), pltpu.VMEM((2, 1, tq), F32),
                        pltpu.VMEM((2, C_VROWS, tq), F32)],
        compiler_params=_cparams("parallel", "parallel", "arbitrary"),
        name="mixer_c",
    )(p3, p3, p3, p3, lam_params, subln)


def _dprep_kernel(cq_ref, ckv_ref, kr_ref, qn_ref, kvn_ref, wqt_ref, wqst_ref, wk_ref, wvt_ref,
                  cos_ref, sin_ref, cost_ref, sint_ref, vonet_ref, qt_out, k_out, vt_out):
    def norm(x_ref, gain_ref):
        x = x_ref[...].astype(F32)
        ms = jnp.mean(x * x, axis=-1, keepdims=True)
        return (x * lax.rsqrt(ms + EPS) * gain_ref[...]).astype(BF16)

    cqn = norm(cq_ref, qn_ref)
    ckvn = norm(ckv_ref, kvn_ref)
    lane = lax.broadcasted_iota(jnp.int32, (1, LANE), 1)
    half = D_ROPE // 2
    in_rope = (lane >= D_NOPE) & (lane < D_NOPE + D_ROPE)
    kr = jnp.where(in_rope, pltpu.roll(kr_ref[...].astype(F32), D_NOPE, axis=1), 0.0)
    kr_swapped = jnp.where(lane < D_NOPE + half, pltpu.roll(kr, LANE - half, axis=1), pltpu.roll(kr, half, axis=1))
    k_rope = kr * cos_ref[...] + kr_swapped * sin_ref[...]
    cos_t = jnp.concatenate([cost_ref[...]] * D_HEADS, axis=0)
    sin_t = jnp.concatenate([sint_ref[...]] * D_HEADS, axis=0)
    q_t = (lax.dot_general(wqt_ref[...], cqn, _NT, preferred_element_type=F32) * cos_t
           + lax.dot_general(wqst_ref[...], cqn, _NT, preferred_element_type=F32) * sin_t)
    qt_out[...] = (q_t * ((D_NOPE + D_ROPE) ** -0.5 * LOG2E)).astype(qt_out.dtype)
    k = jnp.dot(ckvn, wk_ref[...], preferred_element_type=F32) + jnp.concatenate([k_rope] * D_HEADS, axis=1)
    k_out[...] = k.astype(k_out.dtype)
    v_t = lax.dot_general(wvt_ref[...], ckvn, _NT, preferred_element_type=F32) + vonet_ref[...]
    vt_out[...] = v_t.astype(vt_out.dtype)


def _d_tables(s):
    inv = ROPE_BASE ** (-jnp.arange(0, D_ROPE, 2, dtype=F32) / D_ROPE)
    ang = jnp.arange(s, dtype=F32)[:, None] * inv[None, :]
    cos, sin = jnp.cos(ang), jnp.sin(ang)
    pad = LANE - D_NOPE - D_ROPE
    cos_t = jnp.concatenate([jnp.ones((s, D_NOPE), F32), cos, cos, jnp.zeros((s, pad), F32)], axis=1)
    sin_t = jnp.concatenate([jnp.zeros((s, D_NOPE), F32), -sin, sin, jnp.zeros((s, pad), F32)], axis=1)
    return cos_t, sin_t


def _d_weights(w_uq, w_ukv):
    half = D_ROPE // 2
    pad = LANE - D_NOPE - D_ROPE
    wq = w_uq.reshape(D_QLR, D_HEADS, D_NOPE + D_ROPE)
    zq = jnp.zeros((D_QLR, D_HEADS, pad), w_uq.dtype)
    wq_main = jnp.concatenate([wq, zq], axis=-1).reshape(D_QLR, D_HEADS * LANE)
    wq_swap = jnp.concatenate([jnp.zeros((D_QLR, D_HEADS, D_NOPE), w_uq.dtype),
                               wq[..., D_NOPE + half:], wq[..., D_NOPE:D_NOPE + half], zq],
                              axis=-1).reshape(D_QLR, D_HEADS * LANE)
    wkv = w_ukv.reshape(D_KVLR, D_HEADS, D_NOPE + D_VDIM)
    zk = jnp.zeros((D_KVLR, D_HEADS, LANE - D_NOPE), w_ukv.dtype)
    wk = jnp.concatenate([wkv[..., :D_NOPE], zk], axis=-1).reshape(D_KVLR, D_HEADS * LANE)
    wv = jnp.concatenate([wkv[..., D_NOPE:], jnp.zeros((D_KVLR, D_HEADS, D_VROWS - D_VDIM), w_ukv.dtype)],
                         axis=-1).reshape(D_KVLR, D_HEADS * D_VROWS)
    vone = jnp.zeros((D_VROWS, 1), F32).at[D_VDIM, 0].set(1.0)
    vone = jnp.tile(vone, (D_HEADS, 1))
    return wq_main.T.astype(BF16), wq_swap.T.astype(BF16), wk.astype(BF16), wv.T.astype(BF16), vone


def _dprep(p3, q_norm, kv_norm, dw, cos_t, sin_t, tm):
    b, s, _ = p3.shape
    wqt, wqst, wk, wvt, vonet = dw
    wide = D_HEADS * LANE
    vwide = D_HEADS * D_VROWS
    full = lambda shape: pl.BlockSpec(shape, lambda bi, i: (0,) * len(shape))
    tok_major = pl.BlockSpec((None, tm, wide), lambda bi, i: (bi, i, 0))
    feat_major = lambda rows: pl.BlockSpec((None, rows, tm), lambda bi, i: (bi, 0, i))
    return pl.pallas_call(
        _dprep_kernel,
        grid=(b, s // tm),
        in_specs=[pl.BlockSpec((None, tm, D_QLR), lambda bi, i: (bi, i, U_DCQ // 2)),
                  pl.BlockSpec((None, tm, LANE), lambda bi, i: (bi, i, U_DCKV)),
                  pl.BlockSpec((None, tm, LANE), lambda bi, i: (bi, i, U_DKR)),
                  full((1, D_QLR)), full((1, D_KVLR)),
                  full((wide, D_QLR)), full((wide, D_QLR)), full((D_KVLR, wide)), full((vwide, D_KVLR)),
                  pl.BlockSpec((tm, LANE), lambda bi, i: (i, 0)),
                  pl.BlockSpec((tm, LANE), lambda bi, i: (i, 0)),
                  pl.BlockSpec((LANE, tm), lambda bi, i: (0, i)),
                  pl.BlockSpec((LANE, tm), lambda bi, i: (0, i)),
                  full((vwide, 1))],
        out_specs=[feat_major(wide), tok_major, feat_major(vwide)],
        out_shape=[jax.ShapeDtypeStruct((b, wide, s), BF16), jax.ShapeDtypeStruct((b, s, wide), BF16),
                   jax.ShapeDtypeStruct((b, vwide, s), BF16)],
        compiler_params=_cparams("parallel", "parallel"),
        name="mixer_d_prep",
    )(p3, p3, p3, q_norm, kv_norm, wqt, wqst, wk, wvt, cos_t, sin_t, cos_t.T, sin_t.T, vonet)


def _d_kernel(qt_ref, k_ref, vt_ref, g_ref, o_ref, s_ref, mt_ref, m_ref, acc_ref, *, tk):
    streams = []
    for h in range(2):
        sl = slice(h * LANE, (h + 1) * LANE)
        vrows = slice(h * D_VROWS, (h + 1) * D_VROWS)
        streams.append(dict(q=lambda u, q=qt_ref[sl, :]: q, k_ref=k_ref.at[:, sl], vt_ref=vt_ref.at[vrows, :]))
    _attend_t(streams, tk, k_ref.shape[0] // tk, s_ref, mt_ref, m_ref, acc_ref)
    parts = []
    for h in range(2):
        acc = acc_ref[h]
        parts.append(acc[:D_VDIM, :] / acc[D_VDIM:D_VDIM + 1, :])
    o = jnp.concatenate(parts, axis=0).T
    o_ref[...] = (o * _silu(g_ref[...].astype(F32))).astype(o_ref.dtype)


def _mixer_d(p3, qt, kd, vt, tq, tk):
    b, s, _ = p3.shape
    npair = D_HEADS // 2
    return pl.pallas_call(
        functools.partial(_d_kernel, tk=tk),
        grid=(b, npair, s // tq),
        in_specs=[pl.BlockSpec((None, 2 * LANE, tq), lambda bi, hp, i: (bi, hp, i)),
                  pl.BlockSpec((None, s, 2 * LANE), lambda bi, hp, i: (bi, 0, hp)),
                  pl.BlockSpec((None, 2 * D_VROWS, s), lambda bi, hp, i: (bi, hp, 0)),
                  pl.BlockSpec((None, tq, LANE), lambda bi, hp, i: (bi, i, U_DG + hp))],
        out_specs=pl.BlockSpec((None, tq, LANE), lambda bi, hp, i: (bi, i, hp)),
        out_shape=jax.ShapeDtypeStruct((b, s, D_WIDTH), BF16),
        scratch_shapes=[pltpu.VMEM((2, 2, tk, tq), F32), pltpu.VMEM((2, 2, 1, tq), F32),
                        pltpu.VMEM((2, 1, tq), F32), pltpu.VMEM((2, D_VROWS, tq), F32)],
        compiler_params=_cparams("parallel", "parallel", "arbitrary"),
        name="mixer_d",
    )(qt, kd, vt, p3)


def _out_kernel(x_ref, ya_ref, yb_ref, yc_ref, yd_ref, g0_ref, g1_ref, g2_ref, g3_ref,
                wa_ref, wb_ref, wc_ref, wd_ref, bg_ref, wo_ref, np_ref, o_ref):
    merged = None
    for i, (y_ref, w_ref, g_ref) in enumerate(((ya_ref, wa_ref, g0_ref), (yb_ref, wb_ref, g1_ref),
                                                (yc_ref, wc_ref, g2_ref), (yd_ref, wd_ref, g3_ref))):
        t = jnp.dot(y_ref[...], w_ref[...], preferred_element_type=F32)
        gate2 = jnp.tanh(0.5 * g_ref[...].astype(F32) + 0.5 * bg_ref[i:i + 1, :]) + 1.0
        merged = gate2 * t if merged is None else merged + gate2 * t
    o = jnp.dot((0.5 * merged).astype(BF16), wo_ref[...], preferred_element_type=F32)
    ms = jnp.mean(o * o, axis=-1, keepdims=True)
    o_ref[...] = x_ref[...] + o * lax.rsqrt(ms + EPS) * np_ref[...]


def _merge_out(x2, p2, ya, yb, yc, yd, wa, wb, wc, wd, b_gate, w_out, norm_post, layer, tm):
    n = x2.shape[0]
    row = lambda width: pl.BlockSpec((tm, width), lambda i: (i, 0))
    full = lambda shape: pl.BlockSpec((None,) + shape, lambda i: (layer, 0, 0))
    gate = lambda br: pl.BlockSpec((tm, D_MODEL), lambda i, br=br: (i, U_GATE * LANE // D_MODEL + br))
    return pl.pallas_call(
        _out_kernel,
        grid=(n // tm,),
        in_specs=[row(D_MODEL), row(A_WIDTH), row(B_WIDTH), row(C_WIDTH), row(D_WIDTH),
                  gate(0), gate(1), gate(2), gate(3),
                  full((A_WIDTH, D_MODEL)), full((B_WIDTH, D_MODEL)), full((C_WIDTH, D_MODEL)),
                  full((D_WIDTH, D_MODEL)), full((N_BRANCH, D_MODEL)), full((D_MODEL, D_MODEL)),
                  full((1, D_MODEL))],
        out_specs=row(D_MODEL),
        out_shape=jax.ShapeDtypeStruct((n, D_MODEL), F32),
        compiler_params=_cparams("parallel"),
        name="merge_out",
    )(x2, ya, yb, yc, yd, p2, p2, p2, p2, wa, wb, wc, wd, b_gate, w_out, norm_post)


def kernel(x, norm_pre, norm_post, w_in, conv_w, conv_b, lru_wr, lru_br, lru_wi, lru_bi, lru_lambda,
           diff_lam_q1, diff_lam_k1, diff_lam_q2, diff_lam_k2, diff_subln, mla_q_norm, mla_kv_norm,
           mla_w_uq, mla_w_ukv, w_br_a, w_br_b, w_br_c, w_br_d, b_gate, w_out):
    b, s, d = x.shape
    depth = w_in.shape[0]
    n = b * s
    tm_in = min(2048, n)
    tm_out = min(512, n)
    tq = min(512, s)
    tk = min(512, s)

    w_in_t = jnp.swapaxes(w_in, 1, 2)
    out_weights = [w.astype(BF16) for w in (w_br_a, w_br_b, w_br_c, w_br_d)]
    w_out_bf = w_out.astype(BF16)
    a_bias = _a_bias_tables(jnp.asarray([2.0 ** (-8.0 * (i + 1) / A_SLOTS) for i in range(A_SLOTS)], F32))
    cos_t, sin_t = _d_tables(s)
    softplus_neg_lam = jnp.log1p(jnp.exp(-lru_lambda.astype(F32)))

    x2 = x.reshape(n, d)
    for l in range(depth):
        p2 = _inproj(x2, norm_pre[l][None, :], w_in_t, l, tm_in)
        p3 = p2.reshape(b, s, P_WIDTH)
        ya = _mixer_a(p3, a_bias)
        bw, bb = _b_gate_weights(lru_wr[l], lru_br[l], lru_wi[l], lru_bi[l])
        yb = _mixer_b(p3, conv_w[l], conv_b[l][None, :], bw, bb, softplus_neg_lam[l])
        lam_params = jnp.stack([diff_lam_q1[l], diff_lam_k1[l], diff_lam_q2[l], diff_lam_k2[l]])
        yc = _mixer_c(p3, lam_params, diff_subln[l][None, :], l, tq, tk)
        dw = _d_weights(mla_w_uq[l], mla_w_ukv[l])
        qd, kd, vd = _dprep(p3, mla_q_norm[l][None, :], mla_kv_norm[l][None, :], dw, cos_t, sin_t, min(1024, s))
        yd = _mixer_d(p3, qd, kd, vd, min(2 * tq, s), tk)
        x2 = _merge_out(x2, p2, ya.reshape(n, -1), yb.reshape(n, -1), yc.reshape(n, -1), yd.reshape(n, -1),
                        *out_weights, b_gate, w_out_bf, norm_post[:, None, :], l, tm_out)
    return x2.reshape(b, s, d)
```

```python
import functools
import math

import jax
import jax.numpy as jnp
from jax import lax
from jax.experimental import pallas as pl
from jax.experimental.pallas import tpu as pltpu

F32 = jnp.float32
BF16 = jnp.bfloat16

D_MODEL = 1024
EPS = 1e-6
N_BRANCH = 4

A_PATTERNS = ((128, 1), (512, 4), (2048, 16))
A_SLOTS = 6
A_HEAD_DIM = 64
A_QKV = 1152
A_WIDTH = 384
A_RADIUS = 64
A_QBLK = 128
A_KWIN = 256
A_UNROLL = 16

B_WIDTH = 384
B_BLOCK_DIM = 64
B_C = 8.0

C_HEADS = 4
C_HALF = 64
C_VDIM = 128
C_QK = 512
C_WIDTH = 512

D_HEADS = 6
D_NOPE = 64
D_ROPE = 32
D_VDIM = 64
D_QLR = 256
D_KVLR = 128
D_WIDTH = 384
D_VROWS = D_VDIM + 16
ROPE_BASE = 10000.0

LANE = 128
SUBLANE = 8
NEG = -1e30
LOG2E = math.log2(math.e)
_NT = (((1,), (1,)), ((), ()))
VMEM_LIMIT = 56 * 1024 * 1024

IN_TN = 512
IN_MAIN_COLS = 3 * A_QKV + A_WIDTH + 2 * B_WIDTH + 2 * C_QK + 2 * C_WIDTH + D_QLR + D_KVLR
IN_DG_COL = IN_MAIN_COLS + D_ROPE
IN_GATE_COL = IN_DG_COL + D_WIDTH
IN_TAIL_TILES = (N_BRANCH * D_MODEL + D_WIDTH + LANE) // IN_TN
IN_MAIN_TILES = -(-(IN_MAIN_COLS + D_ROPE) // IN_TN)
U_GATE, U_DG = 0, N_BRANCH * D_MODEL // LANE
U_MAIN = IN_TAIL_TILES * IN_TN // LANE
U_AQ, U_AK, U_AV, U_AG = U_MAIN, U_MAIN + 9, U_MAIN + 18, U_MAIN + 27
U_BX, U_BG = U_MAIN + 30, U_MAIN + 33
U_CQ, U_CK, U_CV, U_CG = U_MAIN + 36, U_MAIN + 40, U_MAIN + 44, U_MAIN + 48
U_DCQ, U_DCKV, U_DKR = U_MAIN + 52, U_MAIN + 54, U_MAIN + 55
U_TOTAL = U_MAIN + IN_MAIN_TILES * IN_TN // LANE
P_WIDTH = U_TOTAL * LANE
assert U_TOTAL == 92 and U_DCQ % 2 == 0 and IN_GATE_COL + N_BRANCH * D_MODEL == 11552


def _cparams(*sem):
    return pltpu.CompilerParams(dimension_semantics=sem, vmem_limit_bytes=VMEM_LIMIT)


def _sigmoid(x):
    return 0.5 * jnp.tanh(0.5 * x) + 0.5


def _silu(x):
    return x * _sigmoid(x)


def _inproj_kernel(x_ref, g_ref, wtail_ref, wmain_ref, o_ref, h_ref):
    j = pl.program_id(1)

    @pl.when(j == 0)
    def _():
        x = x_ref[...]
        ms = jnp.mean(x * x, axis=-1, keepdims=True)
        h_ref[...] = (x * lax.rsqrt(ms + EPS) * g_ref[...]).astype(BF16)

    @pl.when(j < IN_TAIL_TILES)
    def _():
        o_ref[...] = lax.dot_general(h_ref[...], wtail_ref[0].astype(BF16), _NT,
                                     preferred_element_type=F32).astype(o_ref.dtype)

    @pl.when(j >= IN_TAIL_TILES)
    def _():
        o_ref[...] = lax.dot_general(h_ref[...], wmain_ref[...].astype(BF16), _NT,
                                     preferred_element_type=F32).astype(o_ref.dtype)


def _inproj(x2, gain, w_in_t, layer, tm):
    n = x2.shape[0]
    gate_tiles = IN_TAIL_TILES - 1

    def tail_row(j):
        per = IN_TN // D_ROPE
        unit = jnp.where(j < gate_tiles, IN_GATE_COL // D_ROPE + jnp.minimum(j, gate_tiles - 1) * per,
                         IN_DG_COL // D_ROPE)
        return unit * D_ROPE

    return pl.pallas_call(
        _inproj_kernel,
        grid=(n // tm, P_WIDTH // IN_TN),
        in_specs=[pl.BlockSpec((tm, D_MODEL), lambda i, j: (i, 0)),
                  pl.BlockSpec((1, D_MODEL), lambda i, j: (0, 0)),
                  pl.BlockSpec((pl.Element(1), pl.Element(IN_TN), pl.Element(D_MODEL)),
                               lambda i, j: (layer, tail_row(j), 0)),
                  pl.BlockSpec((None, IN_TN, D_MODEL),
                               lambda i, j: (layer, jnp.maximum(j - IN_TAIL_TILES, 0), 0))],
        out_specs=pl.BlockSpec((tm, IN_TN), lambda i, j: (i, j)),
        out_shape=jax.ShapeDtypeStruct((n, P_WIDTH), BF16),
        scratch_shapes=[pltpu.VMEM((tm, D_MODEL), BF16)],
        compiler_params=_cparams("parallel", "arbitrary"),
        name="inproj",
    )(x2, gain, w_in_t, w_in_t)


def _a_bias_tables(slopes):
    ii = jnp.arange(A_QBLK, dtype=jnp.int32)[:, None]
    jj = jnp.arange(A_KWIN, dtype=jnp.int32)[None, :]
    out = []
    for _, dil in A_PATTERNS:
        per_edge = []
        for off in (0, A_RADIUS, A_QBLK):
            rel = jnp.abs(off + ii - jj)
            dist = (rel * dil).astype(F32)
            b = -slopes[:, None, None] * dist[None] * LOG2E
            per_edge.append(jnp.where((rel <= A_RADIUS)[None], b, NEG))
        out.append(jnp.stack(per_edge))
    return jnp.stack(out)


def _a_pitch(dil):
    return dil + SUBLANE if dil % (2 * SUBLANE) == 0 else dil


def _a_kernel(q0, k0, v0, q1, k1, v1, q2, k2, v2, gate_ref, bias_ref, o_ref,
              qf, kf, vf, u_ref, z_ref, m_ref, up_ref, zp_ref, mp_ref):
    s_len = q0.shape[0]
    lane = lax.broadcasted_iota(jnp.int32, (1, LANE), 1)
    first = lane < A_HEAD_DIM
    ones_first = jnp.where(first, 1.0, 0.0).astype(BF16)
    ones_second = jnp.where(first, 0.0, 1.0).astype(BF16)
    scale = A_HEAD_DIM ** -0.5 * LOG2E
    padded_groups = [g for g, (_, dil) in enumerate(A_PATTERNS) if _a_pitch(dil) != dil]
    assert padded_groups == [len(A_PATTERNS) - 1]

    for g, ((_, dil), (qr, kr, vr)) in enumerate(zip(A_PATTERNS, ((q0, k0, v0), (q1, k1, v1), (q2, k2, v2)))):
        sub_len = s_len // dil
        nqb = sub_len // A_QBLK
        pitch = _a_pitch(dil)
        if pitch != dil:
            def stage(l2, carry, dil=dil, pitch=pitch, qr=qr, kr=kr, vr=vr):
                src = pl.ds(pl.multiple_of(l2 * 2 * dil, 2 * dil), 2 * dil)
                dst = pl.multiple_of(l2 * 2 * pitch, SUBLANE)
                for ref, buf, mul in ((qr, qf, scale), (kr, kf, None), (vr, vf, None)):
                    x = ref[src, :].astype(F32)
                    x = x if mul is None else x * mul
                    buf[pl.ds(dst, dil), :] = x[:dil]
                    buf[pl.ds(dst + pitch, dil), :] = x[dil:]
                return carry

            lax.fori_loop(0, sub_len // 2, stage, 0, unroll=4)
        elif dil > 1:
            rows = pl.ds(0, s_len)
            qf[rows, :] = qr[...].astype(F32) * scale
            kf[rows, :] = kr[...].astype(F32)
            vf[rows, :] = vr[...].astype(F32)

        def block(idx, carry, g=g, dil=dil, pitch=pitch, sub_len=sub_len, nqb=nqb, qr=qr, kr=kr, vr=vr):
            r = idx // nqb
            qb = idx % nqb
            qs = qb * A_QBLK
            ws = jnp.clip(qs - A_RADIUS, 0, sub_len - A_KWIN)
            edge = jnp.where(qb == 0, 0, jnp.where(qb == nqb - 1, 2, 1))
            if dil == 1:
                qrows = pl.ds(pl.multiple_of(qs, A_QBLK), A_QBLK)
                krows = pl.ds(pl.multiple_of(ws, A_RADIUS), A_KWIN)
                q = (qr[qrows, :].astype(F32) * scale).astype(BF16)
                k = kr[krows, :]
                v = vr[krows, :]
            else:
                qrows = pl.ds(r + qs * pitch, A_QBLK, stride=pitch)
                krows = pl.ds(r + ws * pitch, A_KWIN, stride=pitch)
                q = qf[qrows, :].astype(BF16)
                k = kf[krows, :].astype(BF16)
                v = vf[krows, :].astype(BF16)
            zq = jnp.zeros_like(q)
            zv = jnp.zeros_like(v)
            uz = None
            ms = []
            for h, head_lanes in enumerate((first, jnp.logical_not(first))):
                qh = jnp.where(head_lanes, q, zq)
                s = lax.dot_general(qh, k, (((1,), (1,)), ((), ())), preferred_element_type=F32)
                s = s + bias_ref[g, edge, h]
                mh = jnp.max(s, axis=-1, keepdims=True)
                p = jnp.exp2(s - mh).astype(BF16)
                ones_h = ones_first if h == 0 else ones_second
                vaug = jnp.concatenate([jnp.where(head_lanes, v, zv),
                                        jnp.broadcast_to(ones_h, v.shape)], axis=1)
                part = jnp.dot(p, vaug, preferred_element_type=F32)
                uz = part if uz is None else uz + part
                ms.append(mh)
            u = uz[:, :LANE]
            z = uz[:, LANE:]
            m = jnp.where(first, ms[0], ms[1])
            if g == 0:
                u_ref[qrows, :] = u
                z_ref[qrows, :] = z
                m_ref[qrows, :] = m
            elif pitch != dil:
                up_ref[qrows, :] = u
                zp_ref[qrows, :] = z
                mp_ref[qrows, :] = m
            else:
                m_old = m_ref[qrows, :]
                m_new = jnp.maximum(m_old, m)
                a = jnp.exp2(m_old - m_new)
                b = jnp.exp2(m - m_new)
                u_ref[qrows, :] = a * u_ref[qrows, :] + b * u
                z_ref[qrows, :] = a * z_ref[qrows, :] + b * z
                m_ref[qrows, :] = m_new
            return carry

        lax.fori_loop(0, dil * nqb, block, 0, unroll=A_UNROLL)

    dil = A_PATTERNS[padded_groups[0]][1]
    pitch = _a_pitch(dil)

    def finish(l2, carry):
        rows = pl.ds(pl.multiple_of(l2 * 2 * dil, 2 * dil), 2 * dil)
        src = pl.multiple_of(l2 * 2 * pitch, SUBLANE)

        def padded(ref):
            return jnp.concatenate([ref[pl.ds(src, dil), :], ref[pl.ds(src + pitch, dil), :]], axis=0)

        m_a, m_b = m_ref[rows, :], padded(mp_ref)
        m_new = jnp.maximum(m_a, m_b)
        a = jnp.exp2(m_a - m_new)
        b = jnp.exp2(m_b - m_new)
        u = a * u_ref[rows, :] + b * padded(up_ref)
        z = a * z_ref[rows, :] + b * padded(zp_ref)
        o_ref[rows, :] = (u / z * _silu(gate_ref[rows, :].astype(F32))).astype(o_ref.dtype)
        return carry

    lax.fori_loop(0, s_len // (2 * dil), finish, 0, unroll=4)


def _mixer_a(p3, bias):
    b, s, _ = p3.shape
    npair = A_SLOTS // 2
    assert s // A_PATTERNS[-1][1] >= A_KWIN
    padded_rows = max(s // dil * _a_pitch(dil) for _, dil in A_PATTERNS)

    def col(unit):
        return pl.BlockSpec((None, s, LANE), lambda bi, hp, unit=unit: (bi, 0, unit + hp))

    in_specs = []
    for g in range(len(A_PATTERNS)):
        for base in (U_AQ, U_AK, U_AV):
            in_specs.append(col(base + g * npair))
    in_specs.append(col(U_AG))
    in_specs.append(pl.BlockSpec((len(A_PATTERNS), 3, 2, A_QBLK, A_KWIN), lambda bi, hp: (0, 0, hp, 0, 0)))
    return pl.pallas_call(
        _a_kernel,
        grid=(b, npair),
        in_specs=in_specs,
        out_specs=pl.BlockSpec((None, s, LANE), lambda bi, hp: (bi, 0, hp)),
        out_shape=jax.ShapeDtypeStruct((b, s, A_WIDTH), BF16),
        scratch_shapes=([pltpu.VMEM((padded_rows, LANE), F32)] * 3 + [pltpu.VMEM((s, LANE), F32)] * 3
                        + [pltpu.VMEM((padded_rows, LANE), F32)] * 3),
        compiler_params=_cparams("parallel", "parallel"),
        name="mixer_a",
    )(*([p3] * 10), bias)


B_CHUNK = 256
B_PAD = 8


def _b_kernel(x_ref, g_ref, cw_ref, cb_ref, w_ref, bias_ref, sp_ref, o_ref, xs_ref, a_ref, h_ref, c_ref):
    s_len = x_ref.shape[0]
    nchunk = s_len // B_CHUNK
    ntile = s_len // SUBLANE
    zpad = jnp.zeros((B_PAD, LANE), F32)
    xs_ref[pl.ds(0, B_PAD), :] = zpad
    xs_ref[pl.ds(B_PAD + s_len, B_PAD), :] = zpad
    xs_ref[pl.ds(B_PAD, s_len), :] = x_ref[...].astype(F32)
    row_in_tile = lax.broadcasted_iota(jnp.int32, (1, SUBLANE, 1), 1)
    tile_idx = lax.broadcasted_iota(jnp.int32, (ntile, 1), 0)

    def scan_rows(acc_a, acc_h, axis, shifts, pos, reverse):
        n = acc_a.shape[axis]
        for sh in shifts:
            if reverse:
                sa = pltpu.roll(acc_a, n - sh, axis=axis)
                shh = pltpu.roll(acc_h, n - sh, axis=axis)
                keep = pos < n - sh
            else:
                sa = pltpu.roll(acc_a, sh, axis=axis)
                shh = pltpu.roll(acc_h, sh, axis=axis)
                keep = pos >= sh
            acc_h = acc_h + acc_a * jnp.where(keep, shh, 0.0)
            acc_a = acc_a * jnp.where(keep, sa, 1.0)
        return acc_a, acc_h

    tile_shifts = (1, 2, 4)
    summary_shifts = tuple(1 << e for e in range(ntile.bit_length() - 1))
    assert 1 << len(summary_shifts) == ntile

    def local_scan(c, carry):
        t0 = pl.multiple_of(c * B_CHUNK, B_CHUNK)
        win = xs_ref[pl.ds(t0, B_CHUNK + 2 * B_PAD), :]
        n = B_CHUNK + 2 * B_PAD
        xc = (cb_ref[...]
              + pltpu.roll(win, 1, axis=0) * cw_ref[0:1, :]
              + win * cw_ref[1:2, :]
              + pltpu.roll(win, n - 1, axis=0) * cw_ref[2:3, :]
              + pltpu.roll(win, n - 2, axis=0) * cw_ref[3:4, :])
        xc = xc[B_PAD:B_PAD + B_CHUNK, :]
        ri = jnp.dot(xc.astype(BF16), w_ref[...], preferred_element_type=F32) + bias_ref[...]
        rows = pl.ds(t0, B_CHUNK)
        for direction in (0, 1):
            r = _sigmoid(ri[:, (2 * direction) * LANE:(2 * direction + 1) * LANE])
            i = _sigmoid(ri[:, (2 * direction + 1) * LANE:(2 * direction + 2) * LANE])
            log_a = (-B_C) * r * sp_ref[direction:direction + 1, :]
            a = jnp.exp(log_a)
            var = -jnp.tanh(log_a) * (a * a + 1.0)
            u = jnp.where(var > 0.0, var * lax.rsqrt(var), 0.0) * (i * xc)
            tiles = (B_CHUNK // SUBLANE, SUBLANE, LANE)
            acc_a, acc_h = scan_rows(a.reshape(tiles), u.reshape(tiles), 1, tile_shifts, row_in_tile,
                                     direction == 1)
            a_ref[direction, rows, :] = acc_a.reshape(B_CHUNK, LANE)
            h_ref[direction, rows, :] = acc_h.reshape(B_CHUNK, LANE)
        return carry

    lax.fori_loop(0, nchunk, local_scan, 0)

    for direction in (0, 1):
        reverse = direction == 1
        edge = 0 if reverse else SUBLANE - 1
        sum_a = a_ref[direction, pl.ds(edge, ntile, stride=SUBLANE), :]
        sum_h = h_ref[direction, pl.ds(edge, ntile, stride=SUBLANE), :]
        _, inc_h = scan_rows(sum_a, sum_h, 0, summary_shifts, tile_idx, reverse)
        if reverse:
            state_in = jnp.where(tile_idx < ntile - 1, pltpu.roll(inc_h, ntile - 1, axis=0), 0.0)
        else:
            state_in = jnp.where(tile_idx >= 1, pltpu.roll(inc_h, 1, axis=0), 0.0)
        for r in range(SUBLANE):
            c_ref[direction, pl.ds(r, ntile, stride=SUBLANE), :] = state_in

    def apply_state(c, carry):
        rows = pl.ds(pl.multiple_of(c * B_CHUNK, B_CHUNK), B_CHUNK)
        h = (h_ref[0, rows, :] + a_ref[0, rows, :] * c_ref[0, rows, :]
             + h_ref[1, rows, :] + a_ref[1, rows, :] * c_ref[1, rows, :])
        o_ref[rows, :] = (h * _silu(g_ref[rows, :].astype(F32))).astype(o_ref.dtype)
        return carry

    lax.fori_loop(0, nchunk, apply_state, 0)


def _mixer_b(p3, conv_w, conv_b, w_gate, b_gate, softplus_neg_lam):
    b, s, _ = p3.shape
    ngrp = B_WIDTH // LANE
    assert s % B_CHUNK == 0
    return pl.pallas_call(
        _b_kernel,
        grid=(b, ngrp),
        in_specs=[pl.BlockSpec((None, s, LANE), lambda bi, j: (bi, 0, U_BX + j)),
                  pl.BlockSpec((None, s, LANE), lambda bi, j: (bi, 0, U_BG + j)),
                  pl.BlockSpec((4, LANE), lambda bi, j: (0, j)),
                  pl.BlockSpec((1, LANE), lambda bi, j: (0, j)),
                  pl.BlockSpec((None, LANE, 4 * LANE), lambda bi, j: (j, 0, 0)),
                  pl.BlockSpec((None, 1, 4 * LANE), lambda bi, j: (j, 0, 0)),
                  pl.BlockSpec((2, LANE), lambda bi, j: (0, j))],
        out_specs=pl.BlockSpec((None, s, LANE), lambda bi, j: (bi, 0, j)),
        out_shape=jax.ShapeDtypeStruct((b, s, B_WIDTH), BF16),
        scratch_shapes=[pltpu.VMEM((s + 2 * B_PAD, LANE), F32)] + [pltpu.VMEM((2, s, LANE), F32)] * 3,
        compiler_params=_cparams("parallel", "parallel"),
        name="mixer_b",
    )(p3, p3, conv_w, conv_b, w_gate, b_gate, softplus_neg_lam)


def _b_gate_weights(w_r, b_r, w_i, b_i):
    per = LANE // B_BLOCK_DIM
    ngrp = B_WIDTH // LANE

    def blockdiag(w):
        w = w.reshape(2, ngrp, per, B_BLOCK_DIM, B_BLOCK_DIM)
        eye = jnp.eye(per, dtype=w.dtype)
        full = jnp.einsum("dgpcx,pq->dgpcqx", w, eye).reshape(2, ngrp, LANE, LANE)
        return full.transpose(1, 0, 2, 3)

    w = jnp.concatenate([blockdiag(w_r), blockdiag(w_i)], axis=-1)
    w = w.transpose(0, 2, 1, 3).reshape(ngrp, LANE, 4 * LANE).astype(BF16)
    bias = jnp.concatenate([b_r.reshape(2, ngrp, LANE), b_i.reshape(2, ngrp, LANE)], axis=-1)
    return w, bias.transpose(1, 0, 2).reshape(ngrp, 1, 4 * LANE)


def _attend_t(streams, tk, nk, s_ref, mt_ref, m_ref, acc_ref, tile_of=None):
    if isinstance(nk, int):
        assert nk % 2 == 0 and nk >= 2
    if tile_of is None:
        tile_of = lambda u: u
    for i in range(len(streams)):
        m_ref[i] = jnp.full(m_ref.shape[1:], NEG, F32)
        acc_ref[i] = jnp.zeros(acc_ref.shape[1:], F32)

    def key_rows(u):
        return pl.ds(pl.multiple_of(tile_of(u) * tk, tk), tk)

    def scores(u, slot):
        rows = key_rows(u)
        for i, st in enumerate(streams):
            s = jnp.dot(st["k_ref"][rows, :], st["q"](u), preferred_element_type=F32)
            s_ref[slot, i] = s
            mt_ref[slot, i] = jnp.max(s, axis=0, keepdims=True)

    def accumulate(u, slot, first=False):
        rows = key_rows(u)
        for i, st in enumerate(streams):
            s = s_ref[slot, i]
            if first and st.get("bias0") is not None:
                s = s + st["bias0"][...]
                m_tile = jnp.max(s, axis=0, keepdims=True)
            else:
                m_tile = mt_ref[slot, i]
            off = st["off"](u) if st.get("off") is not None else None
            m_old = m_ref[i]
            m_new = jnp.maximum(m_old, m_tile if off is None else m_tile + off)
            alpha = jnp.exp2(m_old - m_new)
            p = jnp.exp2(s - (m_new if off is None else m_new - off)).astype(BF16)
            acc_ref[i] = alpha * acc_ref[i] + jnp.dot(st["vt_ref"][:, rows], p, preferred_element_type=F32)
            m_ref[i] = m_new

    scores(0, 0)
    scores(1, 1)
    accumulate(0, 0, first=True)

    def body(j, carry):
        u = 1 + 2 * j
        scores(u + 1, 0)
        accumulate(u, 1)
        scores(u + 2, 1)
        accumulate(u + 1, 0)
        return carry

    pairs = (nk - 2) // 2 if isinstance(nk, int) else lax.shift_right_logical(nk - 2, 1)
    lax.fori_loop(0, pairs, body, 0)
    accumulate(nk - 1, 1)


C_AUG = 3
C_VROWS = C_VDIM + 16
C_MAPS = ((0, C_HALF), (1, 0))
C_UNDERFLOW = 128.0


def _c_kernel(q_ref, k_ref, v_ref, g_ref, lam_ref, subln_ref, o_ref,
              kaug_ref, vt_ref, dbias_ref, knorm_ref, s_ref, mt_ref, m_ref, acc_ref, *, tk, slopes, lam_init):
    h = pl.program_id(1)
    qi = pl.program_id(2)
    tq = q_ref.shape[0]
    s_len = k_ref.shape[0]
    nk = s_len // tk
    assert tq == tk
    lp = lam_ref[...]
    lam = (jnp.exp(jnp.sum(lp[0:1, :] * lp[1:2, :], axis=-1, keepdims=True))
           - jnp.exp(jnp.sum(lp[2:3, :] * lp[3:4, :], axis=-1, keepdims=True)) + lam_init)
    slope = jnp.where(h == 0, slopes[0], jnp.where(h == 1, slopes[1], jnp.where(h == 2, slopes[2], slopes[3])))
    c = slope.astype(F32) * LOG2E

    @pl.when(qi == 0)
    def _build_key_side():
        lane = lax.broadcasted_iota(jnp.int32, (1, LANE), 1)
        dim = lax.broadcasted_iota(jnp.int32, (LANE, 1), 0)
        halves = jnp.where((dim < C_HALF) == (lane == 0), jnp.where(lane < 2, 1.0, 0.0), 0.0).astype(BF16)
        knorm2 = jnp.zeros((tk, LANE), F32)
        for ch in range(nk):
            rows = slice(ch * tk, (ch + 1) * tk)
            k = k_ref[rows, :]
            ksq = (k.astype(F32) * k.astype(F32)).astype(BF16)
            knorm2 = jnp.maximum(knorm2, jnp.dot(ksq, halves, preferred_element_type=F32))
            jpos = (ch * tk + lax.broadcasted_iota(jnp.int32, (tk, 1), 0)).astype(F32)
            cj = c * jpos
            hi = cj.astype(BF16).astype(F32)
            mid = (cj - hi).astype(BF16).astype(F32)
            lo = (cj - hi - mid).astype(BF16).astype(F32)
            for cmap, base in C_MAPS:
                aug = jnp.where(lane == base, hi, jnp.where(lane == base + 1, mid,
                                                            jnp.where(lane == base + 2, lo, 0.0)))
                keep = (lane < C_HALF) if cmap == 0 else (lane >= C_HALF)
                kaug_ref[cmap, rows, :] = jnp.where(keep, k, aug.astype(BF16))
            vt_ref[0:C_VDIM, rows] = v_ref[rows, :].astype(F32).T.astype(BF16)
        r16 = lax.broadcasted_iota(jnp.int32, (C_VROWS - C_VDIM, s_len), 0)
        vt_ref[C_VDIM:C_VROWS, :] = jnp.where(r16 == 0, 1.0, 0.0).astype(BF16)
        ii = lax.broadcasted_iota(jnp.int32, (tk, tq), 1)
        jj = lax.broadcasted_iota(jnp.int32, (tk, tq), 0)
        dbias_ref[...] = -c * jnp.abs(ii - jj).astype(F32)
        kmax = jnp.max(knorm2, axis=0, keepdims=True)
        for cmap in range(2):
            val = jnp.sum(jnp.where(lane == cmap, kmax, 0.0), axis=1, keepdims=True)
            knorm_ref[cmap] = jnp.broadcast_to(val, knorm_ref.shape[1:])

    q_t = (q_ref[...].astype(F32) * (C_HALF ** -0.5 * LOG2E)).T
    row = lax.broadcasted_iota(jnp.int32, (LANE, 1), 0)
    ci = c * (qi * tq + lax.broadcasted_iota(jnp.int32, (1, tq), 1)).astype(F32)

    qk_bound = jnp.zeros((1, 1), F32)
    for cmap in range(2):
        keep = (row < C_HALF) if cmap == 0 else (row >= C_HALF)
        qn2 = jnp.max(jnp.sum(jnp.where(keep, q_t * q_t, 0.0), axis=0, keepdims=True), axis=1, keepdims=True)
        qk_bound = jnp.maximum(qk_bound, jnp.sqrt(qn2 * knorm_ref[cmap][0:1, 0:1]))
    qk_bound = qk_bound * 1.02
    reach = ((2.0 * qk_bound + C_UNDERFLOW) / c - 1.0) * (1.0 / tk)
    reach = jnp.minimum(jnp.ceil(jnp.maximum(reach, 0.0)), float(nk)).astype(jnp.int32)
    w = jnp.max(reach)
    lo = jnp.maximum(qi - w, 0)
    hi = jnp.minimum(qi + w, nk - 1)
    n_win = hi - lo + 1
    odd = jnp.bitwise_and(n_win, 1)
    lo = lo - jnp.where(hi == nk - 1, odd, 0)
    n_steps = n_win + odd

    def tile_of(u):
        if isinstance(u, int) and u == 0:
            return qi
        t = lo + u - 1
        return t + (t >= qi).astype(jnp.int32)

    def sign_of(u):
        return jnp.where(tile_of(u) < qi, 1.0, -1.0).astype(F32)

    streams = []
    for cmap, base in C_MAPS:
        keep = (row < C_HALF) if cmap == 0 else (row >= C_HALF)
        q_base = jnp.where(keep, q_t, 0.0)
        aug_rows = jnp.where((row >= base) & (row < base + C_AUG), 1.0, 0.0)
        q_diag = q_base.astype(BF16)
        q_before = (q_base + aug_rows).astype(BF16)
        q_after = (q_base - aug_rows).astype(BF16)

        def q_of(u, q_diag=q_diag, q_before=q_before, q_after=q_after):
            if isinstance(u, int) and u == 0:
                return q_diag
            return jnp.where(tile_of(u) < qi, q_before, q_after)

        def off_of(u):
            if isinstance(u, int) and u == 0:
                return None
            return -sign_of(u) * ci

        streams.append(dict(q=q_of, k_ref=kaug_ref.at[cmap], vt_ref=vt_ref, off=off_of, bias0=dbias_ref))
    _attend_t(streams, tk, n_steps, s_ref, mt_ref, m_ref, acc_ref, tile_of)

    o_t = []
    for cmap in range(2):
        acc = acc_ref[cmap]
        o_t.append(acc[:C_VDIM, :] / acc[C_VDIM:C_VDIM + 1, :])
    o = (o_t[0] - lam * o_t[1]).T
    ms = jnp.mean(o * o, axis=-1, keepdims=True)
    o = o * lax.rsqrt(ms + EPS) * subln_ref[...] * (1.0 - lam_init)
    o_ref[...] = (o * _silu(g_ref[...].astype(F32))).astype(o_ref.dtype)


def _mixer_c(p3, lam_params, subln, layer, tq, tk):
    b, s, _ = p3.shape
    lam_init = 0.8 - 0.6 * math.exp(-0.3 * layer)
    slopes = tuple(2.0 ** (-8.0 * (i + 1) / C_HEADS) for i in range(C_HEADS))
    kern = functools.partial(_c_kernel, tk=tk, slopes=slopes, lam_init=lam_init)
    return pl.pallas_call(
        kern,
        grid=(b, C_HEADS, s // tq),
        in_specs=[pl.BlockSpec((None, tq, LANE), lambda bi, h, i: (bi, i, U_CQ + h)),
                  pl.BlockSpec((None, s, LANE), lambda bi, h, i: (bi, 0, U_CK + h)),
                  pl.BlockSpec((None, s, LANE), lambda bi, h, i: (bi, 0, U_CV + h)),
                  pl.BlockSpec((None, tq, LANE), lambda bi, h, i: (bi, i, U_CG + h)),
                  pl.BlockSpec((4, C_HALF), lambda bi, h, i: (0, 0)),
                  pl.BlockSpec((1, C_VDIM), lambda bi, h, i: (0, 0))],
        out_specs=pl.BlockSpec((None, tq, LANE), lambda bi, h, i: (bi, i, h)),
        out_shape=jax.ShapeDtypeStruct((b, s, C_WIDTH), BF16),
        scratch_shapes=[pltpu.VMEM((2, s, LANE), BF16), pltpu.VMEM((C_VROWS, s), BF16),
                        pltpu.VMEM((tk, tq), F32), pltpu.VMEM((2, SUBLANE, LANE), F32),
                        pltpu.VMEM((2, 2, tk, tq), F32),
                        pltpu.VMEM((2, 2, 1, tq), F32), pltpu.VMEM((2, 1, tq), F32),
                        pltpu.VMEM((2, C_VROWS, tq), F32)],
        compiler_params=_cparams("parallel", "parallel", "arbitrary"),
        name="mixer_c",
    )(p3, p3, p3, p3, lam_params, subln)


def _dprep_kernel(cq_ref, ckv_ref, kr_ref, qn_ref, kvn_ref, wqt_ref, wqst_ref, wk_ref, wvt_ref,
                  cos_ref, sin_ref, cost_ref, sint_ref, vonet_ref, qt_out, k_out, vt_out):
    def norm(x_ref, gain_ref):
        x = x_ref[...].astype(F32)
        ms = jnp.mean(x * x, axis=-1, keepdims=True)
        return (x * lax.rsqrt(ms + EPS) * gain_ref[...]).astype(BF16)

    cqn = norm(cq_ref, qn_ref)
    ckvn = norm(ckv_ref, kvn_ref)
    lane = lax.broadcasted_iota(jnp.int32, (1, LANE), 1)
    half = D_ROPE // 2
    in_rope = (lane >= D_NOPE) & (lane < D_NOPE + D_ROPE)
    kr = jnp.where(in_rope, pltpu.roll(kr_ref[...].astype(F32), D_NOPE, axis=1), 0.0)
    kr_swapped = jnp.where(lane < D_NOPE + half, pltpu.roll(kr, LANE - half, axis=1), pltpu.roll(kr, half, axis=1))
    k_rope = kr * cos_ref[...] + kr_swapped * sin_ref[...]
    cos_t = jnp.concatenate([cost_ref[...]] * D_HEADS, axis=0)
    sin_t = jnp.concatenate([sint_ref[...]] * D_HEADS, axis=0)
    q_t = (lax.dot_general(wqt_ref[...], cqn, _NT, preferred_element_type=F32) * cos_t
           + lax.dot_general(wqst_ref[...], cqn, _NT, preferred_element_type=F32) * sin_t)
    qt_out[...] = (q_t * ((D_NOPE + D_ROPE) ** -0.5 * LOG2E)).astype(qt_out.dtype)
    k = jnp.dot(ckvn, wk_ref[...], preferred_element_type=F32) + jnp.concatenate([k_rope] * D_HEADS, axis=1)
    k_out[...] = k.astype(k_out.dtype)
    v_t = lax.dot_general(wvt_ref[...], ckvn, _NT, preferred_element_type=F32) + vonet_ref[...]
    vt_out[...] = v_t.astype(vt_out.dtype)


def _d_tables(s):
    inv = ROPE_BASE ** (-jnp.arange(0, D_ROPE, 2, dtype=F32) / D_ROPE)
    ang = jnp.arange(s, dtype=F32)[:, None] * inv[None, :]
    cos, sin = jnp.cos(ang), jnp.sin(ang)
    pad = LANE - D_NOPE - D_ROPE
    cos_t = jnp.concatenate([jnp.ones((s, D_NOPE), F32), cos, cos, jnp.zeros((s, pad), F32)], axis=1)
    sin_t = jnp.concatenate([jnp.zeros((s, D_NOPE), F32), -sin, sin, jnp.zeros((s, pad), F32)], axis=1)
    return cos_t, sin_t


def _d_weights(w_uq, w_ukv):
    half = D_ROPE // 2
    pad = LANE - D_NOPE - D_ROPE
    wq = w_uq.reshape(D_QLR, D_HEADS, D_NOPE + D_ROPE)
    zq = jnp.zeros((D_QLR, D_HEADS, pad), w_uq.dtype)
    wq_main = jnp.concatenate([wq, zq], axis=-1).reshape(D_QLR, D_HEADS * LANE)
    wq_swap = jnp.concatenate([jnp.zeros((D_QLR, D_HEADS, D_NOPE), w_uq.dtype),
                               wq[..., D_NOPE + half:], wq[..., D_NOPE:D_NOPE + half], zq],
                              axis=-1).reshape(D_QLR, D_HEADS * LANE)
    wkv = w_ukv.reshape(D_KVLR, D_HEADS, D_NOPE + D_VDIM)
    zk = jnp.zeros((D_KVLR, D_HEADS, LANE - D_NOPE), w_ukv.dtype)
    wk = jnp.concatenate([wkv[..., :D_NOPE], zk], axis=-1).reshape(D_KVLR, D_HEADS * LANE)
    wv = jnp.concatenate([wkv[..., D_NOPE:], jnp.zeros((D_KVLR, D_HEADS, D_VROWS - D_VDIM), w_ukv.dtype)],
                         axis=-1).reshape(D_KVLR, D_HEADS * D_VROWS)
    vone = jnp.zeros((D_VROWS, 1), F32).at[D_VDIM, 0].set(1.0)
    vone = jnp.tile(vone, (D_HEADS, 1))
    return wq_main.T.astype(BF16), wq_swap.T.astype(BF16), wk.astype(BF16), wv.T.astype(BF16), vone


def _dprep(p3, q_norm, kv_norm, dw, cos_t, sin_t, tm):
    b, s, _ = p3.shape
    wqt, wqst, wk, wvt, vonet = dw
    wide = D_HEADS * LANE
    vwide = D_HEADS * D_VROWS
    full = lambda shape: pl.BlockSpec(shape, lambda bi, i: (0,) * len(shape))
    tok_major = pl.BlockSpec((None, tm, wide), lambda bi, i: (bi, i, 0))
    feat_major = lambda rows: pl.BlockSpec((None, rows, tm), lambda bi, i: (bi, 0, i))
    return pl.pallas_call(
        _dprep_kernel,
        grid=(b, s // tm),
        in_specs=[pl.BlockSpec((None, tm, D_QLR), lambda bi, i: (bi, i, U_DCQ // 2)),
                  pl.BlockSpec((None, tm, LANE), lambda bi, i: (bi, i, U_DCKV)),
                  pl.BlockSpec((None, tm, LANE), lambda bi, i: (bi, i, U_DKR)),
                  full((1, D_QLR)), full((1, D_KVLR)),
                  full((wide, D_QLR)), full((wide, D_QLR)), full((D_KVLR, wide)), full((vwide, D_KVLR)),
                  pl.BlockSpec((tm, LANE), lambda bi, i: (i, 0)),
                  pl.BlockSpec((tm, LANE), lambda bi, i: (i, 0)),
                  pl.BlockSpec((LANE, tm), lambda bi, i: (0, i)),
                  pl.BlockSpec((LANE, tm), lambda bi, i: (0, i)),
                  full((vwide, 1))],
        out_specs=[feat_major(wide), tok_major, feat_major(vwide)],
        out_shape=[jax.ShapeDtypeStruct((b, wide, s), BF16), jax.ShapeDtypeStruct((b, s, wide), BF16),
                   jax.ShapeDtypeStruct((b, vwide, s), BF16)],
        compiler_params=_cparams("parallel", "parallel"),
        name="mixer_d_prep",
    )(p3, p3, p3, q_norm, kv_norm, wqt, wqst, wk, wvt, cos_t, sin_t, cos_t.T, sin_t.T, vonet)


def _d_kernel(qt_ref, k_ref, vt_ref, g_ref, o_ref, s_ref, mt_ref, m_ref, acc_ref, *, tk):
    streams = []
    for h in range(2):
        sl = slice(h * LANE, (h + 1) * LANE)
        vrows = slice(h * D_VROWS, (h + 1) * D_VROWS)
        streams.append(dict(q=lambda u, q=qt_ref[sl, :]: q, k_ref=k_ref.at[:, sl], vt_ref=vt_ref.at[vrows, :]))
    _attend_t(streams, tk, k_ref.shape[0] // tk, s_ref, mt_ref, m_ref, acc_ref)
    parts = []
    for h in range(2):
        acc = acc_ref[h]
        parts.append(acc[:D_VDIM, :] / acc[D_VDIM:D_VDIM + 1, :])
    o = jnp.concatenate(parts, axis=0).T
    o_ref[...] = (o * _silu(g_ref[...].astype(F32))).astype(o_ref.dtype)


def _mixer_d(p3, qt, kd, vt, tq, tk):
    b, s, _ = p3.shape
    npair = D_HEADS // 2
    return pl.pallas_call(
        functools.partial(_d_kernel, tk=tk),
        grid=(b, npair, s // tq),
        in_specs=[pl.BlockSpec((None, 2 * LANE, tq), lambda bi, hp, i: (bi, hp, i)),
                  pl.BlockSpec((None, s, 2 * LANE), lambda bi, hp, i: (bi, 0, hp)),
                  pl.BlockSpec((None, 2 * D_VROWS, s), lambda bi, hp, i: (bi, hp, 0)),
                  pl.BlockSpec((None, tq, LANE), lambda bi, hp, i: (bi, i, U_DG + hp))],
        out_specs=pl.BlockSpec((None, tq, LANE), lambda bi, hp, i: (bi, i, hp)),
        out_shape=jax.ShapeDtypeStruct((b, s, D_WIDTH), BF16),
        scratch_shapes=[pltpu.VMEM((2, 2, tk, tq), F32), pltpu.VMEM((2, 2, 1, tq), F32),
                        pltpu.VMEM((2, 1, tq), F32), pltpu.VMEM((2, D_VROWS, tq), F32)],
        compiler_params=_cparams("parallel", "parallel", "arbitrary"),
        name="mixer_d",
    )(qt, kd, vt, p3)


def _out_kernel(x_ref, ya_ref, yb_ref, yc_ref, yd_ref, g0_ref, g1_ref, g2_ref, g3_ref,
                wa_ref, wb_ref, wc_ref, wd_ref, bg_ref, wo_ref, np_ref, o_ref):
    merged = None
    for i, (y_ref, w_ref, g_ref) in enumerate(((ya_ref, wa_ref, g0_ref), (yb_ref, wb_ref, g1_ref),
                                                (yc_ref, wc_ref, g2_ref), (yd_ref, wd_ref, g3_ref))):
        t = jnp.dot(y_ref[...], w_ref[...], preferred_element_type=F32)
        gate2 = jnp.tanh(0.5 * g_ref[...].astype(F32) + 0.5 * bg_ref[i:i + 1, :]) + 1.0
        merged = gate2 * t if merged is None else merged + gate2 * t
    o = jnp.dot((0.5 * merged).astype(BF16), wo_ref[...], preferred_element_type=F32)
    ms = jnp.mean(o * o, axis=-1, keepdims=True)
    o_ref[...] = x_ref[...] + o * lax.rsqrt(ms + EPS) * np_ref[...]


def _merge_out(x2, p2, ya, yb, yc, yd, wa, wb, wc, wd, b_gate, w_out, norm_post, layer, tm):
    n = x2.shape[0]
    row = lambda width: pl.BlockSpec((tm, width), lambda i: (i, 0))
    full = lambda shape: pl.BlockSpec((None,) + shape, lambda i: (layer, 0, 0))
    gate = lambda br: pl.BlockSpec((tm, D_MODEL), lambda i, br=br: (i, U_GATE * LANE // D_MODEL + br))
    return pl.pallas_call(
        _out_kernel,
        grid=(n // tm,),
        in_specs=[row(D_MODEL), row(A_WIDTH), row(B_WIDTH), row(C_WIDTH), row(D_WIDTH),
                  gate(0), gate(1), gate(2), gate(3),
                  full((A_WIDTH, D_MODEL)), full((B_WIDTH, D_MODEL)), full((C_WIDTH, D_MODEL)),
                  full((D_WIDTH, D_MODEL)), full((N_BRANCH, D_MODEL)), full((D_MODEL, D_MODEL)),
                  full((1, D_MODEL))],
        out_specs=row(D_MODEL),
        out_shape=jax.ShapeDtypeStruct((n, D_MODEL), F32),
        compiler_params=_cparams("parallel"),
        name="merge_out",
    )(x2, ya, yb, yc, yd, p2, p2, p2, p2, wa, wb, wc, wd, b_gate, w_out, norm_post)


def kernel(x, norm_pre, norm_post, w_in, conv_w, conv_b, lru_wr, lru_br, lru_wi, lru_bi, lru_lambda,
           diff_lam_q1, diff_lam_k1, diff_lam_q2, diff_lam_k2, diff_subln, mla_q_norm, mla_kv_norm,
           mla_w_uq, mla_w_ukv, w_br_a, w_br_b, w_br_c, w_br_d, b_gate, w_out):
    b, s, d = x.shape
    depth = w_in.shape[0]
    n = b * s
    tm_in = min(2048, n)
    tm_out = min(512, n)
    tq = min(512, s)
    tk = min(512, s)

    w_in_t = jnp.swapaxes(w_in, 1, 2)
    out_weights = [w.astype(BF16) for w in (w_br_a, w_br_b, w_br_c, w_br_d)]
    w_out_bf = w_out.astype(BF16)
    a_bias = _a_bias_tables(jnp.asarray([2.0 ** (-8.0 * (i + 1) / A_SLOTS) for i in range(A_SLOTS)], F32))
    cos_t, sin_t = _d_tables(s)
    softplus_neg_lam = jnp.log1p(jnp.exp(-lru_lambda.astype(F32)))

    x2 = x.reshape(n, d)
    for l in range(depth):
        p2 = _inproj(x2, norm_pre[l][None, :], w_in_t, l, tm_in)
        p3 = p2.reshape(b, s, P_WIDTH)
        ya = _mixer_a(p3, a_bias)
        bw, bb = _b_gate_weights(lru_wr[l], lru_br[l], lru_wi[l], lru_bi[l])
        yb = _mixer_b(p3, conv_w[l], conv_b[l][None, :], bw, bb, softplus_neg_lam[l])
        lam_params = jnp.stack([diff_lam_q1[l], diff_lam_k1[l], diff_lam_q2[l], diff_lam_k2[l]])
        yc = _mixer_c(p3, lam_params, diff_subln[l][None, :], l, tq, tk)
        dw = _d_weights(mla_w_uq[l], mla_w_ukv[l])
        qd, kd, vd = _dprep(p3, mla_q_norm[l][None, :], mla_kv_norm[l][None, :], dw, cos_t, sin_t, min(1024, s))
        yd = _mixer_d(p3, qd, kd, vd, min(4 * tq, s), tk)
        x2 = _merge_out(x2, p2, ya.reshape(n, -1), yb.reshape(n, -1), yc.reshape(n, -1), yd.reshape(n, -1),
                        *out_weights, b_gate, w_out_bf, norm_post[:, None, :], l, tm_out)
    return x2.reshape(b, s, d)
```
